```python
import math
import jax
import jax.numpy as jnp
from jax import lax
import numpy as np

D_MODEL = 2048
BATCH = 4
SEQ = 4096
DEPTH = 1
DEC_BATCH = 32
DEC_SEQ = 64
PAST_LEN = 4096

CHUNK = 64
Q_BLOCK = 128
N_RET_HEADS = 8
RET_KEY_DIM = 64
RET_VALUE_DIM = 128
RET_QK_WIDTH = N_RET_HEADS * RET_KEY_DIM
RET_V_WIDTH = N_RET_HEADS * RET_VALUE_DIM
N_DIFF_HEADS = 8
DIFF_HEAD_DIM = 64
DIFF_QK_WIDTH = N_DIFF_HEADS * 2 * DIFF_HEAD_DIM
DIFF_V_WIDTH = N_DIFF_HEADS * 2 * DIFF_HEAD_DIM
IN_SPLITS = (RET_QK_WIDTH, RET_QK_WIDTH, RET_V_WIDTH, RET_V_WIDTH,
             DIFF_QK_WIDTH, DIFF_QK_WIDTH, DIFF_V_WIDTH, D_MODEL, D_MODEL)
IN_WIDTH = 2 * RET_QK_WIDTH + 2 * RET_V_WIDTH + 2 * DIFF_QK_WIDTH + DIFF_V_WIDTH + 2 * D_MODEL
NUM_BUCKETS = 32
MAX_DISTANCE = 128
N_GROUPS = 8
EXPERTS_PER_GROUP = 8
N_EXPERTS = N_GROUPS * EXPERTS_PER_GROUP
TOP_K_INNER = 2
EXPERT_HIDDEN = 512
EXPERT_BLOCK = 128
NORM_EPS = 1e-6
NEG_INF = -1e30

kernel_name = 'hybrid_retention_diffattn_hmoe_stream_step'


def rms_norm(x, gain=None):
    xf = x.astype(jnp.float32)
    y = xf * lax.rsqrt(jnp.mean(xf * xf, axis=-1, keepdims=True) + NORM_EPS)
    if gain is not None:
        y = y * gain.astype(jnp.float32)
    return y.astype(x.dtype)


def rotary(x, pos):
    half = x.shape[-1] // 2
    theta = 10000.0 ** (-jnp.linspace(0.0, 1.0, half, dtype=jnp.float32))
    ang = pos.astype(jnp.float32)[:, None] * theta[None, :]
    cos = jnp.cos(ang)[None, :, None, :]
    sin = jnp.sin(ang)[None, :, None, :]
    x1 = x[..., :half].astype(jnp.float32)
    x2 = x[..., half:].astype(jnp.float32)
    return jnp.concatenate([x1 * cos - x2 * sin, x1 * sin + x2 * cos], axis=-1).astype(x.dtype)


def split_columns(z):
    parts, start = [], 0
    for width in IN_SPLITS:
        parts.append(z[..., start:start + width])
        start += width
    return parts


def project_inputs(h, w_in, pos):
    B, L, _ = h.shape
    q_r, k_r, v_r, g_r, q_d, k_d, v_d, gate_r, gate_d = split_columns(h @ w_in)
    q_r = rotary(q_r.reshape(B, L, N_RET_HEADS, RET_KEY_DIM), pos)
    k_r = rotary(k_r.reshape(B, L, N_RET_HEADS, RET_KEY_DIM), pos) * (RET_KEY_DIM ** -0.5)
    v_r = v_r.reshape(B, L, N_RET_HEADS, RET_VALUE_DIM)
    q_d = q_d.reshape(B, L, N_DIFF_HEADS, 2, DIFF_HEAD_DIM)
    k_d = k_d.reshape(B, L, N_DIFF_HEADS, 2, DIFF_HEAD_DIM)
    v_d = v_d.reshape(B, L, N_DIFF_HEADS, 2 * DIFF_HEAD_DIM)
    return q_r, k_r, v_r, g_r, q_d, k_d, v_d, gate_r, gate_d


def retention_log_decay():
    return jnp.log(1.0 - 2.0 ** (-5.0 - jnp.arange(N_RET_HEADS, dtype=jnp.float32)))


def retention_chunk(q, k, v, state, log_g):
    L = q.shape[1]
    j = jnp.arange(L, dtype=jnp.float32)
    diff = j[:, None] - j[None, :]
    decay = jnp.where(diff >= 0, jnp.exp(log_g[:, None, None] * jnp.maximum(diff, 0.0)), 0.0).astype(q.dtype)
    inner = jnp.exp(log_g[:, None] * (j + 1.0)).astype(q.dtype)
    outer = jnp.exp(log_g[:, None] * (L - 1.0 - j)).astype(q.dtype)
    chunk_decay = jnp.exp(log_g * L).astype(q.dtype)
    scores = jnp.einsum('blhd,bmhd->bhlm', q, k) * decay
    o = (jnp.einsum('bhlm,bmhe->blhe', scores, v)
         + jnp.einsum('blhd,bhde->blhe', q, state) * inner.T[None, :, :, None])
    new_state = (chunk_decay[None, :, None, None] * state
                 + jnp.einsum('bmhd,hm,bmhe->bhde', k, outer, v))
    return o, new_state.astype(state.dtype)


def retention_prompt(q, k, v, log_g):
    B, S, H, _ = v.shape

    def to_chunks(t):
        return t.reshape(B, S // CHUNK, CHUNK, *t.shape[2:]).swapaxes(0, 1)

    def step(state, qkv):
        o, state = retention_chunk(qkv[0], qkv[1], qkv[2], state, log_g)
        return state, o

    state0 = jnp.zeros((B, H, RET_KEY_DIM, RET_VALUE_DIM), v.dtype)
    final_state, o = lax.scan(step, state0, (to_chunks(q), to_chunks(k), to_chunks(v)))
    return o.swapaxes(0, 1).reshape(B, S, H, RET_VALUE_DIM), final_state


def relative_position_bias(q_pos, k_pos, table):
    rel = k_pos[None, :] - q_pos[:, None]
    half = NUM_BUCKETS // 2
    max_exact = half // 2
    n = jnp.abs(rel)
    log_ratio = jnp.log(jnp.maximum(n, 1).astype(jnp.float32) / max_exact) / math.log(MAX_DISTANCE / max_exact)
    large = jnp.minimum(max_exact + (log_ratio * (half - max_exact)).astype(jnp.int32), half - 1)
    bucket = (rel > 0).astype(jnp.int32) * half + jnp.where(n < max_exact, n, large)
    return jnp.moveaxis(table[bucket], -1, 0).astype(jnp.float32)


def diff_attention_block(q, k, v, q_pos, k_pos, bias_table, lam):
    logits = jnp.einsum('bqhmd,bkhmd->bhmqk', q, k).astype(jnp.float32) * (DIFF_HEAD_DIM ** -0.5)
    bias = relative_position_bias(q_pos, k_pos, bias_table)
    visible = (k_pos[None, :] // CHUNK) <= (q_pos[:, None] // CHUNK)
    logits = jnp.where(visible, logits + bias[None, :, None], NEG_INF)
    probs = jax.nn.softmax(logits, axis=-1)
    weights = probs[:, :, 0] - lam * probs[:, :, 1]
    return jnp.einsum('bhqk,bkhe->bqhe', weights.astype(v.dtype), v)


def diff_attention_prompt(q, k, v, bias_table, lam):
    B, S = q.shape[:2]
    n_blocks = S // Q_BLOCK
    q_blocks = q.reshape(B, n_blocks, Q_BLOCK, *q.shape[2:]).swapaxes(0, 1)
    starts = jnp.arange(n_blocks) * Q_BLOCK
    k_pos = jnp.arange(S)

    def one_block(args):
        qb, start = args
        return diff_attention_block(qb, k, v, start + jnp.arange(Q_BLOCK), k_pos, bias_table, lam)

    o = lax.map(one_block, (q_blocks, starts))
    return o.swapaxes(0, 1).reshape(B, S, N_DIFF_HEADS, 2 * DIFF_HEAD_DIM)


def routed_ffn(h, w_group_router, b_group_router, w_expert_router, b_expert_router,
               w_expert_gate, w_expert_up, w_expert_down):
    n_tok, d = h.shape
    group_logits = (h @ w_group_router).astype(jnp.float32) + b_group_router.astype(jnp.float32)
    group_prob = jax.nn.softmax(group_logits, axis=-1)
    group = jnp.argmax(group_logits, axis=-1)
    group_weight = jnp.take_along_axis(group_prob, group[:, None], axis=-1)
    expert_logits = jnp.einsum('td,gde->tge', h, w_expert_router).astype(jnp.float32)
    expert_logits = (jnp.take_along_axis(expert_logits, group[:, None, None], axis=1)[:, 0]
                     + b_expert_router.astype(jnp.float32)[group])
    top_logit, top_idx = lax.top_k(expert_logits, TOP_K_INNER)
    gate = group_weight * jax.nn.softmax(top_logit, axis=-1)
    expert_id = (group[:, None] * EXPERTS_PER_GROUP + top_idx).reshape(-1)
    n_assign = n_tok * TOP_K_INNER
    token_id = (jnp.arange(n_assign) // TOP_K_INNER).astype(jnp.int32)
    order = jnp.argsort(expert_id)
    e_sorted = expert_id[order]
    counts = jnp.bincount(expert_id, length=N_EXPERTS)
    padded = (counts + EXPERT_BLOCK - 1) // EXPERT_BLOCK * EXPERT_BLOCK
    pad_end = jnp.cumsum(padded)
    dest = (pad_end - padded)[e_sorted] + jnp.arange(n_assign) - (jnp.cumsum(counts) - counts)[e_sorted]
    n_rows = -(-n_assign // EXPERT_BLOCK) * EXPERT_BLOCK + N_EXPERTS * EXPERT_BLOCK
    n_blocks = n_rows // EXPERT_BLOCK
    row_token = jnp.full((n_rows,), n_tok, jnp.int32).at[dest].set(token_id[order])
    row_gate = jnp.zeros((n_rows,), jnp.float32).at[dest].set(gate.reshape(-1)[order])
    block_expert = jnp.minimum(
        jnp.searchsorted(pad_end, jnp.arange(n_blocks) * EXPERT_BLOCK, side='right'), N_EXPERTS - 1)
    h_pad = jnp.concatenate([h, jnp.zeros((1, d), h.dtype)], axis=0)
    rows = h_pad[row_token].reshape(n_blocks, EXPERT_BLOCK, d)

    def expert_block(args):
        xb, e = args
        return (jax.nn.silu(xb @ w_expert_gate[e]) * (xb @ w_expert_up[e])) @ w_expert_down[e]

    out_rows = lax.map(expert_block, (rows, block_expert)).reshape(n_rows, d)
    out = jax.ops.segment_sum(out_rows * row_gate[:, None].astype(out_rows.dtype), row_token,
                              num_segments=n_tok + 1)
    return out[:n_tok]


def trunk_layer(x, pos, ret_state, k_past, v_past, layer, bias_table,
                norm_mix_gain, w_in, lambda_q1, lambda_k1, lambda_q2, lambda_k2, diff_subln_gain,
                w_ret_out, w_diff_out, w_out, norm_ffn_gain,
                w_group_router, b_group_router, w_expert_router, b_expert_router,
                w_expert_gate, w_expert_up, w_expert_down):
    B, L, _ = x.shape
    log_g = retention_log_decay()
    lam_init = 0.8 - 0.6 * math.exp(-0.3 * layer)
    lam = (jnp.exp(jnp.sum(lambda_q1.astype(jnp.float32) * lambda_k1.astype(jnp.float32)))
           - jnp.exp(jnp.sum(lambda_q2.astype(jnp.float32) * lambda_k2.astype(jnp.float32))) + lam_init)
    h = rms_norm(x, norm_mix_gain)
    q_r, k_r, v_r, g_r, q_d, k_d, v_d, gate_r, gate_d = project_inputs(h, w_in, pos)
    if ret_state is None:
        o_r, new_ret = retention_prompt(q_r, k_r, v_r, log_g)
        o_d = diff_attention_prompt(q_d, k_d, v_d, bias_table, lam)
    else:
        o_r, new_ret = retention_chunk(q_r, k_r, v_r, ret_state, log_g)
        k_all = jnp.concatenate([k_past, k_d.astype(k_past.dtype)], axis=1)
        v_all = jnp.concatenate([v_past, v_d.astype(v_past.dtype)], axis=1)
        o_d = diff_attention_block(q_d, k_all, v_all, pos, jnp.arange(k_all.shape[1]), bias_table, lam)
    o_r = rms_norm(o_r).reshape(B, L, RET_V_WIDTH) * jax.nn.silu(g_r)
    o_d = (rms_norm(o_d, diff_subln_gain) * (1.0 - lam_init)).reshape(B, L, DIFF_V_WIDTH)
    merged = jax.nn.sigmoid(gate_r) * (o_r @ w_ret_out) + jax.nn.sigmoid(gate_d) * (o_d @ w_diff_out)
    x = x + merged @ w_out
    h2 = rms_norm(x, norm_ffn_gain).reshape(B * L, D_MODEL)
    x = x + routed_ffn(h2, w_group_router, b_group_router, w_expert_router, b_expert_router,
                       w_expert_gate, w_expert_up, w_expert_down).reshape(B, L, D_MODEL)
    return x, k_d, v_d, new_ret


def setup_inputs(seed: int = 0) -> dict:
    key = jax.random.key(seed)
    ks = jax.random.split(key, 25)

    def normal(k, shape, scale):
        return jax.random.normal(k, shape, jnp.float32) * scale

    return {
        'x_prompt': normal(ks[0], (BATCH, SEQ, D_MODEL), 1.0),
        'x_sample': normal(ks[1], (DEC_BATCH, DEC_SEQ, D_MODEL), 1.0),
        'cache_diff_k': normal(ks[2], (DEPTH, DEC_BATCH, PAST_LEN, N_DIFF_HEADS, 2, DIFF_HEAD_DIM), 1.0),
        'cache_diff_v': normal(ks[3], (DEPTH, DEC_BATCH, PAST_LEN, N_DIFF_HEADS, 2 * DIFF_HEAD_DIM), 1.0),
        'state_retention': normal(ks[4], (DEPTH, DEC_BATCH, N_RET_HEADS, RET_KEY_DIM, RET_VALUE_DIM), 0.5),
        'norm_mix_gain': 1.0 + normal(ks[5], (DEPTH, D_MODEL), 0.1),
        'w_in': normal(ks[6], (DEPTH, D_MODEL, IN_WIDTH), D_MODEL ** -0.5),
        'lambda_q1': normal(ks[7], (DEPTH, DIFF_HEAD_DIM), 0.1),
        'lambda_k1': normal(ks[8], (DEPTH, DIFF_HEAD_DIM), 0.1),
        'lambda_q2': normal(ks[9], (DEPTH, DIFF_HEAD_DIM), 0.1),
        'lambda_k2': normal(ks[10], (DEPTH, DIFF_HEAD_DIM), 0.1),
        'diff_subln_gain': 1.0 + normal(ks[11], (DEPTH, 2 * DIFF_HEAD_DIM), 0.1),
        'w_ret_out': normal(ks[12], (DEPTH, RET_V_WIDTH, D_MODEL), RET_V_WIDTH ** -0.5),
        'w_diff_out': normal(ks[13], (DEPTH, DIFF_V_WIDTH, D_MODEL), DIFF_V_WIDTH ** -0.5),
        'w_out': normal(ks[14], (DEPTH, D_MODEL, D_MODEL), D_MODEL ** -0.5),
        'rel_bias_table': normal(ks[15], (NUM_BUCKETS, N_DIFF_HEADS), 0.5),
        'norm_ffn_gain': 1.0 + normal(ks[16], (DEPTH, D_MODEL), 0.1),
        'w_group_router': normal(ks[17], (DEPTH, D_MODEL, N_GROUPS), D_MODEL ** -0.5),
        'b_group_router': normal(ks[18], (DEPTH, N_GROUPS), 0.01),
        'w_expert_router': normal(ks[19], (DEPTH, N_GROUPS, D_MODEL, EXPERTS_PER_GROUP), D_MODEL ** -0.5),
        'b_expert_router': normal(ks[20], (DEPTH, N_GROUPS, EXPERTS_PER_GROUP), 0.01),
        'w_expert_gate': normal(ks[21], (DEPTH, N_EXPERTS, D_MODEL, EXPERT_HIDDEN), D_MODEL ** -0.5),
        'w_expert_up': normal(ks[22], (DEPTH, N_EXPERTS, D_MODEL, EXPERT_HIDDEN), D_MODEL ** -0.5),
        'w_expert_down': normal(ks[23], (DEPTH, N_EXPERTS, EXPERT_HIDDEN, D_MODEL), EXPERT_HIDDEN ** -0.5),
        'norm_final_gain': 1.0 + normal(ks[24], (D_MODEL,), 0.1),
    }


def reference(x_prompt, x_sample, cache_diff_k, cache_diff_v, state_retention,
              norm_mix_gain, w_in, lambda_q1, lambda_k1, lambda_q2, lambda_k2, diff_subln_gain,
              w_ret_out, w_diff_out, w_out, rel_bias_table, norm_ffn_gain,
              w_group_router, b_group_router, w_expert_router, b_expert_router,
              w_expert_gate, w_expert_up, w_expert_down, norm_final_gain):
    S = x_prompt.shape[1]
    L = x_sample.shape[1]
    past = cache_diff_k.shape[2]
    pos_prompt = jnp.arange(S)
    pos_sample = past + jnp.arange(L)
    xp, xs = x_prompt, x_sample
    kp_list, vp_list, rp_list, ks_list, vs_list, rs_list = [], [], [], [], [], []
    for l in range(DEPTH):
        layer_weights = (norm_mix_gain[l], w_in[l], lambda_q1[l], lambda_k1[l], lambda_q2[l], lambda_k2[l],
                         diff_subln_gain[l], w_ret_out[l], w_diff_out[l], w_out[l], norm_ffn_gain[l],
                         w_group_router[l], b_group_router[l], w_expert_router[l], b_expert_router[l],
                         w_expert_gate[l], w_expert_up[l], w_expert_down[l])
        xp, k_p, v_p, r_p = trunk_layer(xp, pos_prompt, None, None, None, l, rel_bias_table, *layer_weights)
        xs, k_s, v_s, r_s = trunk_layer(xs, pos_sample, state_retention[l], cache_diff_k[l], cache_diff_v[l],
                                        l, rel_bias_table, *layer_weights)
        kp_list.append(k_p)
        vp_list.append(v_p)
        rp_list.append(r_p)
        ks_list.append(k_s)
        vs_list.append(v_s)
        rs_list.append(r_s)
    y_prompt = rms_norm(xp, norm_final_gain)
    y_sample = rms_norm(xs, norm_final_gain)
    return (y_prompt, y_sample, jnp.stack(kp_list), jnp.stack(vp_list), jnp.stack(rp_list),
            jnp.stack(ks_list), jnp.stack(vs_list), jnp.stack(rs_list))
```

```python
import functools
import math

import jax
import jax.numpy as jnp
from jax import lax
from jax.experimental import pallas as pl
from jax.experimental.pallas import tpu as pltpu

F32 = jnp.float32
BF16 = jnp.bfloat16

D_MODEL = 2048
CHUNK = 64
N_HEADS = 8
RET_KEY_DIM = 64
HEAD_V = 128
DIFF_HEAD_DIM = 64
IN_WIDTH = 10240
NUM_BUCKETS = 32
MAX_DISTANCE = 128
N_GROUPS = 8
EXPERTS_PER_GROUP = 8
N_EXPERTS = 64
EXPERT_HIDDEN = 512
EXPERT_BLOCK = 128
NORM_EPS = 1e-6
NEG_INF = -1e30
LANES = 128

VMEM_LIMIT = 56 * 1024 * 1024


def _cparams(sem):
    return pltpu.CompilerParams(dimension_semantics=sem, vmem_limit_bytes=VMEM_LIMIT)


PROJ_TM = 512
PROJ_TN = 512
ATT_TK = 256


def _rotate_pairs(acc, cos, sin):
    outs = []
    lane = lax.broadcasted_iota(jnp.int32, (acc.shape[0], LANES), 1)
    first_half = (lane % 64) < 32
    for c in range(acc.shape[1] // LANES):
        xs = acc[:, c * LANES:(c + 1) * LANES]
        swapped = jnp.where(first_half, pltpu.roll(xs, 96, axis=1), pltpu.roll(xs, 32, axis=1))
        outs.append(xs * cos + swapped * sin)
    return outs


def _proj_kernel(x_ref, gain_ref, w_ref, rot_ref, *out_refs, transposed_k):
    if transposed_k:
        z_ref, g_ref, gates_ref, v32_ref, kt_ref, kt32_ref, h_scr = out_refs
    else:
        z_ref, g_ref, gates_ref, v32_ref, k32_ref, h_scr = out_refs
    j = pl.program_id(1)

    @pl.when(j == 0)
    def _():
        x = x_ref[...]
        ms = jnp.mean(x * x, axis=-1, keepdims=True)
        h_scr[...] = (x * lax.rsqrt(ms + NORM_EPS) * gain_ref[...]).astype(BF16)

    acc = jnp.dot(h_scr[...], w_ref[...], preferred_element_type=F32)
    n_slab = PROJ_TN // LANES

    def store_slabs(vals):
        for c in range(n_slab):
            z_ref[c] = vals[c].astype(BF16)

    def split(a):
        return [a[:, c * LANES:(c + 1) * LANES] for c in range(n_slab)]

    @pl.when(j == 0)
    def _():
        store_slabs(_rotate_pairs(acc, rot_ref[:, 0:128], rot_ref[:, 128:256]))

    @pl.when(j == 1)
    def _():
        store_slabs(_rotate_pairs(acc, rot_ref[:, 256:384], rot_ref[:, 384:512]))

    @pl.when((j == 2) | (j == 3))
    def _():
        store_slabs(split(acc))

    @pl.when((j == 4) | (j == 5))
    def _():
        g_ref[...] = (acc * jax.nn.sigmoid(acc)).astype(BF16)

    @pl.when((j == 6) | (j == 7))
    def _():
        store_slabs(split(acc * (DIFF_HEAD_DIM ** -0.5)))

    @pl.when((j == 8) | (j == 9))
    def _():
        store_slabs(split(acc))
        if transposed_k:
            acc_t = acc.T
            kt32_ref[0, 0] = acc_t.reshape(n_slab, 2, DIFF_HEAD_DIM, PROJ_TM)
            acc_tb = acc_t.astype(BF16)
            for hh in range(n_slab):
                for t in range(PROJ_TM // ATT_TK):
                    kt_ref[0, hh, t] = acc_tb[hh * LANES:(hh + 1) * LANES, t * ATT_TK:(t + 1) * ATT_TK]
        else:
            k32_ref[...] = acc

    @pl.when((j == 10) | (j == 11))
    def _():
        store_slabs(split(acc))
        v32_ref[...] = acc

    @pl.when(j >= 12)
    def _():
        gates_ref[...] = jax.nn.sigmoid(acc).astype(BF16)


def _proj(x2d, gain, w_bf, rot, *, seq, transposed_k):
    T = x2d.shape[0]
    tm, tn = PROJ_TM, PROJ_TN
    ni, nj = T // tm, IN_WIDTH // tn
    rot_blocks = rot.shape[0] // tm

    def zmap(i, j):
        zj = jnp.where(j < 4, j, jnp.where(j < 6, 3, jnp.where(j < 12, j - 2, 9)))
        return (zj, i, 0)

    out_shape = [
        jax.ShapeDtypeStruct((40, T, LANES), BF16),
        jax.ShapeDtypeStruct((T, 1024), BF16),
        jax.ShapeDtypeStruct((T, 4096), BF16),
        jax.ShapeDtypeStruct((T, 1024), F32),
    ]
    out_specs = [
        pl.BlockSpec((4, tm, LANES), zmap),
        pl.BlockSpec((tm, tn), lambda i, j: (i, jnp.clip(j - 4, 0, 1))),
        pl.BlockSpec((tm, tn), lambda i, j: (i, jnp.clip(j - 12, 0, 7))),
        pl.BlockSpec((tm, tn), lambda i, j: (i, jnp.clip(j - 10, 0, 1))),
    ]
    if transposed_k:
        B = T // seq
        spb = seq // tm
        out_shape += [
            jax.ShapeDtypeStruct((B, N_HEADS, seq // ATT_TK, LANES, ATT_TK), BF16),
            jax.ShapeDtypeStruct((1, B, N_HEADS, 2, DIFF_HEAD_DIM, seq), F32),
        ]
        out_specs += [
            pl.BlockSpec((1, 4, tm // ATT_TK, LANES, ATT_TK),
                         lambda i, j: (i // spb, jnp.clip(j - 8, 0, 1), i % spb, 0, 0)),
            pl.BlockSpec((1, 1, 4, 2, DIFF_HEAD_DIM, tm),
                         lambda i, j: (0, i // spb, jnp.clip(j - 8, 0, 1), 0, 0, i % spb)),
        ]
    else:
        out_shape += [jax.ShapeDtypeStruct((T, 1024), F32)]
        out_specs += [pl.BlockSpec((tm, tn), lambda i, j: (i, jnp.clip(j - 8, 0, 1)))]

    return pl.pallas_call(
        functools.partial(_proj_kernel, transposed_k=transposed_k),
        out_shape=out_shape,
        grid=(ni, nj),
        in_specs=[
            pl.BlockSpec((tm, D_MODEL), lambda i, j: (i, 0)),
            pl.BlockSpec((1, D_MODEL), lambda i, j: (0, 0)),
            pl.BlockSpec((D_MODEL, tn), lambda i, j: (0, j)),
            pl.BlockSpec((tm, 512), lambda i, j: (i % rot_blocks, 0)),
        ],
        out_specs=out_specs,
        scratch_shapes=[pltpu.VMEM((tm, D_MODEL), BF16)],
        compiler_params=_cparams(("arbitrary", "arbitrary")),
        name="proj_t" if transposed_k else "proj",
    )(x2d, gain, w_bf, rot)


def _retention_kernel(q_ref, k_ref, v_ref, decay_ref, inner_ref, outer_ref, st0_ref,
                      o_ref, st_out_ref, st_scr, *, chunk):
    c = pl.program_id(1)
    nc = pl.num_programs(1)

    @pl.when(c == 0)
    def _():
        st_scr[...] = jnp.zeros_like(st_scr)
        for h in range(N_HEADS):
            a = h % 2
            st_scr[h, a * RET_KEY_DIM:(a + 1) * RET_KEY_DIM, :] = st0_ref[0, h]

    lane_lo = lax.broadcasted_iota(jnp.int32, (chunk, LANES), 1) < RET_KEY_DIM
    row_lo = lax.broadcasted_iota(jnp.int32, (LANES, LANES), 0) < RET_KEY_DIM
    for h in range(N_HEADS):
        p, a = h // 2, h % 2
        q = q_ref[p]
        k = k_ref[p]
        v = v_ref[h]
        qa = jnp.where(lane_lo == (a == 0), q, jnp.zeros_like(q))
        s = lax.dot_general(qa, k, (((1,), (1,)), ((), ())), preferred_element_type=F32)
        s = s * decay_ref[h]
        st = st_scr[h]
        inner = inner_ref[h]
        o = (jnp.dot(s.astype(BF16), v, preferred_element_type=F32)
             + jnp.dot(qa, st.astype(BF16), preferred_element_type=F32) * inner)
        ko = (k.astype(F32) * outer_ref[p]).astype(BF16)
        upd = lax.dot_general(ko, v, (((0,), (0,)), ((), ())), preferred_element_type=F32)
        upd = jnp.where(row_lo == (a == 0), upd, 0.0)
        chunk_decay = inner[chunk - 1:chunk, :]
        st_scr[h] = chunk_decay * st + upd
        o_n = o * lax.rsqrt(jnp.mean(o * o, axis=-1, keepdims=True) + NORM_EPS)
        o_ref[:, h * HEAD_V:(h + 1) * HEAD_V] = o_n.astype(BF16)

    @pl.when(c == nc - 1)
    def _():
        for h in range(N_HEADS):
            a = h % 2
            st_out_ref[0, h] = st_scr[h, a * RET_KEY_DIM:(a + 1) * RET_KEY_DIM, :]


def _retention_consts(chunk):
    log_g = jnp.log(1.0 - 2.0 ** (-5.0 - jnp.arange(N_HEADS, dtype=F32)))
    j = jnp.arange(chunk, dtype=F32)
    diff = j[:, None] - j[None, :]
    decay = jnp.where(diff >= 0, jnp.exp(log_g[:, None, None] * jnp.maximum(diff, 0.0)), 0.0)
    inner = jnp.exp(log_g[:, None] * (j + 1.0))
    outer = jnp.exp(log_g[:, None] * (chunk - 1.0 - j))
    inner_b = jnp.broadcast_to(inner[:, :, None], (N_HEADS, chunk, LANES))
    outer_pair = jnp.repeat(outer.reshape(N_HEADS // 2, 2, chunk).transpose(0, 2, 1), RET_KEY_DIM, axis=2)
    return decay, inner_b, outer_pair


def _retention(z3, state0, *, batch, seq, chunk):
    T = batch * seq
    nc = seq // chunk
    decay, inner_b, outer_pair = _retention_consts(chunk)
    return pl.pallas_call(
        functools.partial(_retention_kernel, chunk=chunk),
        out_shape=[jax.ShapeDtypeStruct((T, 1024), BF16),
                   jax.ShapeDtypeStruct((batch, N_HEADS, RET_KEY_DIM, HEAD_V), F32)],
        grid=(batch, nc),
        in_specs=[
            pl.BlockSpec((4, chunk, LANES), lambda b, c: (0, b * nc + c, 0)),
            pl.BlockSpec((4, chunk, LANES), lambda b, c: (1, b * nc + c, 0)),
            pl.BlockSpec((8, chunk, LANES), lambda b, c: (1, b * nc + c, 0)),
            pl.BlockSpec((N_HEADS, chunk, chunk), lambda b, c: (0, 0, 0)),
            pl.BlockSpec((N_HEADS, chunk, LANES), lambda b, c: (0, 0, 0)),
            pl.BlockSpec((N_HEADS // 2, chunk, LANES), lambda b, c: (0, 0, 0)),
            pl.BlockSpec((1, N_HEADS, RET_KEY_DIM, HEAD_V), lambda b, c: (b, 0, 0, 0)),
        ],
        out_specs=[
            pl.BlockSpec((chunk, 1024), lambda b, c: (b * nc + c, 0)),
            pl.BlockSpec((1, N_HEADS, RET_KEY_DIM, HEAD_V), lambda b, c: (b, 0, 0, 0)),
        ],
        scratch_shapes=[pltpu.VMEM((N_HEADS, LANES, LANES), F32)],
        compiler_params=_cparams(("arbitrary", "arbitrary")),
        name=f"retention_c{chunk}",
    )(z3, z3, z3, decay, inner_b, outer_pair, state0)


def _lam_init(layer=0):
    return 0.8 - 0.6 * math.exp(-0.3 * layer)


def _lam_from_ref(lam_ref):
    lp = lam_ref[...]
    s1 = jnp.sum(lp[0:1] * lp[1:2], axis=-1, keepdims=True)
    s2 = jnp.sum(lp[2:3] * lp[3:4], axis=-1, keepdims=True)
    return jnp.exp(s1) - jnp.exp(s2) + _lam_init()


def _stack_q(q):
    lane_lo = lax.broadcasted_iota(jnp.int32, q.shape, 1) < DIFF_HEAD_DIM
    zero = jnp.zeros_like(q)
    return jnp.concatenate([jnp.where(lane_lo, q, zero), jnp.where(lane_lo, zero, q)], axis=0)


def _softmax_step(s, v, m_ref, l_ref, acc_ref):
    m_prev = m_ref[...]
    m_new = jnp.maximum(m_prev, jnp.max(s, axis=1, keepdims=True))
    alpha = jnp.exp(m_prev - m_new)
    p = jnp.exp(s - m_new[:, 0:1])
    l_ref[...] = alpha * l_ref[...] + jnp.sum(p, axis=1, keepdims=True)
    acc_ref[...] = alpha * acc_ref[...] + jnp.dot(p.astype(BF16), v, preferred_element_type=F32)
    m_ref[...] = m_new


def _diff_finish(m_ref, l_ref, acc_ref, lam, gain, n):
    o = acc_ref[...] / l_ref[...]
    o = o[:n] - lam * o[n:]
    o = o * lax.rsqrt(jnp.mean(o * o, axis=-1, keepdims=True) + NORM_EPS) * gain
    return o * (1.0 - _lam_init())


def _relative_bias(q_pos, k_pos, table):
    rel = k_pos[None, :] - q_pos[:, None]
    half = NUM_BUCKETS // 2
    max_exact = half // 2
    n = jnp.abs(rel)
    log_ratio = jnp.log(jnp.maximum(n, 1).astype(F32) / max_exact) / math.log(MAX_DISTANCE / max_exact)
    large = jnp.minimum(max_exact + (log_ratio * (half - max_exact)).astype(jnp.int32), half - 1)
    bucket = (rel > 0).astype(jnp.int32) * half + jnp.where(n < max_exact, n, large)
    return jnp.moveaxis(table[bucket], -1, 0).astype(F32)


def _far_bias(table):
    return table[NUM_BUCKETS // 2 - 1].astype(F32)


ATT_TQ = 256


def _attn_prompt_kernel(q_ref, kt_ref, v_ref, bprev_ref, bdiag_ref, lam_ref, gain_ref,
                        o_ref, m_scr, l_scr, acc_scr):
    i = pl.program_id(2)
    qs = _stack_q(q_ref[0])
    m_scr[...] = jnp.full_like(m_scr, NEG_INF)
    l_scr[...] = jnp.zeros_like(l_scr)
    acc_scr[...] = jnp.zeros_like(acc_scr)

    def tile(j, bias):
        s = jnp.dot(qs, kt_ref[0, 0, j], preferred_element_type=F32)
        if bias is not None:
            s = s + bias
        v = v_ref[0, pl.ds(pl.multiple_of(j * ATT_TK, ATT_TK), ATT_TK), :]
        _softmax_step(s, v, m_scr, l_scr, acc_scr)

    def far_body(j, carry):
        tile(j, None)
        return carry

    lax.fori_loop(0, i - 1, far_body, 0)

    @pl.when(i >= 1)
    def _():
        tile(i - 1, bprev_ref[0])

    tile(i, bdiag_ref[0])
    o = _diff_finish(m_scr, l_scr, acc_scr, _lam_from_ref(lam_ref), gain_ref[...], ATT_TQ)
    o_ref[...] = o.astype(BF16)


def _attn_prompt(z3, kt5, bias_prev, bias_diag, lam_rows, gain, *, batch, seq):
    T = batch * seq
    nq = seq // ATT_TQ
    return pl.pallas_call(
        _attn_prompt_kernel,
        out_shape=jax.ShapeDtypeStruct((T, 1024), BF16),
        grid=(batch, N_HEADS, nq),
        in_specs=[
            pl.BlockSpec((1, ATT_TQ, LANES), lambda b, h, i: (16 + h, b * nq + i, 0)),
            pl.BlockSpec((1, 1, seq // ATT_TK, LANES, ATT_TK), lambda b, h, i: (b, h, 0, 0, 0)),
            pl.BlockSpec((1, seq, LANES), lambda b, h, i: (32 + h, b, 0)),
            pl.BlockSpec((1, 2 * ATT_TQ, ATT_TK), lambda b, h, i: (h, 0, 0)),
            pl.BlockSpec((1, 2 * ATT_TQ, ATT_TK), lambda b, h, i: (h, 0, 0)),
            pl.BlockSpec((8, LANES), lambda b, h, i: (0, 0)),
            pl.BlockSpec((1, LANES), lambda b, h, i: (0, 0)),
        ],
        out_specs=pl.BlockSpec((ATT_TQ, LANES), lambda b, h, i: (b * nq + i, h)),
        scratch_shapes=[pltpu.VMEM((2 * ATT_TQ, LANES), F32)] * 3,
        compiler_params=_cparams(("arbitrary", "arbitrary", "arbitrary")),
        name="attn_prompt",
    )(z3, kt5, z3, bias_prev, bias_diag, lam_rows, gain)


SAMPLE_TK = 512


def _attn_sample_kernel(q_ref, kc_ref, vc_ref, kn_ref, vn_ref, bpast_ref, bnew_ref, lam_ref, gain_ref,
                        o_ref, m_scr, l_scr, acc_scr, *, n_past_tiles, n_q):
    t = pl.program_id(1)

    @pl.when(t == 0)
    def _():
        m_scr[...] = jnp.full_like(m_scr, NEG_INF)
        l_scr[...] = jnp.zeros_like(l_scr)
        acc_scr[...] = jnp.zeros_like(acc_scr)

    @pl.when(t < n_past_tiles)
    def _():
        for h in range(N_HEADS):
            qs = _stack_q(q_ref[h])
            kt = kc_ref[0, h].astype(BF16)
            s = jnp.dot(qs, kt, preferred_element_type=F32) + bpast_ref[0, h]
            v = vc_ref[0, pl.ds(h, SAMPLE_TK, stride=N_HEADS), :].astype(BF16)
            _softmax_step(s, v, m_scr.at[h], l_scr.at[h], acc_scr.at[h])

    @pl.when(t == n_past_tiles)
    def _():
        lam = _lam_from_ref(lam_ref)
        for h in range(N_HEADS):
            qs = _stack_q(q_ref[h])
            s = lax.dot_general(qs, kn_ref[h], (((1,), (1,)), ((), ())), preferred_element_type=F32)
            s = s + bnew_ref[h]
            _softmax_step(s, vn_ref[h], m_scr.at[h], l_scr.at[h], acc_scr.at[h])
            o = _diff_finish(m_scr.at[h], l_scr.at[h], acc_scr.at[h], lam, gain_ref[...], n_q)
            o_ref[:, h * HEAD_V:(h + 1) * HEAD_V] = o.astype(BF16)


def _attn_sample(z3, kc, vc, bias_past, bias_new, lam_rows, gain, *, batch, n_q, past):
    npt = past // SAMPLE_TK
    return pl.pallas_call(
        functools.partial(_attn_sample_kernel, n_past_tiles=npt, n_q=n_q),
        out_shape=jax.ShapeDtypeStruct((batch * n_q, 1024), BF16),
        grid=(batch, npt + 1),
        in_specs=[
            pl.BlockSpec((8, n_q, LANES), lambda b, t: (2, b, 0)),
            pl.BlockSpec((1, N_HEADS, LANES, SAMPLE_TK), lambda b, t: (b, 0, 0, jnp.minimum(t, npt - 1))),
            pl.BlockSpec((1, SAMPLE_TK * N_HEADS, LANES), lambda b, t: (b, jnp.minimum(t, npt - 1), 0)),
            pl.BlockSpec((8, n_q, LANES), lambda b, t: (3, b, 0)),
            pl.BlockSpec((8, n_q, LANES), lambda b, t: (4, b, 0)),
            pl.BlockSpec((1, N_HEADS, 2 * n_q, SAMPLE_TK), lambda b, t: (jnp.where(t == npt - 1, 1, 0), 0, 0, 0)),
            pl.BlockSpec((N_HEADS, 2 * n_q, n_q), lambda b, t: (0, 0, 0)),
            pl.BlockSpec((8, LANES), lambda b, t: (0, 0)),
            pl.BlockSpec((1, LANES), lambda b, t: (0, 0)),
        ],
        out_specs=pl.BlockSpec((n_q, 1024), lambda b, t: (b, 0)),
        scratch_shapes=[pltpu.VMEM((N_HEADS, 2 * n_q, LANES), F32)] * 3,
        compiler_params=_cparams(("arbitrary", "arbitrary")),
        name="attn_sample",
    )(z3, kc, vc, z3, z3, bias_past, bias_new, lam_rows, gain)


POST_TM = 256


def _post_a_kernel(or_ref, g_ref, od_ref, gr_ref, gd_ref, wr_ref, wd_ref, out_ref):
    a = jnp.dot(or_ref[...] * g_ref[...], wr_ref[...], preferred_element_type=F32)
    b = jnp.dot(od_ref[...], wd_ref[...], preferred_element_type=F32)
    out_ref[...] = (gr_ref[...].astype(F32) * a + gd_ref[...].astype(F32) * b).astype(BF16)


def _post_a(o_r, g, o_d, gates, wr_bf, wd_bf):
    T = o_r.shape[0]
    tm = POST_TM
    return pl.pallas_call(
        _post_a_kernel,
        out_shape=jax.ShapeDtypeStruct((T, D_MODEL), BF16),
        grid=(T // tm,),
        in_specs=[
            pl.BlockSpec((tm, 1024), lambda i: (i, 0)),
            pl.BlockSpec((tm, 1024), lambda i: (i, 0)),
            pl.BlockSpec((tm, 1024), lambda i: (i, 0)),
            pl.BlockSpec((tm, D_MODEL), lambda i: (i, 0)),
            pl.BlockSpec((tm, D_MODEL), lambda i: (i, 1)),
            pl.BlockSpec((1024, D_MODEL), lambda i: (0, 0)),
            pl.BlockSpec((1024, D_MODEL), lambda i: (0, 0)),
        ],
        out_specs=pl.BlockSpec((tm, D_MODEL), lambda i: (i, 0)),
        compiler_params=_cparams(("arbitrary",)),
        name="post_a",
    )(o_r, g, o_d, gates, gates, wr_bf, wd_bf)


def _post_b_kernel(xp_ref, mp_ref, xs_ref, ms_ref, wo_ref, gain_ref, wrt_ref, x1_ref, lg_ref, *, n_prompt_blocks):
    i = pl.program_id(0)
    tm = POST_TM

    def body(x_ref, mg_ref):
        x1 = x_ref[...] + jnp.dot(mg_ref[...], wo_ref[...], preferred_element_type=F32)
        x1_ref[...] = x1
        h2 = x1 * lax.rsqrt(jnp.mean(x1 * x1, axis=-1, keepdims=True) + NORM_EPS) * gain_ref[...]
        h_hi = h2.astype(BF16)
        h_lo = (h2 - h_hi.astype(F32)).astype(BF16)
        r = jnp.dot(jnp.concatenate([h_hi, h_lo], axis=0), wrt_ref[...], preferred_element_type=F32)
        lg_ref[...] = (r[:tm, :LANES] + r[:tm, LANES:]) + (r[tm:, :LANES] + r[tm:, LANES:])

    @pl.when(i < n_prompt_blocks)
    def _():
        body(xp_ref, mp_ref)

    @pl.when(i >= n_prompt_blocks)
    def _():
        body(xs_ref, ms_ref)


def _post_b(xp2, mgp, xs2, mgs, wo_bf, gain, w_router2):
    tm = POST_TM
    npb, nsb = xp2.shape[0] // tm, xs2.shape[0] // tm
    T = xp2.shape[0] + xs2.shape[0]
    pmap = lambda i: (jnp.minimum(i, npb - 1), 0)
    smap = lambda i: (jnp.maximum(i - npb, 0), 0)
    return pl.pallas_call(
        functools.partial(_post_b_kernel, n_prompt_blocks=npb),
        out_shape=[jax.ShapeDtypeStruct((T, D_MODEL), F32),
                   jax.ShapeDtypeStruct((T, LANES), F32)],
        grid=(npb + nsb,),
        in_specs=[
            pl.BlockSpec((tm, D_MODEL), pmap),
            pl.BlockSpec((tm, D_MODEL), pmap),
            pl.BlockSpec((tm, D_MODEL), smap),
            pl.BlockSpec((tm, D_MODEL), smap),
            pl.BlockSpec((D_MODEL, D_MODEL), lambda i: (0, 0)),
            pl.BlockSpec((1, D_MODEL), lambda i: (0, 0)),
            pl.BlockSpec((D_MODEL, 2 * LANES), lambda i: (0, 0)),
        ],
        out_specs=[pl.BlockSpec((tm, D_MODEL), lambda i: (i, 0)),
                   pl.BlockSpec((tm, LANES), lambda i: (i, 0))],
        compiler_params=_cparams(("arbitrary",)),
        name="post_b",
    )(xp2, mgp, xs2, mgs, wo_bf, gain, w_router2)


ROUTE_TM = 512


def _route_kernel(lg_ref, bias_ref, info_ref, cnt_ref, tri_scr, carry_scr):
    i = pl.program_id(0)
    tm = ROUTE_TM

    @pl.when(i == 0)
    def _():
        r = lax.broadcasted_iota(jnp.int32, (tm, tm), 0)
        c = lax.broadcasted_iota(jnp.int32, (tm, tm), 1)
        tri_scr[...] = jnp.where(c < r, 1.0, 0.0).astype(BF16)
        carry_scr[...] = jnp.zeros_like(carry_scr)

    lg = lg_ref[...] + bias_ref[...]
    lane = lax.broadcasted_iota(jnp.int32, (tm, LANES), 1)
    lane_f = lane.astype(F32)
    neg = jnp.float32(-jnp.inf)

    def first_argmax(vals):
        top = jnp.max(vals, axis=1, keepdims=True)
        idx = jnp.min(jnp.where(vals == top, lane_f, float(LANES)), axis=1, keepdims=True)
        return top, idx

    is_group = lane < N_GROUPS
    gl = jnp.where(is_group, lg, neg)
    g_top, g_idx = first_argmax(gl)
    g_weight = 1.0 / jnp.sum(jnp.exp(gl - g_top), axis=1, keepdims=True)
    lane_group = ((lane - N_GROUPS) >> 3).astype(F32)
    in_group = (lane >= N_GROUPS) & (lane < N_GROUPS + N_EXPERTS) & (lane_group == g_idx)
    el = jnp.where(in_group, lg, neg)
    t1, i1 = first_argmax(el)
    el2 = jnp.where(lane_f == i1, neg, el)
    t2, i2 = first_argmax(el2)
    e2w = jnp.exp(t2 - t1)
    p1 = 1.0 / (1.0 + e2w)
    gate1 = g_weight * p1
    gate2 = g_weight * (e2w * p1)
    e1 = i1 - float(N_GROUPS)
    e2 = i2 - float(N_GROUPS)

    hot1 = lane_f == e1
    hot2 = lane_f == e2
    both = jnp.where(hot1 | hot2, 1.0, 0.0)
    prefix = jnp.dot(tri_scr[...], both.astype(BF16), preferred_element_type=F32) + carry_scr[...]
    rank1 = jnp.sum(jnp.where(hot1, prefix, 0.0), axis=1, keepdims=True)
    rank2 = jnp.sum(jnp.where(hot2, prefix, 0.0), axis=1, keepdims=True)
    carry_scr[...] = carry_scr[...] + jnp.sum(both, axis=0, keepdims=True)

    info = jnp.where(lane == 0, e1, jnp.where(lane == 1, e2, jnp.where(lane == 2, gate1, jnp.where(
        lane == 3, gate2, jnp.where(lane == 4, rank1, jnp.where(lane == 5, rank2, 0.0))))))
    info_ref[...] = info
    cnt_ref[...] = carry_scr[...]


def _route(lg_all, bias_row):
    T = lg_all.shape[0]
    tm = ROUTE_TM
    return pl.pallas_call(
        _route_kernel,
        out_shape=[jax.ShapeDtypeStruct((T, LANES), F32), jax.ShapeDtypeStruct((1, LANES), F32)],
        grid=(T // tm,),
        in_specs=[pl.BlockSpec((tm, LANES), lambda i: (i, 0)), pl.BlockSpec((1, LANES), lambda i: (0, 0))],
        out_specs=[pl.BlockSpec((tm, LANES), lambda i: (i, 0)), pl.BlockSpec((1, LANES), lambda i: (0, 0))],
        scratch_shapes=[pltpu.VMEM((tm, tm), BF16), pltpu.VMEM((1, LANES), F32)],
        compiler_params=_cparams(("arbitrary",)),
        name="route",
    )(lg_all, bias_row)


def _moe_kernel(tok_ref, bexp_ref, nused_ref, x1_hbm, gain_ref, wg_ref, wu_ref, wd_ref,
                out_ref, xbuf, sem, wg_bf, wu_bf, wd_bf):
    b = pl.program_id(0)
    n_used = nused_ref[0]
    slot = b % 2

    def gather_copy(row, s, r):
        return pltpu.make_async_copy(x1_hbm.at[pl.ds(row, 1)], xbuf.at[s, pl.ds(r, 1)], sem.at[s])

    def issue(blk, s):
        base = blk * EXPERT_BLOCK

        def body(r, carry):
            gather_copy(tok_ref[base + r], s, r).start()
            return carry

        lax.fori_loop(0, EXPERT_BLOCK, body, 0, unroll=8)

    @pl.when(b == 0)
    def _():
        issue(0, 0)

    @pl.when(b + 1 < n_used)
    def _():
        issue(b + 1, 1 - slot)

    @pl.when(b < n_used)
    def _():
        pltpu.make_async_copy(x1_hbm.at[pl.ds(0, EXPERT_BLOCK)], xbuf.at[slot], sem.at[slot]).wait()
        new_expert = (b == 0) | (bexp_ref[b] != bexp_ref[jnp.maximum(b - 1, 0)])

        @pl.when(new_expert)
        def _():
            wg_bf[...] = wg_ref[0].astype(BF16)
            wu_bf[...] = wu_ref[0].astype(BF16)
            wd_bf[...] = wd_ref[0].astype(BF16)

        x = xbuf[slot]
        h = (x * lax.rsqrt(jnp.mean(x * x, axis=-1, keepdims=True) + NORM_EPS) * gain_ref[...]).astype(BF16)
        g = jnp.dot(h, wg_bf[...], preferred_element_type=F32)
        u = jnp.dot(h, wu_bf[...], preferred_element_type=F32)
        a = (g * jax.nn.sigmoid(g) * u).astype(BF16)
        out_ref[...] = jnp.dot(a, wd_bf[...], preferred_element_type=F32)

    @pl.when(b >= n_used)
    def _():
        out_ref[...] = jnp.zeros_like(out_ref)


def _moe(row_token, block_expert, n_used, x1_all, gain, wg, wu, wd):
    n_rows = row_token.shape[0]
    n_blocks = n_rows // EXPERT_BLOCK
    grid_spec = pltpu.PrefetchScalarGridSpec(
        num_scalar_prefetch=3,
        grid=(n_blocks,),
        in_specs=[
            pl.BlockSpec(memory_space=pl.ANY),
            pl.BlockSpec((1, D_MODEL), lambda b, tok, be, nu: (0, 0)),
            pl.BlockSpec((1, D_MODEL, EXPERT_HIDDEN), lambda b, tok, be, nu: (be[b], 0, 0)),
            pl.BlockSpec((1, D_MODEL, EXPERT_HIDDEN), lambda b, tok, be, nu: (be[b], 0, 0)),
            pl.BlockSpec((1, EXPERT_HIDDEN, D_MODEL), lambda b, tok, be, nu: (be[b], 0, 0)),
        ],
        out_specs=pl.BlockSpec((EXPERT_BLOCK, D_MODEL), lambda b, tok, be, nu: (b, 0)),
        scratch_shapes=[
            pltpu.VMEM((2, EXPERT_BLOCK, D_MODEL), F32),
            pltpu.SemaphoreType.DMA((2,)),
            pltpu.VMEM((D_MODEL, EXPERT_HIDDEN), BF16),
            pltpu.VMEM((D_MODEL, EXPERT_HIDDEN), BF16),
            pltpu.VMEM((EXPERT_HIDDEN, D_MODEL), BF16),
        ],
    )
    return pl.pallas_call(
        _moe_kernel,
        out_shape=jax.ShapeDtypeStruct((n_rows, D_MODEL), F32),
        grid_spec=grid_spec,
        compiler_params=_cparams(("arbitrary",)),
        name="moe",
    )(row_token, block_expert, n_used, x1_all, gain, wg, wu, wd)


FINAL_TM = 256


def _final_kernel(dest_ref, x1_ref, info_ref, gain_ref, rows_hbm, y_ref, rbuf, sem, *, tok_offset):
    i = pl.program_id(0)
    n = pl.num_programs(0)
    tm = FINAL_TM
    slot = i % 2

    def issue(blk, s):
        base = 2 * (tok_offset + blk * tm)

        def body(r, carry):
            for kk in range(2):
                pltpu.make_async_copy(rows_hbm.at[pl.ds(dest_ref[base + 2 * r + kk], 1)],
                                      rbuf.at[s, kk, pl.ds(r, 1)], sem.at[s]).start()
            return carry

        lax.fori_loop(0, tm, body, 0, unroll=4)

    @pl.when(i == 0)
    def _():
        issue(0, 0)

    @pl.when(i + 1 < n)
    def _():
        issue(i + 1, 1 - slot)

    for kk in range(2):
        pltpu.make_async_copy(rows_hbm.at[pl.ds(0, tm)], rbuf.at[slot, kk], sem.at[slot]).wait()
    info = info_ref[...]
    x = x1_ref[...] + info[:, 2:3] * rbuf[slot, 0] + info[:, 3:4] * rbuf[slot, 1]
    y_ref[...] = x * lax.rsqrt(jnp.mean(x * x, axis=-1, keepdims=True) + NORM_EPS) * gain_ref[...]


def _final(dest, x1_all, info, gain, out_rows, *, tok_offset, n_tok):
    tm = FINAL_TM
    off = tok_offset // tm
    grid_spec = pltpu.PrefetchScalarGridSpec(
        num_scalar_prefetch=1,
        grid=(n_tok // tm,),
        in_specs=[
            pl.BlockSpec((tm, D_MODEL), lambda i, d: (i + off, 0)),
            pl.BlockSpec((tm, LANES), lambda i, d: (i + off, 0)),
            pl.BlockSpec((1, D_MODEL), lambda i, d: (0, 0)),
            pl.BlockSpec(memory_space=pl.ANY),
        ],
        out_specs=pl.BlockSpec((tm, D_MODEL), lambda i, d: (i, 0)),
        scratch_shapes=[pltpu.VMEM((2, 2, tm, D_MODEL), F32), pltpu.SemaphoreType.DMA((2,))],
    )
    return pl.pallas_call(
        functools.partial(_final_kernel, tok_offset=tok_offset),
        out_shape=jax.ShapeDtypeStruct((n_tok, D_MODEL), F32),
        grid_spec=grid_spec,
        compiler_params=_cparams(("arbitrary",)),
        name="final",
    )(dest, x1_all, info, gain, out_rows)


def _rot_table(pos, n_rows):
    half = RET_KEY_DIM // 2
    theta = 10000.0 ** (-jnp.linspace(0.0, 1.0, half, dtype=F32))
    ang = pos.astype(F32)[:, None] * theta[None, :]
    cos, sin = jnp.cos(ang), jnp.sin(ang)
    cos128 = jnp.tile(cos, (1, 4))
    sin128 = jnp.tile(jnp.concatenate([-sin, sin], axis=1), (1, 2))
    k_scale = RET_KEY_DIM ** -0.5
    tab = jnp.concatenate([cos128, sin128, cos128 * k_scale, sin128 * k_scale], axis=1)
    return jnp.tile(tab, (n_rows // tab.shape[0], 1))


def kernel(x_prompt, x_sample, cache_diff_k, cache_diff_v, state_retention, norm_mix_gain, w_in, lambda_q1,
           lambda_k1, lambda_q2, lambda_k2, diff_subln_gain, w_ret_out, w_diff_out, w_out, rel_bias_table,
           norm_ffn_gain, w_group_router, b_group_router, w_expert_router, b_expert_router, w_expert_gate,
           w_expert_up, w_expert_down, norm_final_gain):
    B, S, D = x_prompt.shape
    BS, L, _ = x_sample.shape
    past = cache_diff_k.shape[2]
    TP, TS = B * S, BS * L
    T = TP + TS

    w_in_bf = w_in[0].astype(BF16)
    wr_bf = w_ret_out[0].astype(BF16)
    wd_bf = w_diff_out[0].astype(BF16)
    wo_bf = w_out[0].astype(BF16)
    gain_mix = norm_mix_gain[0][None, :]
    gain_ffn = norm_ffn_gain[0][None, :]
    gain_fin = norm_final_gain[None, :]
    gain_sub = diff_subln_gain[0][None, :]
    lam_rows = jnp.zeros((8, LANES), F32).at[0:4, 0:DIFF_HEAD_DIM].set(
        jnp.stack([lambda_q1[0], lambda_k1[0], lambda_q2[0], lambda_k2[0]]))
    rot_p = _rot_table(jnp.arange(S), S)
    rot_s = _rot_table(past + jnp.arange(L), PROJ_TM)

    far = _far_bias(rel_bias_table)[:, None, None]
    qp = ATT_TQ + jnp.arange(ATT_TQ)
    def tile_bias(k_pos):
        bias = _relative_bias(qp, k_pos, rel_bias_table) - far
        visible = (k_pos[None, :] // CHUNK) <= (qp[:, None] // CHUNK)
        bias = jnp.where(visible[None], bias, NEG_INF)
        return jnp.concatenate([bias, bias], axis=1)
    bias_prev = tile_bias(jnp.arange(ATT_TK))
    bias_diag = tile_bias(ATT_TQ + jnp.arange(ATT_TK))
    q_pos_s = past + jnp.arange(L)
    b_last = _relative_bias(q_pos_s, past - SAMPLE_TK + jnp.arange(SAMPLE_TK), rel_bias_table) - far
    b_last = jnp.concatenate([b_last, b_last], axis=1)
    bias_past = jnp.stack([jnp.zeros_like(b_last), b_last])
    b_new = _relative_bias(q_pos_s, q_pos_s, rel_bias_table) - far
    bias_new = jnp.concatenate([b_new, b_new], axis=1)

    xp2 = x_prompt.reshape(TP, D)
    xs2 = x_sample.reshape(TS, D)
    z3p, gp, gatesp, v32p, kt5, kt32 = _proj(xp2, gain_mix, w_in_bf, rot_p, seq=S, transposed_k=True)
    z3s, gs, gatess, v32s, k32s = _proj(xs2, gain_mix, w_in_bf, rot_s, seq=L, transposed_k=False)

    zero_state = jnp.zeros((B, N_HEADS, RET_KEY_DIM, HEAD_V), F32)
    orp, ret_p = _retention(z3p, zero_state, batch=B, seq=S, chunk=256)
    ors, ret_s = _retention(z3s, state_retention[0], batch=BS, seq=L, chunk=L)

    odp = _attn_prompt(z3p, kt5, bias_prev, bias_diag, lam_rows, gain_sub, batch=B, seq=S)
    kc = jnp.transpose(cache_diff_k[0], (0, 2, 3, 4, 1)).reshape(BS, N_HEADS, LANES, past)
    vc = cache_diff_v[0].reshape(BS, past * N_HEADS, LANES)
    ods = _attn_sample(z3s, kc, vc, bias_past, bias_new, lam_rows, gain_sub, batch=BS, n_q=L, past=past)

    mgp = _post_a(orp, gp, odp, gatesp, wr_bf, wd_bf)
    mgs = _post_a(ors, gs, ods, gatess, wr_bf, wd_bf)

    w_rt = jnp.zeros((D, LANES), F32)
    w_rt = w_rt.at[:, 0:N_GROUPS].set(w_group_router[0])
    w_rt = w_rt.at[:, N_GROUPS:N_GROUPS + N_EXPERTS].set(
        jnp.transpose(w_expert_router[0], (1, 0, 2)).reshape(D, N_EXPERTS))
    w_rt_hi = w_rt.astype(BF16)
    w_rt_lo = (w_rt - w_rt_hi.astype(F32)).astype(BF16)
    w_router2 = jnp.concatenate([w_rt_hi, w_rt_lo], axis=1)
    bias_row = jnp.zeros((1, LANES), F32)
    bias_row = bias_row.at[0, 0:N_GROUPS].set(b_group_router[0])
    bias_row = bias_row.at[0, N_GROUPS:N_GROUPS + N_EXPERTS].set(b_expert_router[0].reshape(-1))

    x1_all, lg_all = _post_b(xp2, mgp, xs2, mgs, wo_bf, gain_ffn, w_router2)

    info, counts = _route(lg_all, bias_row)
    counts = counts[0, :N_EXPERTS].astype(jnp.int32)
    padded = (counts + EXPERT_BLOCK - 1) // EXPERT_BLOCK * EXPERT_BLOCK
    pad_end = jnp.cumsum(padded)
    offs = pad_end - padded
    e12 = info[:, 0:2].astype(jnp.int32)
    rank12 = info[:, 4:6].astype(jnp.int32)
    dest = (offs[e12] + rank12).reshape(-1)
    n_assign = 2 * T
    n_rows = -(-n_assign // EXPERT_BLOCK) * EXPERT_BLOCK + N_EXPERTS * EXPERT_BLOCK
    n_blocks = n_rows // EXPERT_BLOCK
    token_id = (jnp.arange(n_assign, dtype=jnp.int32) // 2)
    row_token = jnp.zeros((n_rows,), jnp.int32).at[dest].set(token_id)
    block_expert = jnp.minimum(
        jnp.searchsorted(pad_end, jnp.arange(n_blocks, dtype=jnp.int32) * EXPERT_BLOCK, side='right'),
        N_EXPERTS - 1).astype(jnp.int32)
    n_used = (pad_end[-1:] // EXPERT_BLOCK).astype(jnp.int32)

    out_rows = _moe(row_token, block_expert, n_used, x1_all, gain_ffn,
                    w_expert_gate[0], w_expert_up[0], w_expert_down[0])

    y_p = _final(dest, x1_all, info, gain_fin, out_rows, tok_offset=0, n_tok=TP)
    y_s = _final(dest, x1_all, info, gain_fin, out_rows, tok_offset=TP, n_tok=TS)

    new_k_p = jnp.transpose(kt32, (0, 1, 5, 2, 3, 4))
    new_v_p = v32p.reshape(1, B, S, N_HEADS, HEAD_V)
    new_k_s = k32s.reshape(1, BS, L, N_HEADS, 2, DIFF_HEAD_DIM)
    new_v_s = v32s.reshape(1, BS, L, N_HEADS, HEAD_V)
    return (y_p.reshape(B, S, D), y_s.reshape(BS, L, D), new_k_p, new_v_p, ret_p[None],
            new_k_s, new_v_s, ret_s[None])
```

```python
import functools
import math

import jax
import jax.numpy as jnp
from jax import lax
from jax.experimental import pallas as pl
from jax.experimental.pallas import tpu as pltpu

F32 = jnp.float32
BF16 = jnp.bfloat16

D_MODEL = 2048
CHUNK = 64
N_HEADS = 8
RET_KEY_DIM = 64
HEAD_V = 128
DIFF_HEAD_DIM = 64
IN_WIDTH = 10240
NUM_BUCKETS = 32
MAX_DISTANCE = 128
N_GROUPS = 8
EXPERTS_PER_GROUP = 8
N_EXPERTS = 64
EXPERT_HIDDEN = 512
EXPERT_BLOCK = 128
NORM_EPS = 1e-6
NEG_INF = -1e30
LANES = 128
LOG2E = 1.4426950408889634
VT_ROWS = HEAD_V + 16

VMEM_LIMIT = 56 * 1024 * 1024


def _cparams(sem):
    return pltpu.CompilerParams(dimension_semantics=sem, vmem_limit_bytes=VMEM_LIMIT)


PROJ_TM = 512
PROJ_TN = 512
ATT_TK = 256


def _rotate_pairs(acc, cos, sin):
    outs = []
    lane = lax.broadcasted_iota(jnp.int32, (acc.shape[0], LANES), 1)
    first_half = (lane % 64) < 32
    for c in range(acc.shape[1] // LANES):
        xs = acc[:, c * LANES:(c + 1) * LANES]
        swapped = jnp.where(first_half, pltpu.roll(xs, 96, axis=1), pltpu.roll(xs, 32, axis=1))
        outs.append(xs * cos + swapped * sin)
    return outs


def _proj_kernel(x_ref, gain_ref, w_ref, rot_ref, *out_refs, transposed_k):
    if transposed_k:
        z_ref, g_ref, gates_ref, v32_ref, qt_ref, vt_ref, kt32_ref, h_scr = out_refs
    else:
        z_ref, g_ref, gates_ref, v32_ref, k32_ref, h_scr = out_refs
    j = pl.program_id(1)

    @pl.when(j == 0)
    def _():
        x = x_ref[...]
        ms = jnp.mean(x * x, axis=-1, keepdims=True)
        h_scr[...] = (x * lax.rsqrt(ms + NORM_EPS) * gain_ref[...]).astype(BF16)

    acc = jnp.dot(h_scr[...], w_ref[...], preferred_element_type=F32)
    n_slab = PROJ_TN // LANES

    def store_slabs(vals):
        for c in range(n_slab):
            z_ref[c] = vals[c].astype(BF16)

    def split(a):
        return [a[:, c * LANES:(c + 1) * LANES] for c in range(n_slab)]

    @pl.when(j == 0)
    def _():
        store_slabs(_rotate_pairs(acc, rot_ref[:, 0:128], rot_ref[:, 128:256]))

    @pl.when(j == 1)
    def _():
        store_slabs(_rotate_pairs(acc, rot_ref[:, 256:384], rot_ref[:, 384:512]))

    @pl.when((j == 2) | (j == 3))
    def _():
        store_slabs(split(acc))

    @pl.when((j == 4) | (j == 5))
    def _():
        g_ref[...] = (acc * jax.nn.sigmoid(acc)).astype(BF16)

    @pl.when((j == 6) | (j == 7))
    def _():
        qs = acc * (DIFF_HEAD_DIM ** -0.5 * LOG2E)
        store_slabs(split(qs))
        if transposed_k:
            qs_t = qs.T.astype(BF16)
            for hh in range(n_slab):
                qt_ref[0, hh] = qs_t[hh * LANES:(hh + 1) * LANES, :]

    @pl.when((j == 8) | (j == 9))
    def _():
        store_slabs(split(acc))
        if transposed_k:
            kt32_ref[0, 0] = acc.T.reshape(n_slab, 2, DIFF_HEAD_DIM, PROJ_TM)
        else:
            k32_ref[...] = acc

    @pl.when((j == 10) | (j == 11))
    def _():
        store_slabs(split(acc))
        v32_ref[...] = acc
        if transposed_k:
            acc_tb = acc.T.astype(BF16)
            ones = jnp.ones((VT_ROWS - HEAD_V, ATT_TK), BF16)
            for hh in range(n_slab):
                for t in range(PROJ_TM // ATT_TK):
                    vt_ref[0, hh, t, 0:HEAD_V, :] = acc_tb[hh * LANES:(hh + 1) * LANES, t * ATT_TK:(t + 1) * ATT_TK]
                    vt_ref[0, hh, t, HEAD_V:VT_ROWS, :] = ones

    @pl.when(j >= 12)
    def _():
        gates_ref[...] = jax.nn.sigmoid(acc).astype(BF16)


def _proj(x2d, gain, w_bf, rot, *, seq, transposed_k):
    T = x2d.shape[0]
    tm, tn = PROJ_TM, PROJ_TN
    ni, nj = T // tm, IN_WIDTH // tn
    rot_blocks = rot.shape[0] // tm

    def zmap(i, j):
        zj = jnp.where(j < 4, j, jnp.where(j < 6, 3, jnp.where(j < 12, j - 2, 9)))
        return (zj, i, 0)

    out_shape = [
        jax.ShapeDtypeStruct((40, T, LANES), BF16),
        jax.ShapeDtypeStruct((T, 1024), BF16),
        jax.ShapeDtypeStruct((T, 4096), BF16),
        jax.ShapeDtypeStruct((T, 1024), F32),
    ]
    out_specs = [
        pl.BlockSpec((4, tm, LANES), zmap),
        pl.BlockSpec((tm, tn), lambda i, j: (i, jnp.clip(j - 4, 0, 1))),
        pl.BlockSpec((tm, tn), lambda i, j: (i, jnp.clip(j - 12, 0, 7))),
        pl.BlockSpec((tm, tn), lambda i, j: (i, jnp.clip(j - 10, 0, 1))),
    ]
    if transposed_k:
        B = T // seq
        spb = seq // tm
        out_shape += [
            jax.ShapeDtypeStruct((B, N_HEADS, LANES, seq), BF16),
            jax.ShapeDtypeStruct((B, N_HEADS, seq // ATT_TK, VT_ROWS, ATT_TK), BF16),
            jax.ShapeDtypeStruct((1, B, N_HEADS, 2, DIFF_HEAD_DIM, seq), F32),
        ]
        out_specs += [
            pl.BlockSpec((1, 4, LANES, tm),
                         lambda i, j: (i // spb, jnp.clip(j - 6, 0, 1), 0, i % spb)),
            pl.BlockSpec((1, 4, tm // ATT_TK, VT_ROWS, ATT_TK),
                         lambda i, j: (i // spb, jnp.clip(j - 10, 0, 1), i % spb, 0, 0)),
            pl.BlockSpec((1, 1, 4, 2, DIFF_HEAD_DIM, tm),
                         lambda i, j: (0, i // spb, jnp.clip(j - 8, 0, 1), 0, 0, i % spb)),
        ]
    else:
        out_shape += [jax.ShapeDtypeStruct((T, 1024), F32)]
        out_specs += [pl.BlockSpec((tm, tn), lambda i, j: (i, jnp.clip(j - 8, 0, 1)))]

    return pl.pallas_call(
        functools.partial(_proj_kernel, transposed_k=transposed_k),
        out_shape=out_shape,
        grid=(ni, nj),
        in_specs=[
            pl.BlockSpec((tm, D_MODEL), lambda i, j: (i, 0)),
            pl.BlockSpec((1, D_MODEL), lambda i, j: (0, 0)),
            pl.BlockSpec((D_MODEL, tn), lambda i, j: (0, j)),
            pl.BlockSpec((tm, 512), lambda i, j: (i % rot_blocks, 0)),
        ],
        out_specs=out_specs,
        scratch_shapes=[pltpu.VMEM((tm, D_MODEL), BF16)],
        compiler_params=_cparams(("arbitrary", "arbitrary")),
        name="proj_t" if transposed_k else "proj",
    )(x2d, gain, w_bf, rot)


def _retention_kernel(q_ref, k_ref, v_ref, decay_ref, inner_ref, outer_ref, st0_ref,
                      o_ref, st_out_ref, st_scr, *, chunk):
    c = pl.program_id(1)
    nc = pl.num_programs(1)

    @pl.when(c == 0)
    def _():
        st_scr[...] = jnp.zeros_like(st_scr)
        for h in range(N_HEADS):
            a = h % 2
            st_scr[h, a * RET_KEY_DIM:(a + 1) * RET_KEY_DIM, :] = st0_ref[0, h]

    lane_lo = lax.broadcasted_iota(jnp.int32, (chunk, LANES), 1) < RET_KEY_DIM
    row_lo = lax.broadcasted_iota(jnp.int32, (LANES, LANES), 0) < RET_KEY_DIM
    for h in range(N_HEADS):
        p, a = h // 2, h % 2
        q = q_ref[p]
        k = k_ref[p]
        v = v_ref[h]
        qa = jnp.where(lane_lo == (a == 0), q, jnp.zeros_like(q))
        s = lax.dot_general(qa, k, (((1,), (1,)), ((), ())), preferred_element_type=F32)
        s = s * decay_ref[h]
        st = st_scr[h]
        inner = inner_ref[h]
        o = (jnp.dot(s.astype(BF16), v, preferred_element_type=F32)
             + jnp.dot(qa, st.astype(BF16), preferred_element_type=F32) * inner)
        ko = (k.astype(F32) * outer_ref[p]).astype(BF16)
        upd = lax.dot_general(ko, v, (((0,), (0,)), ((), ())), preferred_element_type=F32)
        upd = jnp.where(row_lo == (a == 0), upd, 0.0)
        chunk_decay = inner[chunk - 1:chunk, :]
        st_scr[h] = chunk_decay * st + upd
        o_n = o * lax.rsqrt(jnp.mean(o * o, axis=-1, keepdims=True) + NORM_EPS)
        o_ref[:, h * HEAD_V:(h + 1) * HEAD_V] = o_n.astype(BF16)

    @pl.when(c == nc - 1)
    def _():
        for h in range(N_HEADS):
            a = h % 2
            st_out_ref[0, h] = st_scr[h, a * RET_KEY_DIM:(a + 1) * RET_KEY_DIM, :]


def _retention_consts(chunk):
    log_g = jnp.log(1.0 - 2.0 ** (-5.0 - jnp.arange(N_HEADS, dtype=F32)))
    j = jnp.arange(chunk, dtype=F32)
    diff = j[:, None] - j[None, :]
    decay = jnp.where(diff >= 0, jnp.exp(log_g[:, None, None] * jnp.maximum(diff, 0.0)), 0.0)
    inner = jnp.exp(log_g[:, None] * (j + 1.0))
    outer = jnp.exp(log_g[:, None] * (chunk - 1.0 - j))
    inner_b = jnp.broadcast_to(inner[:, :, None], (N_HEADS, chunk, LANES))
    outer_pair = jnp.repeat(outer.reshape(N_HEADS // 2, 2, chunk).transpose(0, 2, 1), RET_KEY_DIM, axis=2)
    return decay, inner_b, outer_pair


def _retention(z3, state0, *, batch, seq, chunk):
    T = batch * seq
    nc = seq // chunk
    decay, inner_b, outer_pair = _retention_consts(chunk)
    return pl.pallas_call(
        functools.partial(_retention_kernel, chunk=chunk),
        out_shape=[jax.ShapeDtypeStruct((T, 1024), BF16),
                   jax.ShapeDtypeStruct((batch, N_HEADS, RET_KEY_DIM, HEAD_V), F32)],
        grid=(batch, nc),
        in_specs=[
            pl.BlockSpec((4, chunk, LANES), lambda b, c: (0, b * nc + c, 0)),
            pl.BlockSpec((4, chunk, LANES), lambda b, c: (1, b * nc + c, 0)),
            pl.BlockSpec((8, chunk, LANES), lambda b, c: (1, b * nc + c, 0)),
            pl.BlockSpec((N_HEADS, chunk, chunk), lambda b, c: (0, 0, 0)),
            pl.BlockSpec((N_HEADS, chunk, LANES), lambda b, c: (0, 0, 0)),
            pl.BlockSpec((N_HEADS // 2, chunk, LANES), lambda b, c: (0, 0, 0)),
            pl.BlockSpec((1, N_HEADS, RET_KEY_DIM, HEAD_V), lambda b, c: (b, 0, 0, 0)),
        ],
        out_specs=[
            pl.BlockSpec((chunk, 1024), lambda b, c: (b * nc + c, 0)),
            pl.BlockSpec((1, N_HEADS, RET_KEY_DIM, HEAD_V), lambda b, c: (b, 0, 0, 0)),
        ],
        scratch_shapes=[pltpu.VMEM((N_HEADS, LANES, LANES), F32)],
        compiler_params=_cparams(("arbitrary", "arbitrary")),
        name=f"retention_c{chunk}",
    )(z3, z3, z3, decay, inner_b, outer_pair, state0)


def _lam_init(layer=0):
    return 0.8 - 0.6 * math.exp(-0.3 * layer)


def _lam_from_ref(lam_ref):
    lp = lam_ref[...]
    s1 = jnp.sum(lp[0:1] * lp[1:2], axis=-1, keepdims=True)
    s2 = jnp.sum(lp[2:3] * lp[3:4], axis=-1, keepdims=True)
    return jnp.exp(s1) - jnp.exp(s2) + _lam_init()


def _stack_q(q):
    lane_lo = lax.broadcasted_iota(jnp.int32, q.shape, 1) < DIFF_HEAD_DIM
    zero = jnp.zeros_like(q)
    return jnp.concatenate([jnp.where(lane_lo, q, zero), jnp.where(lane_lo, zero, q)], axis=0)


def _softmax_step(s, v, m_ref, l_ref, acc_ref):
    m_prev = m_ref[...]
    m_new = jnp.maximum(m_prev, jnp.max(s, axis=1, keepdims=True))
    alpha = jnp.exp2(m_prev - m_new)
    p = jnp.exp2(s - m_new[:, 0:1])
    l_ref[...] = alpha * l_ref[...] + jnp.sum(p, axis=1, keepdims=True)
    acc_ref[...] = alpha * acc_ref[...] + jnp.dot(p.astype(BF16), v, preferred_element_type=F32)
    m_ref[...] = m_new


def _diff_finish(m_ref, l_ref, acc_ref, lam, gain, n):
    o = acc_ref[...] / l_ref[...]
    o = o[:n] - lam * o[n:]
    o = o * lax.rsqrt(jnp.mean(o * o, axis=-1, keepdims=True) + NORM_EPS) * gain
    return o * (1.0 - _lam_init())


def _relative_bias(q_pos, k_pos, table):
    rel = k_pos[None, :] - q_pos[:, None]
    half = NUM_BUCKETS // 2
    max_exact = half // 2
    n = jnp.abs(rel)
    log_ratio = jnp.log(jnp.maximum(n, 1).astype(F32) / max_exact) / math.log(MAX_DISTANCE / max_exact)
    large = jnp.minimum(max_exact + (log_ratio * (half - max_exact)).astype(jnp.int32), half - 1)
    bucket = (rel > 0).astype(jnp.int32) * half + jnp.where(n < max_exact, n, large)
    out = jnp.zeros((table.shape[1],) + bucket.shape, F32)
    for bkt in range(NUM_BUCKETS):
        out = jnp.where(bucket[None] == bkt, table[bkt].astype(F32)[:, None, None], out)
    return out


def _far_bias(table):
    return table[NUM_BUCKETS // 2 - 1].astype(F32)


ATT_TQ = 256
ATT_HP = 4


def _attn_prompt_kernel(qt_ref, k_ref, vt_ref, bprev_ref, bdiag_ref, lam_ref, gain_ref, o_ref, acc_scr):
    i = pl.program_id(2)
    row_lo = lax.broadcasted_iota(jnp.int32, (LANES, ATT_TQ), 0) < DIFF_HEAD_DIM
    qs = []
    for hh in range(ATT_HP):
        qt = qt_ref[0, hh]
        zero = jnp.zeros_like(qt)
        qs.append(jnp.concatenate([jnp.where(row_lo, qt, zero), jnp.where(row_lo, zero, qt)], axis=1))
    acc_scr[...] = jnp.zeros_like(acc_scr)

    def tile(j, ms, biases):
        ss = []
        for hh in range(ATT_HP):
            kt = k_ref[hh, pl.ds(pl.multiple_of(j * ATT_TK, ATT_TK), ATT_TK), :]
            s = jnp.dot(kt, qs[hh], preferred_element_type=F32)
            ss.append(s if biases is None else s + biases[hh])
        ps, alphas, m_out = [], [], []
        for hh in range(ATT_HP):
            m_new = jnp.maximum(ms[hh], jnp.max(ss[hh], axis=0, keepdims=True))
            alphas.append(jnp.exp2(ms[hh] - m_new))
            ps.append(jnp.exp2(ss[hh] - m_new).astype(BF16))
            m_out.append(m_new)
        for hh in range(ATT_HP):
            acc_scr[hh] = alphas[hh] * acc_scr[hh] + jnp.dot(vt_ref[0, hh, j], ps[hh],
                                                             preferred_element_type=F32)
        return tuple(m_out)

    ms = tuple(jnp.full((1, 2 * ATT_TQ), NEG_INF, F32) for _ in range(ATT_HP))
    ms = lax.fori_loop(0, i - 1, lambda j, c: tile(j, c, None), ms)
    jp = jnp.maximum(i - 1, 0)
    ms = lax.cond(i >= 1,
                  lambda c: tile(jp, c, [bprev_ref[hh] for hh in range(ATT_HP)]),
                  lambda c: c, ms)
    tile(i, ms, [bdiag_ref[hh] for hh in range(ATT_HP)])

    lam = _lam_from_ref(lam_ref)
    for hh in range(ATT_HP):
        o = acc_scr[hh, 0:HEAD_V, :] / acc_scr[hh, HEAD_V:HEAD_V + 1, :]
        o = o[:, :ATT_TQ] - lam * o[:, ATT_TQ:]
        o = o * lax.rsqrt(jnp.mean(o * o, axis=0, keepdims=True) + NORM_EPS) * gain_ref[...]
        o_ref[:, hh * HEAD_V:(hh + 1) * HEAD_V] = (o * (1.0 - _lam_init())).T.astype(BF16)


def _attn_prompt(z3, qt, vt5, bias_prev, bias_diag, lam_rows, gain_col, *, batch, seq):
    T = batch * seq
    nq = seq // ATT_TQ
    hp = ATT_HP
    return pl.pallas_call(
        _attn_prompt_kernel,
        out_shape=jax.ShapeDtypeStruct((T, 1024), BF16),
        grid=(batch, N_HEADS // hp, nq),
        in_specs=[
            pl.BlockSpec((1, hp, LANES, ATT_TQ), lambda b, g, i: (b, g, 0, i)),
            pl.BlockSpec((hp, seq, LANES), lambda b, g, i: (24 // hp + g, b, 0)),
            pl.BlockSpec((1, hp, seq // ATT_TK, VT_ROWS, ATT_TK), lambda b, g, i: (b, g, 0, 0, 0)),
            pl.BlockSpec((hp, ATT_TK, 2 * ATT_TQ), lambda b, g, i: (g, 0, 0)),
            pl.BlockSpec((hp, ATT_TK, 2 * ATT_TQ), lambda b, g, i: (g, 0, 0)),
            pl.BlockSpec((8, LANES), lambda b, g, i: (0, 0)),
            pl.BlockSpec((LANES, ATT_TQ), lambda b, g, i: (0, 0)),
        ],
        out_specs=pl.BlockSpec((ATT_TQ, hp * HEAD_V), lambda b, g, i: (b * nq + i, g)),
        scratch_shapes=[pltpu.VMEM((hp, VT_ROWS, 2 * ATT_TQ), F32)],
        compiler_params=_cparams(("arbitrary", "arbitrary", "arbitrary")),
        name="attn_prompt",
    )(qt, z3, vt5, bias_prev, bias_diag, lam_rows, gain_col)


SAMPLE_TK = 512


def _attn_sample_kernel(q_ref, kc_ref, vc_ref, kn_ref, vn_ref, bpast_ref, bnew_ref, lam_ref, gain_ref,
                        o_ref, m_scr, l_scr, acc_scr, *, n_past_tiles, n_q):
    t = pl.program_id(1)

    @pl.when(t == 0)
    def _():
        m_scr[...] = jnp.full_like(m_scr, NEG_INF)
        l_scr[...] = jnp.zeros_like(l_scr)
        acc_scr[...] = jnp.zeros_like(acc_scr)

    @pl.when(t < n_past_tiles)
    def _():
        ss = []
        for h in range(N_HEADS):
            kt = kc_ref[0, h].astype(BF16)
            ss.append(jnp.dot(_stack_q(q_ref[h]), kt, preferred_element_type=F32) + bpast_ref[0, h])
        ps, alphas = [], []
        for h in range(N_HEADS):
            m_prev = m_scr[h]
            m_new = jnp.maximum(m_prev, jnp.max(ss[h], axis=1, keepdims=True))
            alpha = jnp.exp2(m_prev - m_new)
            p = jnp.exp2(ss[h] - m_new[:, 0:1])
            l_scr[h] = alpha * l_scr[h] + jnp.sum(p, axis=1, keepdims=True)
            m_scr[h] = m_new
            ps.append(p.astype(BF16))
            alphas.append(alpha)
        for h in range(N_HEADS):
            v = vc_ref[0, pl.ds(h, SAMPLE_TK, stride=N_HEADS), :].astype(BF16)
            acc_scr[h] = alphas[h] * acc_scr[h] + jnp.dot(ps[h], v, preferred_element_type=F32)

    @pl.when(t == n_past_tiles)
    def _():
        lam = _lam_from_ref(lam_ref)
        for h in range(N_HEADS):
            qs = _stack_q(q_ref[h])
            s = lax.dot_general(qs, kn_ref[h], (((1,), (1,)), ((), ())), preferred_element_type=F32)
            s = s + bnew_ref[h]
            _softmax_step(s, vn_ref[h], m_scr.at[h], l_scr.at[h], acc_scr.at[h])
            o = _diff_finish(m_scr.at[h], l_scr.at[h], acc_scr.at[h], lam, gain_ref[...], n_q)
            o_ref[:, h * HEAD_V:(h + 1) * HEAD_V] = o.astype(BF16)


def _attn_sample(z3, kc, vc, bias_past, bias_new, lam_rows, gain, *, batch, n_q, past):
    npt = past // SAMPLE_TK
    return pl.pallas_call(
        functools.partial(_attn_sample_kernel, n_past_tiles=npt, n_q=n_q),
        out_shape=jax.ShapeDtypeStruct((batch * n_q, 1024), BF16),
        grid=(batch, npt + 1),
        in_specs=[
            pl.BlockSpec((8, n_q, LANES), lambda b, t: (2, b, 0)),
            pl.BlockSpec((1, N_HEADS, LANES, SAMPLE_TK), lambda b, t: (b, 0, 0, jnp.minimum(t, npt - 1))),
            pl.BlockSpec((1, SAMPLE_TK * N_HEADS, LANES), lambda b, t: (b, jnp.minimum(t, npt - 1), 0)),
            pl.BlockSpec((8, n_q, LANES), lambda b, t: (3, b, 0)),
            pl.BlockSpec((8, n_q, LANES), lambda b, t: (4, b, 0)),
            pl.BlockSpec((1, N_HEADS, 2 * n_q, SAMPLE_TK), lambda b, t: (jnp.where(t == npt - 1, 1, 0), 0, 0, 0)),
            pl.BlockSpec((N_HEADS, 2 * n_q, n_q), lambda b, t: (0, 0, 0)),
            pl.BlockSpec((8, LANES), lambda b, t: (0, 0)),
            pl.BlockSpec((1, LANES), lambda b, t: (0, 0)),
        ],
        out_specs=pl.BlockSpec((n_q, 1024), lambda b, t: (b, 0)),
        scratch_shapes=[pltpu.VMEM((N_HEADS, 2 * n_q, LANES), F32)] * 3,
        compiler_params=_cparams(("arbitrary", "arbitrary")),
        name="attn_sample",
    )(z3, kc, vc, z3, z3, bias_past, bias_new, lam_rows, gain)


POST_TM = 256


def _post_a_kernel(or_ref, g_ref, od_ref, gr_ref, gd_ref, wr_ref, wd_ref, out_ref):
    a = jnp.dot(or_ref[...] * g_ref[...], wr_ref[...], preferred_element_type=F32)
    b = jnp.dot(od_ref[...], wd_ref[...], preferred_element_type=F32)
    out_ref[...] = (gr_ref[...].astype(F32) * a + gd_ref[...].astype(F32) * b).astype(BF16)


def _post_a(o_r, g, o_d, gates, wr_bf, wd_bf):
    T = o_r.shape[0]
    tm = POST_TM
    return pl.pallas_call(
        _post_a_kernel,
        out_shape=jax.ShapeDtypeStruct((T, D_MODEL), BF16),
        grid=(T // tm,),
        in_specs=[
            pl.BlockSpec((tm, 1024), lambda i: (i, 0)),
            pl.BlockSpec((tm, 1024), lambda i: (i, 0)),
            pl.BlockSpec((tm, 1024), lambda i: (i, 0)),
            pl.BlockSpec((tm, D_MODEL), lambda i: (i, 0)),
            pl.BlockSpec((tm, D_MODEL), lambda i: (i, 1)),
            pl.BlockSpec((1024, D_MODEL), lambda i: (0, 0)),
            pl.BlockSpec((1024, D_MODEL), lambda i: (0, 0)),
        ],
        out_specs=pl.BlockSpec((tm, D_MODEL), lambda i: (i, 0)),
        compiler_params=_cparams(("arbitrary",)),
        name="post_a",
    )(o_r, g, o_d, gates, gates, wr_bf, wd_bf)


def _post_b_kernel(xp_ref, mp_ref, xs_ref, ms_ref, wo_ref, gain_ref, wrt_ref, x1_ref, lg_ref, *, n_prompt_blocks):
    i = pl.program_id(0)
    tm = POST_TM

    def body(x_ref, mg_ref):
        x1 = x_ref[...] + jnp.dot(mg_ref[...], wo_ref[...], preferred_element_type=F32)
        x1_ref[...] = x1
        h2 = x1 * lax.rsqrt(jnp.mean(x1 * x1, axis=-1, keepdims=True) + NORM_EPS) * gain_ref[...]
        h_hi = h2.astype(BF16)
        h_lo = (h2 - h_hi.astype(F32)).astype(BF16)
        r = jnp.dot(jnp.concatenate([h_hi, h_lo], axis=0), wrt_ref[...], preferred_element_type=F32)
        lg_ref[...] = (r[:tm, :LANES] + r[:tm, LANES:]) + (r[tm:, :LANES] + r[tm:, LANES:])

    @pl.when(i < n_prompt_blocks)
    def _():
        body(xp_ref, mp_ref)

    @pl.when(i >= n_prompt_blocks)
    def _():
        body(xs_ref, ms_ref)


def _post_b(xp2, mgp, xs2, mgs, wo_bf, gain, w_router2):
    tm = POST_TM
    npb, nsb = xp2.shape[0] // tm, xs2.shape[0] // tm
    T = xp2.shape[0] + xs2.shape[0]
    pmap = lambda i: (jnp.minimum(i, npb - 1), 0)
    smap = lambda i: (jnp.maximum(i - npb, 0), 0)
    return pl.pallas_call(
        functools.partial(_post_b_kernel, n_prompt_blocks=npb),
        out_shape=[jax.ShapeDtypeStruct((T, D_MODEL), F32),
                   jax.ShapeDtypeStruct((T, LANES), F32)],
        grid=(npb + nsb,),
        in_specs=[
            pl.BlockSpec((tm, D_MODEL), pmap),
            pl.BlockSpec((tm, D_MODEL), pmap),
            pl.BlockSpec((tm, D_MODEL), smap),
            pl.BlockSpec((tm, D_MODEL), smap),
            pl.BlockSpec((D_MODEL, D_MODEL), lambda i: (0, 0)),
            pl.BlockSpec((1, D_MODEL), lambda i: (0, 0)),
            pl.BlockSpec((D_MODEL, 2 * LANES), lambda i: (0, 0)),
        ],
        out_specs=[pl.BlockSpec((tm, D_MODEL), lambda i: (i, 0)),
                   pl.BlockSpec((tm, LANES), lambda i: (i, 0))],
        compiler_params=_cparams(("arbitrary",)),
        name="post_b",
    )(xp2, mgp, xs2, mgs, wo_bf, gain, w_router2)


ROUTE_TM = 512


def _route_kernel(lg_ref, bias_ref, info_ref, cnt_ref, tri_scr, carry_scr):
    i = pl.program_id(0)
    tm = ROUTE_TM

    @pl.when(i == 0)
    def _():
        r = lax.broadcasted_iota(jnp.int32, (tm, tm), 0)
        c = lax.broadcasted_iota(jnp.int32, (tm, tm), 1)
        tri_scr[...] = jnp.where(c < r, 1.0, 0.0).astype(BF16)
        carry_scr[...] = jnp.zeros_like(carry_scr)

    lg = lg_ref[...] + bias_ref[...]
    lane = lax.broadcasted_iota(jnp.int32, (tm, LANES), 1)
    lane_f = lane.astype(F32)
    neg = jnp.float32(-jnp.inf)

    def first_argmax(vals):
        top = jnp.max(vals, axis=1, keepdims=True)
        idx = jnp.min(jnp.where(vals == top, lane_f, float(LANES)), axis=1, keepdims=True)
        return top, idx

    is_group = lane < N_GROUPS
    gl = jnp.where(is_group, lg, neg)
    g_top, g_idx = first_argmax(gl)
    g_weight = 1.0 / jnp.sum(jnp.exp(gl - g_top), axis=1, keepdims=True)
    lane_group = ((lane - N_GROUPS) >> 3).astype(F32)
    in_group = (lane >= N_GROUPS) & (lane < N_GROUPS + N_EXPERTS) & (lane_group == g_idx)
    el = jnp.where(in_group, lg, neg)
    t1, i1 = first_argmax(el)
    el2 = jnp.where(lane_f == i1, neg, el)
    t2, i2 = first_argmax(el2)
    e2w = jnp.exp(t2 - t1)
    p1 = 1.0 / (1.0 + e2w)
    gate1 = g_weight * p1
    gate2 = g_weight * (e2w * p1)
    e1 = i1 - float(N_GROUPS)
    e2 = i2 - float(N_GROUPS)

    hot1 = lane_f == e1
    hot2 = lane_f == e2
    both = jnp.where(hot1 | hot2, 1.0, 0.0)
    prefix = jnp.dot(tri_scr[...], both.astype(BF16), preferred_element_type=F32) + carry_scr[...]
    rank1 = jnp.sum(jnp.where(hot1, prefix, 0.0), axis=1, keepdims=True)
    rank2 = jnp.sum(jnp.where(hot2, prefix, 0.0), axis=1, keepdims=True)
    carry_scr[...] = carry_scr[...] + jnp.sum(both, axis=0, keepdims=True)

    info = jnp.where(lane == 0, e1, jnp.where(lane == 1, e2, jnp.where(lane == 2, gate1, jnp.where(
        lane == 3, gate2, jnp.where(lane == 4, rank1, jnp.where(lane == 5, rank2, 0.0))))))
    info_ref[...] = info
    cnt_ref[...] = carry_scr[...]


def _route(lg_all, bias_row):
    T = lg_all.shape[0]
    tm = ROUTE_TM
    return pl.pallas_call(
        _route_kernel,
        out_shape=[jax.ShapeDtypeStruct((T, LANES), F32), jax.ShapeDtypeStruct((1, LANES), F32)],
        grid=(T // tm,),
        in_specs=[pl.BlockSpec((tm, LANES), lambda i: (i, 0)), pl.BlockSpec((1, LANES), lambda i: (0, 0))],
        out_specs=[pl.BlockSpec((tm, LANES), lambda i: (i, 0)), pl.BlockSpec((1, LANES), lambda i: (0, 0))],
        scratch_shapes=[pltpu.VMEM((tm, tm), BF16), pltpu.VMEM((1, LANES), F32)],
        compiler_params=_cparams(("arbitrary",)),
        name="route",
    )(lg_all, bias_row)


def _moe_kernel(tok_ref, bexp_ref, nused_ref, x1_hbm, gain_ref, wg_ref, wu_ref, wd_ref,
                out_ref, xbuf, sem, wg_bf, wu_bf, wd_bf):
    b = pl.program_id(0)
    n_used = nused_ref[0]
    slot = b % 2

    def gather_copy(row, s, r):
        return pltpu.make_async_copy(x1_hbm.at[pl.ds(row, 1)], xbuf.at[s, pl.ds(r, 1)], sem.at[s])

    def issue(blk, s):
        base = blk * EXPERT_BLOCK

        def body(r, carry):
            gather_copy(tok_ref[base + r], s, r).start()
            return carry

        lax.fori_loop(0, EXPERT_BLOCK, body, 0, unroll=8)

    @pl.when(b == 0)
    def _():
        issue(0, 0)

    @pl.when(b + 1 < n_used)
    def _():
        issue(b + 1, 1 - slot)

    @pl.when(b < n_used)
    def _():
        pltpu.make_async_copy(x1_hbm.at[pl.ds(0, EXPERT_BLOCK)], xbuf.at[slot], sem.at[slot]).wait()
        new_expert = (b == 0) | (bexp_ref[b] != bexp_ref[jnp.maximum(b - 1, 0)])

        @pl.when(new_expert)
        def _():
            wg_bf[...] = wg_ref[0].astype(BF16)
            wu_bf[...] = wu_ref[0].astype(BF16)
            wd_bf[...] = wd_ref[0].astype(BF16)

        x = xbuf[slot]
        h = (x * lax.rsqrt(jnp.mean(x * x, axis=-1, keepdims=True) + NORM_EPS) * gain_ref[...]).astype(BF16)
        g = jnp.dot(h, wg_bf[...], preferred_element_type=F32)
        u = jnp.dot(h, wu_bf[...], preferred_element_type=F32)
        a = (g * jax.nn.sigmoid(g) * u).astype(BF16)
        out_ref[...] = jnp.dot(a, wd_bf[...], preferred_element_type=F32)

    @pl.when(b >= n_used)
    def _():
        out_ref[...] = jnp.zeros_like(out_ref)


def _moe(row_token, block_expert, n_used, x1_all, gain, wg, wu, wd):
    n_rows = row_token.shape[0]
    n_blocks = n_rows // EXPERT_BLOCK
    grid_spec = pltpu.PrefetchScalarGridSpec(
        num_scalar_prefetch=3,
        grid=(n_blocks,),
        in_specs=[
            pl.BlockSpec(memory_space=pl.ANY),
            pl.BlockSpec((1, D_MODEL), lambda b, tok, be, nu: (0, 0)),
            pl.BlockSpec((1, D_MODEL, EXPERT_HIDDEN), lambda b, tok, be, nu: (be[b], 0, 0)),
            pl.BlockSpec((1, D_MODEL, EXPERT_HIDDEN), lambda b, tok, be, nu: (be[b], 0, 0)),
            pl.BlockSpec((1, EXPERT_HIDDEN, D_MODEL), lambda b, tok, be, nu: (be[b], 0, 0)),
        ],
        out_specs=pl.BlockSpec((EXPERT_BLOCK, D_MODEL), lambda b, tok, be, nu: (b, 0)),
        scratch_shapes=[
            pltpu.VMEM((2, EXPERT_BLOCK, D_MODEL), F32),
            pltpu.SemaphoreType.DMA((2,)),
            pltpu.VMEM((D_MODEL, EXPERT_HIDDEN), BF16),
            pltpu.VMEM((D_MODEL, EXPERT_HIDDEN), BF16),
            pltpu.VMEM((EXPERT_HIDDEN, D_MODEL), BF16),
        ],
    )
    return pl.pallas_call(
        _moe_kernel,
        out_shape=jax.ShapeDtypeStruct((n_rows, D_MODEL), F32),
        grid_spec=grid_spec,
        compiler_params=_cparams(("arbitrary",)),
        name="moe",
    )(row_token, block_expert, n_used, x1_all, gain, wg, wu, wd)


FINAL_TM = 256


def _final_kernel(dest_ref, x1_ref, info_ref, gain_ref, rows_hbm, y_ref, rbuf, sem, *, tok_offset):
    i = pl.program_id(0)
    n = pl.num_programs(0)
    tm = FINAL_TM
    slot = i % 2

    def issue(blk, s):
        base = 2 * (tok_offset + blk * tm)

        def body(r, carry):
            for kk in range(2):
                pltpu.make_async_copy(rows_hbm.at[pl.ds(dest_ref[base + 2 * r + kk], 1)],
                                      rbuf.at[s, kk, pl.ds(r, 1)], sem.at[s]).start()
            return carry

        lax.fori_loop(0, tm, body, 0, unroll=4)

    @pl.when(i == 0)
    def _():
        issue(0, 0)

    @pl.when(i + 1 < n)
    def _():
        issue(i + 1, 1 - slot)

    for kk in range(2):
        pltpu.make_async_copy(rows_hbm.at[pl.ds(0, tm)], rbuf.at[slot, kk], sem.at[slot]).wait()
    info = info_ref[...]
    x = x1_ref[...] + info[:, 2:3] * rbuf[slot, 0] + info[:, 3:4] * rbuf[slot, 1]
    y_ref[...] = x * lax.rsqrt(jnp.mean(x * x, axis=-1, keepdims=True) + NORM_EPS) * gain_ref[...]


def _final(dest, x1_all, info, gain, out_rows, *, tok_offset, n_tok):
    tm = FINAL_TM
    off = tok_offset // tm
    grid_spec = pltpu.PrefetchScalarGridSpec(
        num_scalar_prefetch=1,
        grid=(n_tok // tm,),
        in_specs=[
            pl.BlockSpec((tm, D_MODEL), lambda i, d: (i + off, 0)),
            pl.BlockSpec((tm, LANES), lambda i, d: (i + off, 0)),
            pl.BlockSpec((1, D_MODEL), lambda i, d: (0, 0)),
            pl.BlockSpec(memory_space=pl.ANY),
        ],
        out_specs=pl.BlockSpec((tm, D_MODEL), lambda i, d: (i, 0)),
        scratch_shapes=[pltpu.VMEM((2, 2, tm, D_MODEL), F32), pltpu.SemaphoreType.DMA((2,))],
    )
    return pl.pallas_call(
        functools.partial(_final_kernel, tok_offset=tok_offset),
        out_shape=jax.ShapeDtypeStruct((n_tok, D_MODEL), F32),
        grid_spec=grid_spec,
        compiler_params=_cparams(("arbitrary",)),
        name="final",
    )(dest, x1_all, info, gain, out_rows)


def _rot_table(pos, n_rows):
    half = RET_KEY_DIM // 2
    theta = 10000.0 ** (-jnp.linspace(0.0, 1.0, half, dtype=F32))
    ang = pos.astype(F32)[:, None] * theta[None, :]
    cos, sin = jnp.cos(ang), jnp.sin(ang)
    cos128 = jnp.tile(cos, (1, 4))
    sin128 = jnp.tile(jnp.concatenate([-sin, sin], axis=1), (1, 2))
    k_scale = RET_KEY_DIM ** -0.5
    tab = jnp.concatenate([cos128, sin128, cos128 * k_scale, sin128 * k_scale], axis=1)
    return jnp.tile(tab, (n_rows // tab.shape[0], 1))


def kernel(x_prompt, x_sample, cache_diff_k, cache_diff_v, state_retention, norm_mix_gain, w_in, lambda_q1,
           lambda_k1, lambda_q2, lambda_k2, diff_subln_gain, w_ret_out, w_diff_out, w_out, rel_bias_table,
           norm_ffn_gain, w_group_router, b_group_router, w_expert_router, b_expert_router, w_expert_gate,
           w_expert_up, w_expert_down, norm_final_gain):
    B, S, D = x_prompt.shape
    BS, L, _ = x_sample.shape
    past = cache_diff_k.shape[2]
    TP, TS = B * S, BS * L
    T = TP + TS

    w_in_bf = w_in[0].astype(BF16)
    wr_bf = w_ret_out[0].astype(BF16)
    wd_bf = w_diff_out[0].astype(BF16)
    wo_bf = w_out[0].astype(BF16)
    gain_mix = norm_mix_gain[0][None, :]
    gain_ffn = norm_ffn_gain[0][None, :]
    gain_fin = norm_final_gain[None, :]
    gain_sub = diff_subln_gain[0][None, :]
    lam_rows = jnp.zeros((8, LANES), F32).at[0:4, 0:DIFF_HEAD_DIM].set(
        jnp.stack([lambda_q1[0], lambda_k1[0], lambda_q2[0], lambda_k2[0]]))
    rot_p = _rot_table(jnp.arange(S), S)
    rot_s = _rot_table(past + jnp.arange(L), PROJ_TM)

    far = _far_bias(rel_bias_table)[:, None, None]
    qp = ATT_TQ + jnp.arange(ATT_TQ)
    def tile_bias(k_pos):
        bias = (_relative_bias(qp, k_pos, rel_bias_table) - far) * LOG2E
        visible = (k_pos[None, :] // CHUNK) <= (qp[:, None] // CHUNK)
        bias = jnp.swapaxes(jnp.where(visible[None], bias, NEG_INF), 1, 2)
        return jnp.concatenate([bias, bias], axis=2)
    bias_prev = tile_bias(jnp.arange(ATT_TK))
    bias_diag = tile_bias(ATT_TQ + jnp.arange(ATT_TK))
    q_pos_s = past + jnp.arange(L)
    b_last = (_relative_bias(q_pos_s, past - SAMPLE_TK + jnp.arange(SAMPLE_TK), rel_bias_table) - far) * LOG2E
    b_last = jnp.concatenate([b_last, b_last], axis=1)
    bias_past = jnp.stack([jnp.zeros_like(b_last), b_last])
    b_new = (_relative_bias(q_pos_s, q_pos_s, rel_bias_table) - far) * LOG2E
    bias_new = jnp.concatenate([b_new, b_new], axis=1)

    xp2 = x_prompt.reshape(TP, D)
    xs2 = x_sample.reshape(TS, D)
    z3p, gp, gatesp, v32p, qt, vt5, kt32 = _proj(xp2, gain_mix, w_in_bf, rot_p, seq=S, transposed_k=True)
    z3s, gs, gatess, v32s, k32s = _proj(xs2, gain_mix, w_in_bf, rot_s, seq=L, transposed_k=False)

    zero_state = jnp.zeros((B, N_HEADS, RET_KEY_DIM, HEAD_V), F32)
    orp, ret_p = _retention(z3p, zero_state, batch=B, seq=S, chunk=256)
    ors, ret_s = _retention(z3s, state_retention[0], batch=BS, seq=L, chunk=L)

    gain_col = jnp.broadcast_to(diff_subln_gain[0][:, None], (HEAD_V, ATT_TQ))
    odp = _attn_prompt(z3p, qt, vt5, bias_prev, bias_diag, lam_rows, gain_col, batch=B, seq=S)
    kc = jnp.transpose(cache_diff_k[0], (0, 2, 3, 4, 1)).reshape(BS, N_HEADS, LANES, past)
    vc = cache_diff_v[0].reshape(BS, past * N_HEADS, LANES)
    ods = _attn_sample(z3s, kc, vc, bias_past, bias_new, lam_rows, gain_sub, batch=BS, n_q=L, past=past)

    mgp = _post_a(orp, gp, odp, gatesp, wr_bf, wd_bf)
    mgs = _post_a(ors, gs, ods, gatess, wr_bf, wd_bf)

    w_rt = jnp.zeros((D, LANES), F32)
    w_rt = w_rt.at[:, 0:N_GROUPS].set(w_group_router[0])
    w_rt = w_rt.at[:, N_GROUPS:N_GROUPS + N_EXPERTS].set(
        jnp.transpose(w_expert_router[0], (1, 0, 2)).reshape(D, N_EXPERTS))
    w_rt_hi = w_rt.astype(BF16)
    w_rt_lo = (w_rt - w_rt_hi.astype(F32)).astype(BF16)
    w_router2 = jnp.concatenate([w_rt_hi, w_rt_lo], axis=1)
    bias_row = jnp.zeros((1, LANES), F32)
    bias_row = bias_row.at[0, 0:N_GROUPS].set(b_group_router[0])
    bias_row = bias_row.at[0, N_GROUPS:N_GROUPS + N_EXPERTS].set(b_expert_router[0].reshape(-1))

    x1_all, lg_all = _post_b(xp2, mgp, xs2, mgs, wo_bf, gain_ffn, w_router2)

    info, counts = _route(lg_all, bias_row)
    counts = counts[0, :N_EXPERTS].astype(jnp.int32)
    padded = (counts + EXPERT_BLOCK - 1) // EXPERT_BLOCK * EXPERT_BLOCK
    pad_end = jnp.cumsum(padded)
    offs = pad_end - padded
    e12 = info[:, 0:2].astype(jnp.int32)
    rank12 = info[:, 4:6].astype(jnp.int32)
    expert_ids = jnp.arange(N_EXPERTS, dtype=jnp.int32)
    offs_e = jnp.sum(jnp.where(e12[:, :, None] == expert_ids, offs, 0), axis=-1)
    dest = (offs_e + rank12).reshape(-1)
    n_assign = 2 * T
    n_rows = -(-n_assign // EXPERT_BLOCK) * EXPERT_BLOCK + N_EXPERTS * EXPERT_BLOCK
    n_blocks = n_rows // EXPERT_BLOCK
    token_id = (jnp.arange(n_assign, dtype=jnp.int32) // 2)
    row_token = jnp.zeros((n_rows,), jnp.int32).at[dest].set(token_id)
    block_start = jnp.arange(n_blocks, dtype=jnp.int32) * EXPERT_BLOCK
    block_expert = jnp.minimum(jnp.sum((pad_end[None, :] <= block_start[:, None]).astype(jnp.int32), axis=1),
                               N_EXPERTS - 1)
    n_used = (pad_end[-1:] // EXPERT_BLOCK).astype(jnp.int32)

    out_rows = _moe(row_token, block_expert, n_used, x1_all, gain_ffn,
                    w_expert_gate[0], w_expert_up[0], w_expert_down[0])

    y_p = _final(dest, x1_all, info, gain_fin, out_rows, tok_offset=0, n_tok=TP)
    y_s = _final(dest, x1_all, info, gain_fin, out_rows, tok_offset=TP, n_tok=TS)

    new_k_p = jnp.transpose(kt32, (0, 1, 5, 2, 3, 4))
    new_v_p = v32p.reshape(1, B, S, N_HEADS, HEAD_V)
    new_k_s = k32s.reshape(1, BS, L, N_HEADS, 2, DIFF_HEAD_DIM)
    new_v_s = v32s.reshape(1, BS, L, N_HEADS, HEAD_V)
    return (y_p.reshape(B, S, D), y_s.reshape(BS, L, D), new_k_p, new_v_p, ret_p[None],
            new_k_s, new_v_s, ret_s[None])
```

```python
import functools
import math

import jax
import jax.numpy as jnp
from jax import lax
from jax.experimental import pallas as pl
from jax.experimental.pallas import tpu as pltpu

F32 = jnp.float32
BF16 = jnp.bfloat16

D_MODEL = 2048
CHUNK = 64
N_HEADS = 8
RET_KEY_DIM = 64
HEAD_V = 128
DIFF_HEAD_DIM = 64
IN_WIDTH = 10240
NUM_BUCKETS = 32
MAX_DISTANCE = 128
N_GROUPS = 8
EXPERTS_PER_GROUP = 8
N_EXPERTS = 64
EXPERT_HIDDEN = 512
EXPERT_BLOCK = 128
NORM_EPS = 1e-6
NEG_INF = -1e30
LANES = 128
LOG2E = 1.4426950408889634
VT_ROWS = HEAD_V + 16

VMEM_LIMIT = 56 * 1024 * 1024


def _cparams(sem):
    return pltpu.CompilerParams(dimension_semantics=sem, vmem_limit_bytes=VMEM_LIMIT)


PROJ_TM = 512
PROJ_TN = 1024
PROJ_MC = 128
ATT_TK = 256


def _rotate_pairs(acc, cos, sin):
    outs = []
    lane = lax.broadcasted_iota(jnp.int32, (acc.shape[0], LANES), 1)
    first_half = (lane % 64) < 32
    for c in range(acc.shape[1] // LANES):
        xs = acc[:, c * LANES:(c + 1) * LANES]
        swapped = jnp.where(first_half, pltpu.roll(xs, 96, axis=1), pltpu.roll(xs, 32, axis=1))
        outs.append(xs * cos + swapped * sin)
    return outs


def _proj_kernel(x_ref, gain_ref, w_ref, rot_ref, *out_refs, transposed_k):
    if transposed_k:
        z_ref, g_ref, gates_ref, v32_ref, qt_ref, vt_ref, kt32_ref, h_scr = out_refs
    else:
        z_ref, g_ref, gates_ref, v32_ref, k32_ref, h_scr = out_refs
    j = pl.program_id(1)

    @pl.when(j == 0)
    def _():
        x = x_ref[...]
        ms = jnp.mean(x * x, axis=-1, keepdims=True)
        h_scr[...] = (x * lax.rsqrt(ms + NORM_EPS) * gain_ref[...]).astype(BF16)

    n_slab = PROJ_TN // LANES

    def for_row_chunks(epilogue):
        for c in range(PROJ_TM // PROJ_MC):
            rows = slice(c * PROJ_MC, (c + 1) * PROJ_MC)
            epilogue(c, rows, jnp.dot(h_scr[rows, :], w_ref[...], preferred_element_type=F32))

    def store_slabs(rows, vals):
        for s in range(n_slab):
            z_ref[s, rows, :] = vals[s].astype(BF16)

    def split(a):
        return [a[:, s * LANES:(s + 1) * LANES] for s in range(n_slab)]

    @pl.when(j == 0)
    def _():
        def epilogue(c, rows, acc):
            half = PROJ_TN // 2
            q = _rotate_pairs(acc[:, :half], rot_ref[rows, 0:128], rot_ref[rows, 128:256])
            k = _rotate_pairs(acc[:, half:], rot_ref[rows, 256:384], rot_ref[rows, 384:512])
            store_slabs(rows, q + k)
        for_row_chunks(epilogue)

    @pl.when(j == 1)
    def _():
        for_row_chunks(lambda c, rows, acc: store_slabs(rows, split(acc)))

    @pl.when(j == 2)
    def _():
        def epilogue(c, rows, acc):
            g_ref[rows, :] = (acc * jax.nn.sigmoid(acc)).astype(BF16)
        for_row_chunks(epilogue)

    @pl.when(j == 3)
    def _():
        def epilogue(c, rows, acc):
            qs = acc * (DIFF_HEAD_DIM ** -0.5 * LOG2E)
            store_slabs(rows, split(qs))
            if transposed_k:
                qs_t = qs.T.astype(BF16)
                for hh in range(n_slab):
                    qt_ref[0, hh, :, rows] = qs_t[hh * LANES:(hh + 1) * LANES, :]
        for_row_chunks(epilogue)

    @pl.when(j == 4)
    def _():
        def epilogue(c, rows, acc):
            store_slabs(rows, split(acc))
            if transposed_k:
                kt32_ref[0, 0, :, :, :, rows] = acc.T.reshape(n_slab, 2, DIFF_HEAD_DIM, PROJ_MC)
            else:
                k32_ref[rows, :] = acc
        for_row_chunks(epilogue)

    @pl.when(j == 5)
    def _():
        def epilogue(c, rows, acc):
            store_slabs(rows, split(acc))
            v32_ref[rows, :] = acc
            if transposed_k:
                acc_tb = acc.T.astype(BF16)
                t, cols = (c * PROJ_MC) // ATT_TK, slice((c * PROJ_MC) % ATT_TK, (c * PROJ_MC) % ATT_TK + PROJ_MC)
                ones = jnp.ones((VT_ROWS - HEAD_V, PROJ_MC), BF16)
                for hh in range(n_slab):
                    vt_ref[0, hh, t, 0:HEAD_V, cols] = acc_tb[hh * LANES:(hh + 1) * LANES, :]
                    vt_ref[0, hh, t, HEAD_V:VT_ROWS, cols] = ones
        for_row_chunks(epilogue)

    @pl.when(j >= 6)
    def _():
        def epilogue(c, rows, acc):
            gates_ref[rows, :] = jax.nn.sigmoid(acc).astype(BF16)
        for_row_chunks(epilogue)


def _proj(x2d, gain, w_bf, rot, *, seq, transposed_k):
    T = x2d.shape[0]
    tm, tn = PROJ_TM, PROJ_TN
    ni, nj = T // tm, IN_WIDTH // tn
    rot_blocks = rot.shape[0] // tm

    def zmap(i, j):
        zj = jnp.where(j < 2, j, jnp.where(j < 3, 1, jnp.where(j < 6, j - 1, 4)))
        return (zj, i, 0)

    out_shape = [
        jax.ShapeDtypeStruct((40, T, LANES), BF16),
        jax.ShapeDtypeStruct((T, 1024), BF16),
        jax.ShapeDtypeStruct((T, 4096), BF16),
        jax.ShapeDtypeStruct((T, 1024), F32),
    ]
    out_specs = [
        pl.BlockSpec((8, tm, LANES), zmap),
        pl.BlockSpec((tm, tn), lambda i, j: (i, 0)),
        pl.BlockSpec((tm, tn), lambda i, j: (i, jnp.clip(j - 6, 0, 3))),
        pl.BlockSpec((tm, tn), lambda i, j: (i, 0)),
    ]
    if transposed_k:
        B = T // seq
        spb = seq // tm
        out_shape += [
            jax.ShapeDtypeStruct((B, N_HEADS, LANES, seq), BF16),
            jax.ShapeDtypeStruct((B, N_HEADS, seq // ATT_TK, VT_ROWS, ATT_TK), BF16),
            jax.ShapeDtypeStruct((1, B, N_HEADS, 2, DIFF_HEAD_DIM, seq), F32),
        ]
        out_specs += [
            pl.BlockSpec((1, N_HEADS, LANES, tm), lambda i, j: (i // spb, 0, 0, i % spb)),
            pl.BlockSpec((1, N_HEADS, tm // ATT_TK, VT_ROWS, ATT_TK), lambda i, j: (i // spb, 0, i % spb, 0, 0)),
            pl.BlockSpec((1, 1, N_HEADS, 2, DIFF_HEAD_DIM, tm), lambda i, j: (0, i // spb, 0, 0, 0, i % spb)),
        ]
    else:
        out_shape += [jax.ShapeDtypeStruct((T, 1024), F32)]
        out_specs += [pl.BlockSpec((tm, tn), lambda i, j: (i, 0))]

    return pl.pallas_call(
        functools.partial(_proj_kernel, transposed_k=transposed_k),
        out_shape=out_shape,
        grid=(ni, nj),
        in_specs=[
            pl.BlockSpec((tm, D_MODEL), lambda i, j: (i, 0)),
            pl.BlockSpec((1, D_MODEL), lambda i, j: (0, 0)),
            pl.BlockSpec((D_MODEL, tn), lambda i, j: (0, j)),
            pl.BlockSpec((tm, 512), lambda i, j: (i % rot_blocks, 0)),
        ],
        out_specs=out_specs,
        scratch_shapes=[pltpu.VMEM((tm, D_MODEL), BF16)],
        compiler_params=_cparams(("arbitrary", "arbitrary")),
        name="proj_t" if transposed_k else "proj",
    )(x2d, gain, w_bf, rot)


def _retention_kernel(q_ref, k_ref, v_ref, decay_ref, inner_ref, outer_ref, st0_ref,
                      o_ref, st_out_ref, st_scr, *, chunk):
    c = pl.program_id(1)
    nc = pl.num_programs(1)

    @pl.when(c == 0)
    def _():
        st_scr[...] = jnp.zeros_like(st_scr)
        for h in range(N_HEADS):
            a = h % 2
            st_scr[h, a * RET_KEY_DIM:(a + 1) * RET_KEY_DIM, :] = st0_ref[0, h]

    lane_lo = lax.broadcasted_iota(jnp.int32, (chunk, LANES), 1) < RET_KEY_DIM
    row_lo = lax.broadcasted_iota(jnp.int32, (LANES, LANES), 0) < RET_KEY_DIM
    for h in range(N_HEADS):
        p, a = h // 2, h % 2
        q = q_ref[p]
        k = k_ref[p]
        v = v_ref[h]
        qa = jnp.where(lane_lo == (a == 0), q, jnp.zeros_like(q))
        s = lax.dot_general(qa, k, (((1,), (1,)), ((), ())), preferred_element_type=F32)
        s = s * decay_ref[h]
        st = st_scr[h]
        inner = inner_ref[h]
        o = (jnp.dot(s.astype(BF16), v, preferred_element_type=F32)
             + jnp.dot(qa, st.astype(BF16), preferred_element_type=F32) * inner)
        ko = (k.astype(F32) * outer_ref[p]).astype(BF16)
        upd = lax.dot_general(ko, v, (((0,), (0,)), ((), ())), preferred_element_type=F32)
        upd = jnp.where(row_lo == (a == 0), upd, 0.0)
        chunk_decay = inner[chunk - 1:chunk, :]
        st_scr[h] = chunk_decay * st + upd
        o_n = o * lax.rsqrt(jnp.mean(o * o, axis=-1, keepdims=True) + NORM_EPS)
        o_ref[:, h * HEAD_V:(h + 1) * HEAD_V] = o_n.astype(BF16)

    @pl.when(c == nc - 1)
    def _():
        for h in range(N_HEADS):
            a = h % 2
            st_out_ref[0, h] = st_scr[h, a * RET_KEY_DIM:(a + 1) * RET_KEY_DIM, :]


def _retention_consts(chunk):
    log_g = jnp.log(1.0 - 2.0 ** (-5.0 - jnp.arange(N_HEADS, dtype=F32)))
    j = jnp.arange(chunk, dtype=F32)
    diff = j[:, None] - j[None, :]
    decay = jnp.where(diff >= 0, jnp.exp(log_g[:, None, None] * jnp.maximum(diff, 0.0)), 0.0)
    inner = jnp.exp(log_g[:, None] * (j + 1.0))
    outer = jnp.exp(log_g[:, None] * (chunk - 1.0 - j))
    inner_b = jnp.broadcast_to(inner[:, :, None], (N_HEADS, chunk, LANES))
    outer_pair = jnp.repeat(outer.reshape(N_HEADS // 2, 2, chunk).transpose(0, 2, 1), RET_KEY_DIM, axis=2)
    return decay, inner_b, outer_pair


def _retention(z3, state0, *, batch, seq, chunk):
    T = batch * seq
    nc = seq // chunk
    decay, inner_b, outer_pair = _retention_consts(chunk)
    return pl.pallas_call(
        functools.partial(_retention_kernel, chunk=chunk),
        out_shape=[jax.ShapeDtypeStruct((T, 1024), BF16),
                   jax.ShapeDtypeStruct((batch, N_HEADS, RET_KEY_DIM, HEAD_V), F32)],
        grid=(batch, nc),
        in_specs=[
            pl.BlockSpec((4, chunk, LANES), lambda b, c: (0, b * nc + c, 0)),
            pl.BlockSpec((4, chunk, LANES), lambda b, c: (1, b * nc + c, 0)),
            pl.BlockSpec((8, chunk, LANES), lambda b, c: (1, b * nc + c, 0)),
            pl.BlockSpec((N_HEADS, chunk, chunk), lambda b, c: (0, 0, 0)),
            pl.BlockSpec((N_HEADS, chunk, LANES), lambda b, c: (0, 0, 0)),
            pl.BlockSpec((N_HEADS // 2, chunk, LANES), lambda b, c: (0, 0, 0)),
            pl.BlockSpec((1, N_HEADS, RET_KEY_DIM, HEAD_V), lambda b, c: (b, 0, 0, 0)),
        ],
        out_specs=[
            pl.BlockSpec((chunk, 1024), lambda b, c: (b * nc + c, 0)),
            pl.BlockSpec((1, N_HEADS, RET_KEY_DIM, HEAD_V), lambda b, c: (b, 0, 0, 0)),
        ],
        scratch_shapes=[pltpu.VMEM((N_HEADS, LANES, LANES), F32)],
        compiler_params=_cparams(("arbitrary", "arbitrary")),
        name=f"retention_c{chunk}",
    )(z3, z3, z3, decay, inner_b, outer_pair, state0)


def _lam_init(layer=0):
    return 0.8 - 0.6 * math.exp(-0.3 * layer)


def _lam_from_ref(lam_ref):
    lp = lam_ref[...]
    s1 = jnp.sum(lp[0:1] * lp[1:2], axis=-1, keepdims=True)
    s2 = jnp.sum(lp[2:3] * lp[3:4], axis=-1, keepdims=True)
    return jnp.exp(s1) - jnp.exp(s2) + _lam_init()


def _stack_q(q):
    lane_lo = lax.broadcasted_iota(jnp.int32, q.shape, 1) < DIFF_HEAD_DIM
    zero = jnp.zeros_like(q)
    return jnp.concatenate([jnp.where(lane_lo, q, zero), jnp.where(lane_lo, zero, q)], axis=0)


def _softmax_step(s, v, m_ref, l_ref, acc_ref):
    m_prev = m_ref[...]
    m_new = jnp.maximum(m_prev, jnp.max(s, axis=1, keepdims=True))
    alpha = jnp.exp2(m_prev - m_new)
    p = jnp.exp2(s - m_new[:, 0:1])
    l_ref[...] = alpha * l_ref[...] + jnp.sum(p, axis=1, keepdims=True)
    acc_ref[...] = alpha * acc_ref[...] + jnp.dot(p.astype(BF16), v, preferred_element_type=F32)
    m_ref[...] = m_new


def _diff_finish(m_ref, l_ref, acc_ref, lam, gain, n):
    o = acc_ref[...] / l_ref[...]
    o = o[:n] - lam * o[n:]
    o = o * lax.rsqrt(jnp.mean(o * o, axis=-1, keepdims=True) + NORM_EPS) * gain
    return o * (1.0 - _lam_init())


def _relative_bias(q_pos, k_pos, table):
    rel = k_pos[None, :] - q_pos[:, None]
    half = NUM_BUCKETS // 2
    max_exact = half // 2
    n = jnp.abs(rel)
    log_ratio = jnp.log(jnp.maximum(n, 1).astype(F32) / max_exact) / math.log(MAX_DISTANCE / max_exact)
    large = jnp.minimum(max_exact + (log_ratio * (half - max_exact)).astype(jnp.int32), half - 1)
    bucket = (rel > 0).astype(jnp.int32) * half + jnp.where(n < max_exact, n, large)
    out = jnp.zeros((table.shape[1],) + bucket.shape, F32)
    for bkt in range(NUM_BUCKETS):
        out = jnp.where(bucket[None] == bkt, table[bkt].astype(F32)[:, None, None], out)
    return out


def _far_bias(table):
    return table[NUM_BUCKETS // 2 - 1].astype(F32)


ATT_TQ = 256
ATT_HP = 4


def _attn_prompt_kernel(qt_ref, k_ref, vt_ref, bprev_ref, bdiag_ref, lam_ref, gain_ref, o_ref, acc_scr):
    i = pl.program_id(2)
    row_lo = lax.broadcasted_iota(jnp.int32, (LANES, ATT_TQ), 0) < DIFF_HEAD_DIM
    qs = []
    for hh in range(ATT_HP):
        qt = qt_ref[0, hh]
        zero = jnp.zeros_like(qt)
        qs.append(jnp.concatenate([jnp.where(row_lo, qt, zero), jnp.where(row_lo, zero, qt)], axis=1))
    acc_scr[...] = jnp.zeros_like(acc_scr)

    def tile(j, ms, biases):
        ss = []
        for hh in range(ATT_HP):
            kt = k_ref[hh, pl.ds(pl.multiple_of(j * ATT_TK, ATT_TK), ATT_TK), :]
            s = jnp.dot(kt, qs[hh], preferred_element_type=F32)
            ss.append(s if biases is None else s + biases[hh])
        ps, alphas, m_out = [], [], []
        for hh in range(ATT_HP):
            m_new = jnp.maximum(ms[hh], jnp.max(ss[hh], axis=0, keepdims=True))
            alphas.append(jnp.exp2(ms[hh] - m_new))
            ps.append(jnp.exp2(ss[hh] - m_new).astype(BF16))
            m_out.append(m_new)
        for hh in range(ATT_HP):
            acc_scr[hh] = alphas[hh] * acc_scr[hh] + jnp.dot(vt_ref[0, hh, j], ps[hh],
                                                             preferred_element_type=F32)
        return tuple(m_out)

    ms = tuple(jnp.full((1, 2 * ATT_TQ), NEG_INF, F32) for _ in range(ATT_HP))
    ms = lax.fori_loop(0, i - 1, lambda j, c: tile(j, c, None), ms)
    jp = jnp.maximum(i - 1, 0)
    ms = lax.cond(i >= 1,
                  lambda c: tile(jp, c, [bprev_ref[hh] for hh in range(ATT_HP)]),
                  lambda c: c, ms)
    tile(i, ms, [bdiag_ref[hh] for hh in range(ATT_HP)])

    lam = _lam_from_ref(lam_ref)
    for hh in range(ATT_HP):
        o = acc_scr[hh, 0:HEAD_V, :] / acc_scr[hh, HEAD_V:HEAD_V + 1, :]
        o = o[:, :ATT_TQ] - lam * o[:, ATT_TQ:]
        o = o * lax.rsqrt(jnp.mean(o * o, axis=0, keepdims=True) + NORM_EPS) * gain_ref[...]
        o_ref[:, hh * HEAD_V:(hh + 1) * HEAD_V] = (o * (1.0 - _lam_init())).T.astype(BF16)


def _attn_prompt(z3, qt, vt5, bias_prev, bias_diag, lam_rows, gain_col, *, batch, seq):
    T = batch * seq
    nq = seq // ATT_TQ
    hp = ATT_HP
    return pl.pallas_call(
        _attn_prompt_kernel,
        out_shape=jax.ShapeDtypeStruct((T, 1024), BF16),
        grid=(batch, N_HEADS // hp, nq),
        in_specs=[
            pl.BlockSpec((1, hp, LANES, ATT_TQ), lambda b, g, i: (b, g, 0, i)),
            pl.BlockSpec((hp, seq, LANES), lambda b, g, i: (24 // hp + g, b, 0)),
            pl.BlockSpec((1, hp, seq // ATT_TK, VT_ROWS, ATT_TK), lambda b, g, i: (b, g, 0, 0, 0)),
            pl.BlockSpec((hp, ATT_TK, 2 * ATT_TQ), lambda b, g, i: (g, 0, 0)),
            pl.BlockSpec((hp, ATT_TK, 2 * ATT_TQ), lambda b, g, i: (g, 0, 0)),
            pl.BlockSpec((8, LANES), lambda b, g, i: (0, 0)),
            pl.BlockSpec((LANES, ATT_TQ), lambda b, g, i: (0, 0)),
        ],
        out_specs=pl.BlockSpec((ATT_TQ, hp * HEAD_V), lambda b, g, i: (b * nq + i, g)),
        scratch_shapes=[pltpu.VMEM((hp, VT_ROWS, 2 * ATT_TQ), F32)],
        compiler_params=_cparams(("arbitrary", "arbitrary", "arbitrary")),
        name="attn_prompt",
    )(qt, z3, vt5, bias_prev, bias_diag, lam_rows, gain_col)


SAMPLE_TK = 512


def _attn_sample_kernel(q_ref, kc_ref, vc_ref, kn_ref, vn_ref, bpast_ref, bnew_ref, lam_ref, gain_ref,
                        o_ref, m_scr, l_scr, acc_scr, *, n_past_tiles, n_q):
    t = pl.program_id(1)

    @pl.when(t == 0)
    def _():
        m_scr[...] = jnp.full_like(m_scr, NEG_INF)
        l_scr[...] = jnp.zeros_like(l_scr)
        acc_scr[...] = jnp.zeros_like(acc_scr)

    @pl.when(t < n_past_tiles)
    def _():
        ss = []
        for h in range(N_HEADS):
            kt = kc_ref[0, h].astype(BF16)
            ss.append(jnp.dot(_stack_q(q_ref[h]), kt, preferred_element_type=F32) + bpast_ref[0, h])
        ps, alphas = [], []
        for h in range(N_HEADS):
            m_prev = m_scr[h]
            m_new = jnp.maximum(m_prev, jnp.max(ss[h], axis=1, keepdims=True))
            alpha = jnp.exp2(m_prev - m_new)
            p = jnp.exp2(ss[h] - m_new[:, 0:1])
            l_scr[h] = alpha * l_scr[h] + jnp.sum(p, axis=1, keepdims=True)
            m_scr[h] = m_new
            ps.append(p.astype(BF16))
            alphas.append(alpha)
        for h in range(N_HEADS):
            v = vc_ref[0, pl.ds(h, SAMPLE_TK, stride=N_HEADS), :].astype(BF16)
            acc_scr[h] = alphas[h] * acc_scr[h] + jnp.dot(ps[h], v, preferred_element_type=F32)

    @pl.when(t == n_past_tiles)
    def _():
        lam = _lam_from_ref(lam_ref)
        for h in range(N_HEADS):
            qs = _stack_q(q_ref[h])
            s = lax.dot_general(qs, kn_ref[h], (((1,), (1,)), ((), ())), preferred_element_type=F32)
            s = s + bnew_ref[h]
            _softmax_step(s, vn_ref[h], m_scr.at[h], l_scr.at[h], acc_scr.at[h])
            o = _diff_finish(m_scr.at[h], l_scr.at[h], acc_scr.at[h], lam, gain_ref[...], n_q)
            o_ref[:, h * HEAD_V:(h + 1) * HEAD_V] = o.astype(BF16)


def _attn_sample(z3, kc, vc, bias_past, bias_new, lam_rows, gain, *, batch, n_q, past):
    npt = past // SAMPLE_TK
    return pl.pallas_call(
        functools.partial(_attn_sample_kernel, n_past_tiles=npt, n_q=n_q),
        out_shape=jax.ShapeDtypeStruct((batch * n_q, 1024), BF16),
        grid=(batch, npt + 1),
        in_specs=[
            pl.BlockSpec((8, n_q, LANES), lambda b, t: (2, b, 0)),
            pl.BlockSpec((1, N_HEADS, LANES, SAMPLE_TK), lambda b, t: (b, 0, 0, jnp.minimum(t, npt - 1))),
            pl.BlockSpec((1, SAMPLE_TK * N_HEADS, LANES), lambda b, t: (b, jnp.minimum(t, npt - 1), 0)),
            pl.BlockSpec((8, n_q, LANES), lambda b, t: (3, b, 0)),
            pl.BlockSpec((8, n_q, LANES), lambda b, t: (4, b, 0)),
            pl.BlockSpec((1, N_HEADS, 2 * n_q, SAMPLE_TK), lambda b, t: (jnp.where(t == npt - 1, 1, 0), 0, 0, 0)),
            pl.BlockSpec((N_HEADS, 2 * n_q, n_q), lambda b, t: (0, 0, 0)),
            pl.BlockSpec((8, LANES), lambda b, t: (0, 0)),
            pl.BlockSpec((1, LANES), lambda b, t: (0, 0)),
        ],
        out_specs=pl.BlockSpec((n_q, 1024), lambda b, t: (b, 0)),
        scratch_shapes=[pltpu.VMEM((N_HEADS, 2 * n_q, LANES), F32)] * 3,
        compiler_params=_cparams(("arbitrary", "arbitrary")),
        name="attn_sample",
    )(z3, kc, vc, z3, z3, bias_past, bias_new, lam_rows, gain)


POST_TM = 256


def _post_a_kernel(or_ref, g_ref, od_ref, gr_ref, gd_ref, wr_ref, wd_ref, out_ref):
    a = jnp.dot(or_ref[...] * g_ref[...], wr_ref[...], preferred_element_type=F32)
    b = jnp.dot(od_ref[...], wd_ref[...], preferred_element_type=F32)
    out_ref[...] = (gr_ref[...].astype(F32) * a + gd_ref[...].astype(F32) * b).astype(BF16)


def _post_a(o_r, g, o_d, gates, wr_bf, wd_bf):
    T = o_r.shape[0]
    tm = POST_TM
    return pl.pallas_call(
        _post_a_kernel,
        out_shape=jax.ShapeDtypeStruct((T, D_MODEL), BF16),
        grid=(T // tm,),
        in_specs=[
            pl.BlockSpec((tm, 1024), lambda i: (i, 0)),
            pl.BlockSpec((tm, 1024), lambda i: (i, 0)),
            pl.BlockSpec((tm, 1024), lambda i: (i, 0)),
            pl.BlockSpec((tm, D_MODEL), lambda i: (i, 0)),
            pl.BlockSpec((tm, D_MODEL), lambda i: (i, 1)),
            pl.BlockSpec((1024, D_MODEL), lambda i: (0, 0)),
            pl.BlockSpec((1024, D_MODEL), lambda i: (0, 0)),
        ],
        out_specs=pl.BlockSpec((tm, D_MODEL), lambda i: (i, 0)),
        compiler_params=_cparams(("arbitrary",)),
        name="post_a",
    )(o_r, g, o_d, gates, gates, wr_bf, wd_bf)


def _post_b_kernel(xp_ref, mp_ref, xs_ref, ms_ref, wo_ref, gain_ref, wrt_ref, x1_ref, lg_ref, *, n_prompt_blocks):
    i = pl.program_id(0)
    tm = POST_TM

    def body(x_ref, mg_ref):
        x1 = x_ref[...] + jnp.dot(mg_ref[...], wo_ref[...], preferred_element_type=F32)
        x1_ref[...] = x1
        h2 = x1 * lax.rsqrt(jnp.mean(x1 * x1, axis=-1, keepdims=True) + NORM_EPS) * gain_ref[...]
        h_hi = h2.astype(BF16)
        h_lo = (h2 - h_hi.astype(F32)).astype(BF16)
        r = jnp.dot(jnp.concatenate([h_hi, h_lo], axis=0), wrt_ref[...], preferred_element_type=F32)
        lg_ref[...] = (r[:tm, :LANES] + r[:tm, LANES:]) + (r[tm:, :LANES] + r[tm:, LANES:])

    @pl.when(i < n_prompt_blocks)
    def _():
        body(xp_ref, mp_ref)

    @pl.when(i >= n_prompt_blocks)
    def _():
        body(xs_ref, ms_ref)


def _post_b(xp2, mgp, xs2, mgs, wo_bf, gain, w_router2):
    tm = POST_TM
    npb, nsb = xp2.shape[0] // tm, xs2.shape[0] // tm
    T = xp2.shape[0] + xs2.shape[0]
    pmap = lambda i: (jnp.minimum(i, npb - 1), 0)
    smap = lambda i: (jnp.maximum(i - npb, 0), 0)
    return pl.pallas_call(
        functools.partial(_post_b_kernel, n_prompt_blocks=npb),
        out_shape=[jax.ShapeDtypeStruct((T, D_MODEL), F32),
                   jax.ShapeDtypeStruct((T, LANES), F32)],
        grid=(npb + nsb,),
        in_specs=[
            pl.BlockSpec((tm, D_MODEL), pmap),
            pl.BlockSpec((tm, D_MODEL), pmap),
            pl.BlockSpec((tm, D_MODEL), smap),
            pl.BlockSpec((tm, D_MODEL), smap),
            pl.BlockSpec((D_MODEL, D_MODEL), lambda i: (0, 0)),
            pl.BlockSpec((1, D_MODEL), lambda i: (0, 0)),
            pl.BlockSpec((D_MODEL, 2 * LANES), lambda i: (0, 0)),
        ],
        out_specs=[pl.BlockSpec((tm, D_MODEL), lambda i: (i, 0)),
                   pl.BlockSpec((tm, LANES), lambda i: (i, 0))],
        compiler_params=_cparams(("arbitrary",)),
        name="post_b",
    )(xp2, mgp, xs2, mgs, wo_bf, gain, w_router2)


ROUTE_TM = 512


def _route_kernel(lg_ref, bias_ref, info_ref, cnt_ref, tri_scr, carry_scr):
    i = pl.program_id(0)
    tm = ROUTE_TM

    @pl.when(i == 0)
    def _():
        r = lax.broadcasted_iota(jnp.int32, (tm, tm), 0)
        c = lax.broadcasted_iota(jnp.int32, (tm, tm), 1)
        tri_scr[...] = jnp.where(c < r, 1.0, 0.0).astype(BF16)
        carry_scr[...] = jnp.zeros_like(carry_scr)

    lg = lg_ref[...] + bias_ref[...]
    lane = lax.broadcasted_iota(jnp.int32, (tm, LANES), 1)
    lane_f = lane.astype(F32)
    neg = jnp.float32(-jnp.inf)

    def first_argmax(vals):
        top = jnp.max(vals, axis=1, keepdims=True)
        idx = jnp.min(jnp.where(vals == top, lane_f, float(LANES)), axis=1, keepdims=True)
        return top, idx

    is_group = lane < N_GROUPS
    gl = jnp.where(is_group, lg, neg)
    g_top, g_idx = first_argmax(gl)
    g_weight = 1.0 / jnp.sum(jnp.exp(gl - g_top), axis=1, keepdims=True)
    lane_group = ((lane - N_GROUPS) >> 3).astype(F32)
    in_group = (lane >= N_GROUPS) & (lane < N_GROUPS + N_EXPERTS) & (lane_group == g_idx)
    el = jnp.where(in_group, lg, neg)
    t1, i1 = first_argmax(el)
    el2 = jnp.where(lane_f == i1, neg, el)
    t2, i2 = first_argmax(el2)
    e2w = jnp.exp(t2 - t1)
    p1 = 1.0 / (1.0 + e2w)
    gate1 = g_weight * p1
    gate2 = g_weight * (e2w * p1)
    e1 = i1 - float(N_GROUPS)
    e2 = i2 - float(N_GROUPS)

    hot1 = lane_f == e1
    hot2 = lane_f == e2
    both = jnp.where(hot1 | hot2, 1.0, 0.0)
    prefix = jnp.dot(tri_scr[...], both.astype(BF16), preferred_element_type=F32) + carry_scr[...]
    rank1 = jnp.sum(jnp.where(hot1, prefix, 0.0), axis=1, keepdims=True)
    rank2 = jnp.sum(jnp.where(hot2, prefix, 0.0), axis=1, keepdims=True)
    carry_scr[...] = carry_scr[...] + jnp.sum(both, axis=0, keepdims=True)

    info = jnp.where(lane == 0, e1, jnp.where(lane == 1, e2, jnp.where(lane == 2, gate1, jnp.where(
        lane == 3, gate2, jnp.where(lane == 4, rank1, jnp.where(lane == 5, rank2, 0.0))))))
    info_ref[...] = info
    cnt_ref[...] = carry_scr[...]


def _route(lg_all, bias_row):
    T = lg_all.shape[0]
    tm = ROUTE_TM
    return pl.pallas_call(
        _route_kernel,
        out_shape=[jax.ShapeDtypeStruct((T, LANES), F32), jax.ShapeDtypeStruct((1, LANES), F32)],
        grid=(T // tm,),
        in_specs=[pl.BlockSpec((tm, LANES), lambda i: (i, 0)), pl.BlockSpec((1, LANES), lambda i: (0, 0))],
        out_specs=[pl.BlockSpec((tm, LANES), lambda i: (i, 0)), pl.BlockSpec((1, LANES), lambda i: (0, 0))],
        scratch_shapes=[pltpu.VMEM((tm, tm), BF16), pltpu.VMEM((1, LANES), F32)],
        compiler_params=_cparams(("arbitrary",)),
        name="route",
    )(lg_all, bias_row)


def _moe_kernel(tok_ref, bexp_ref, nxt_ref, nused_ref, x1_hbm, gain_ref, wg_hbm, wu_hbm, wd_hbm,
                out_ref, xbuf, gsem, wgbuf, wubuf, wdbuf, wsem, wg_bf, wu_bf, wd_bf, wslot, *, n_blocks):
    b = pl.program_id(0)
    n_used = nused_ref[0]
    slot = b % 2

    def gather_copy(row, s, r):
        return pltpu.make_async_copy(x1_hbm.at[pl.ds(row, 1)], xbuf.at[s, pl.ds(r, 1)], gsem.at[s])

    def weight_copies(e, s):
        return (pltpu.make_async_copy(wg_hbm.at[e], wgbuf.at[s], wsem.at[s]),
                pltpu.make_async_copy(wu_hbm.at[e], wubuf.at[s], wsem.at[s]),
                pltpu.make_async_copy(wd_hbm.at[e], wdbuf.at[s], wsem.at[s]))

    def issue(blk, s):
        base = blk * EXPERT_BLOCK

        def body(r, carry):
            gather_copy(tok_ref[base + r], s, r).start()
            return carry

        lax.fori_loop(0, EXPERT_BLOCK, body, 0, unroll=8)

    @pl.when(b == 0)
    def _():
        issue(0, 0)
        for cp in weight_copies(bexp_ref[0], 0):
            cp.start()
        wslot[0] = 0

    @pl.when(b < n_used)
    def _():
        e = bexp_ref[b]
        new_expert = (b == 0) | (e != bexp_ref[jnp.maximum(b - 1, 0)])

        @pl.when(new_expert)
        def _():
            @pl.when(b > 0)
            def _():
                wslot[0] = 1 - wslot[0]

            s = wslot[0]
            for cp in weight_copies(e, s):
                cp.wait()
            nxt = nxt_ref[b]

            @pl.when(nxt >= 0)
            def _():
                for cp in weight_copies(nxt, 1 - s):
                    cp.start()

            wg_bf[...] = wgbuf[s].astype(BF16)
            wu_bf[...] = wubuf[s].astype(BF16)
            wd_bf[...] = wdbuf[s].astype(BF16)

    def expert_block(cur):
        pltpu.make_async_copy(x1_hbm.at[pl.ds(0, EXPERT_BLOCK)], xbuf.at[cur], gsem.at[cur]).wait()
        base = jnp.minimum(b + 1, n_blocks - 1) * EXPERT_BLOCK
        for r in range(EXPERT_BLOCK):
            gather_copy(tok_ref[base + r], 1 - cur, r).start()
        x = xbuf[cur]
        h = (x * lax.rsqrt(jnp.mean(x * x, axis=-1, keepdims=True) + NORM_EPS) * gain_ref[...]).astype(BF16)
        g = jnp.dot(h, wg_bf[...], preferred_element_type=F32)
        u = jnp.dot(h, wu_bf[...], preferred_element_type=F32)
        a = (g * jax.nn.sigmoid(g) * u).astype(BF16)
        out_ref[...] = jnp.dot(a, wd_bf[...], preferred_element_type=F32)

    for cur in range(2):
        pl.when((b < n_used) & (slot == cur))(functools.partial(expert_block, cur))

    @pl.when(b == n_used)
    def _():
        pltpu.make_async_copy(x1_hbm.at[pl.ds(0, EXPERT_BLOCK)], xbuf.at[slot], gsem.at[slot]).wait()

    @pl.when((b >= n_used) & (b < n_blocks))
    def _():
        out_ref[...] = jnp.zeros_like(out_ref)


def _moe(row_token, block_expert, next_expert, n_used, x1_all, gain, wg, wu, wd):
    n_rows = row_token.shape[0]
    n_blocks = n_rows // EXPERT_BLOCK
    grid_spec = pltpu.PrefetchScalarGridSpec(
        num_scalar_prefetch=4,
        grid=(n_blocks + 1,),
        in_specs=[
            pl.BlockSpec(memory_space=pl.ANY),
            pl.BlockSpec((1, D_MODEL), lambda b, *_: (0, 0)),
            pl.BlockSpec(memory_space=pl.ANY),
            pl.BlockSpec(memory_space=pl.ANY),
            pl.BlockSpec(memory_space=pl.ANY),
        ],
        out_specs=pl.BlockSpec((EXPERT_BLOCK, D_MODEL), lambda b, *_: (jnp.minimum(b, n_blocks - 1), 0)),
        scratch_shapes=[
            pltpu.VMEM((2, EXPERT_BLOCK, D_MODEL), F32),
            pltpu.SemaphoreType.DMA((2,)),
            pltpu.VMEM((2, D_MODEL, EXPERT_HIDDEN), F32),
            pltpu.VMEM((2, D_MODEL, EXPERT_HIDDEN), F32),
            pltpu.VMEM((2, EXPERT_HIDDEN, D_MODEL), F32),
            pltpu.SemaphoreType.DMA((2,)),
            pltpu.VMEM((D_MODEL, EXPERT_HIDDEN), BF16),
            pltpu.VMEM((D_MODEL, EXPERT_HIDDEN), BF16),
            pltpu.VMEM((EXPERT_HIDDEN, D_MODEL), BF16),
            pltpu.SMEM((1,), jnp.int32),
        ],
    )
    return pl.pallas_call(
        functools.partial(_moe_kernel, n_blocks=n_blocks),
        out_shape=jax.ShapeDtypeStruct((n_rows, D_MODEL), F32),
        grid_spec=grid_spec,
        compiler_params=_cparams(("arbitrary",)),
        name="moe",
    )(row_token, block_expert, next_expert, n_used, x1_all, gain, wg, wu, wd)


FINAL_TM = 256


def _final_kernel(dest_ref, x1_ref, info_ref, gain_ref, rows_hbm, y_ref, rbuf, sem, *, tok_offset):
    i = pl.program_id(0)
    n = pl.num_programs(0)
    tm = FINAL_TM
    slot = i % 2

    def issue(blk, s):
        base = 2 * (tok_offset + blk * tm)

        def body(r, carry):
            for kk in range(2):
                pltpu.make_async_copy(rows_hbm.at[pl.ds(dest_ref[base + 2 * r + kk], 1)],
                                      rbuf.at[s, kk, pl.ds(r, 1)], sem.at[s]).start()
            return carry

        lax.fori_loop(0, tm, body, 0, unroll=4)

    @pl.when(i == 0)
    def _():
        issue(0, 0)

    @pl.when(i + 1 < n)
    def _():
        issue(i + 1, 1 - slot)

    for kk in range(2):
        pltpu.make_async_copy(rows_hbm.at[pl.ds(0, tm)], rbuf.at[slot, kk], sem.at[slot]).wait()
    info = info_ref[...]
    x = x1_ref[...] + info[:, 2:3] * rbuf[slot, 0] + info[:, 3:4] * rbuf[slot, 1]
    y_ref[...] = x * lax.rsqrt(jnp.mean(x * x, axis=-1, keepdims=True) + NORM_EPS) * gain_ref[...]


def _final(dest, x1_all, info, gain, out_rows, *, tok_offset, n_tok):
    tm = FINAL_TM
    off = tok_offset // tm
    grid_spec = pltpu.PrefetchScalarGridSpec(
        num_scalar_prefetch=1,
        grid=(n_tok // tm,),
        in_specs=[
            pl.BlockSpec((tm, D_MODEL), lambda i, d: (i + off, 0)),
            pl.BlockSpec((tm, LANES), lambda i, d: (i + off, 0)),
            pl.BlockSpec((1, D_MODEL), lambda i, d: (0, 0)),
            pl.BlockSpec(memory_space=pl.ANY),
        ],
        out_specs=pl.BlockSpec((tm, D_MODEL), lambda i, d: (i, 0)),
        scratch_shapes=[pltpu.VMEM((2, 2, tm, D_MODEL), F32), pltpu.SemaphoreType.DMA((2,))],
    )
    return pl.pallas_call(
        functools.partial(_final_kernel, tok_offset=tok_offset),
        out_shape=jax.ShapeDtypeStruct((n_tok, D_MODEL), F32),
        grid_spec=grid_spec,
        compiler_params=_cparams(("arbitrary",)),
        name="final",
    )(dest, x1_all, info, gain, out_rows)


def _rot_table(pos, n_rows):
    half = RET_KEY_DIM // 2
    theta = 10000.0 ** (-jnp.linspace(0.0, 1.0, half, dtype=F32))
    ang = pos.astype(F32)[:, None] * theta[None, :]
    cos, sin = jnp.cos(ang), jnp.sin(ang)
    cos128 = jnp.tile(cos, (1, 4))
    sin128 = jnp.tile(jnp.concatenate([-sin, sin], axis=1), (1, 2))
    k_scale = RET_KEY_DIM ** -0.5
    tab = jnp.concatenate([cos128, sin128, cos128 * k_scale, sin128 * k_scale], axis=1)
    return jnp.tile(tab, (n_rows // tab.shape[0], 1))


def kernel(x_prompt, x_sample, cache_diff_k, cache_diff_v, state_retention, norm_mix_gain, w_in, lambda_q1,
           lambda_k1, lambda_q2, lambda_k2, diff_subln_gain, w_ret_out, w_diff_out, w_out, rel_bias_table,
           norm_ffn_gain, w_group_router, b_group_router, w_expert_router, b_expert_router, w_expert_gate,
           w_expert_up, w_expert_down, norm_final_gain):
    B, S, D = x_prompt.shape
    BS, L, _ = x_sample.shape
    past = cache_diff_k.shape[2]
    TP, TS = B * S, BS * L
    T = TP + TS

    w_in_bf = w_in[0].astype(BF16)
    wr_bf = w_ret_out[0].astype(BF16)
    wd_bf = w_diff_out[0].astype(BF16)
    wo_bf = w_out[0].astype(BF16)
    gain_mix = norm_mix_gain[0][None, :]
    gain_ffn = norm_ffn_gain[0][None, :]
    gain_fin = norm_final_gain[None, :]
    gain_sub = diff_subln_gain[0][None, :]
    lam_rows = jnp.zeros((8, LANES), F32).at[0:4, 0:DIFF_HEAD_DIM].set(
        jnp.stack([lambda_q1[0], lambda_k1[0], lambda_q2[0], lambda_k2[0]]))
    rot_p = _rot_table(jnp.arange(S), S)
    rot_s = _rot_table(past + jnp.arange(L), PROJ_TM)

    far = _far_bias(rel_bias_table)[:, None, None]
    qp = ATT_TQ + jnp.arange(ATT_TQ)
    def tile_bias(k_pos):
        bias = (_relative_bias(qp, k_pos, rel_bias_table) - far) * LOG2E
        visible = (k_pos[None, :] // CHUNK) <= (qp[:, None] // CHUNK)
        bias = jnp.swapaxes(jnp.where(visible[None], bias, NEG_INF), 1, 2)
        return jnp.concatenate([bias, bias], axis=2)
    bias_prev = tile_bias(jnp.arange(ATT_TK))
    bias_diag = tile_bias(ATT_TQ + jnp.arange(ATT_TK))
    q_pos_s = past + jnp.arange(L)
    b_last = (_relative_bias(q_pos_s, past - SAMPLE_TK + jnp.arange(SAMPLE_TK), rel_bias_table) - far) * LOG2E
    b_last = jnp.concatenate([b_last, b_last], axis=1)
    bias_past = jnp.stack([jnp.zeros_like(b_last), b_last])
    b_new = (_relative_bias(q_pos_s, q_pos_s, rel_bias_table) - far) * LOG2E
    bias_new = jnp.concatenate([b_new, b_new], axis=1)

    xp2 = x_prompt.reshape(TP, D)
    xs2 = x_sample.reshape(TS, D)
    z3p, gp, gatesp, v32p, qt, vt5, kt32 = _proj(xp2, gain_mix, w_in_bf, rot_p, seq=S, transposed_k=True)
    z3s, gs, gatess, v32s, k32s = _proj(xs2, gain_mix, w_in_bf, rot_s, seq=L, transposed_k=False)

    zero_state = jnp.zeros((B, N_HEADS, RET_KEY_DIM, HEAD_V), F32)
    orp, ret_p = _retention(z3p, zero_state, batch=B, seq=S, chunk=256)
    ors, ret_s = _retention(z3s, state_retention[0], batch=BS, seq=L, chunk=L)

    gain_col = jnp.broadcast_to(diff_subln_gain[0][:, None], (HEAD_V, ATT_TQ))
    odp = _attn_prompt(z3p, qt, vt5, bias_prev, bias_diag, lam_rows, gain_col, batch=B, seq=S)
    kc = jnp.transpose(cache_diff_k[0], (0, 2, 3, 4, 1)).reshape(BS, N_HEADS, LANES, past)
    vc = cache_diff_v[0].reshape(BS, past * N_HEADS, LANES)
    ods = _attn_sample(z3s, kc, vc, bias_past, bias_new, lam_rows, gain_sub, batch=BS, n_q=L, past=past)

    mgp = _post_a(orp, gp, odp, gatesp, wr_bf, wd_bf)
    mgs = _post_a(ors, gs, ods, gatess, wr_bf, wd_bf)

    w_rt = jnp.zeros((D, LANES), F32)
    w_rt = w_rt.at[:, 0:N_GROUPS].set(w_group_router[0])
    w_rt = w_rt.at[:, N_GROUPS:N_GROUPS + N_EXPERTS].set(
        jnp.transpose(w_expert_router[0], (1, 0, 2)).reshape(D, N_EXPERTS))
    w_rt_hi = w_rt.astype(BF16)
    w_rt_lo = (w_rt - w_rt_hi.astype(F32)).astype(BF16)
    w_router2 = jnp.concatenate([w_rt_hi, w_rt_lo], axis=1)
    bias_row = jnp.zeros((1, LANES), F32)
    bias_row = bias_row.at[0, 0:N_GROUPS].set(b_group_router[0])
    bias_row = bias_row.at[0, N_GROUPS:N_GROUPS + N_EXPERTS].set(b_expert_router[0].reshape(-1))

    x1_all, lg_all = _post_b(xp2, mgp, xs2, mgs, wo_bf, gain_ffn, w_router2)

    info, counts = _route(lg_all, bias_row)
    counts = counts[0, :N_EXPERTS].astype(jnp.int32)
    padded = (counts + EXPERT_BLOCK - 1) // EXPERT_BLOCK * EXPERT_BLOCK
    pad_end = jnp.cumsum(padded)
    offs = pad_end - padded
    e12 = info[:, 0:2].astype(jnp.int32)
    rank12 = info[:, 4:6].astype(jnp.int32)
    expert_ids = jnp.arange(N_EXPERTS, dtype=jnp.int32)
    offs_e = jnp.sum(jnp.where(e12[:, :, None] == expert_ids, offs, 0), axis=-1)
    dest = (offs_e + rank12).reshape(-1)
    n_assign = 2 * T
    n_rows = -(-n_assign // EXPERT_BLOCK) * EXPERT_BLOCK + N_EXPERTS * EXPERT_BLOCK
    n_blocks = n_rows // EXPERT_BLOCK
    token_id = (jnp.arange(n_assign, dtype=jnp.int32) // 2)
    row_token = jnp.zeros((n_rows,), jnp.int32).at[dest].set(token_id)
    block_start = jnp.arange(n_blocks, dtype=jnp.int32) * EXPERT_BLOCK
    block_expert = jnp.minimum(jnp.sum((pad_end[None, :] <= block_start[:, None]).astype(jnp.int32), axis=1),
                               N_EXPERTS - 1)
    n_used = (pad_end[-1:] // EXPERT_BLOCK).astype(jnp.int32)
    later_used = (expert_ids[None, :] > expert_ids[:, None]) & (counts[None, :] > 0)
    next_of_expert = jnp.min(jnp.where(later_used, expert_ids[None, :], N_EXPERTS), axis=1)
    next_of_expert = jnp.where(next_of_expert == N_EXPERTS, -1, next_of_expert)
    next_expert = jnp.sum(jnp.where(block_expert[:, None] == expert_ids[None, :], next_of_expert[None, :], 0),
                          axis=1).astype(jnp.int32)

    out_rows = _moe(row_token, block_expert, next_expert, n_used, x1_all, gain_ffn,
                    w_expert_gate[0], w_expert_up[0], w_expert_down[0])

    y_p = _final(dest, x1_all, info, gain_fin, out_rows, tok_offset=0, n_tok=TP)
    y_s = _final(dest, x1_all, info, gain_fin, out_rows, tok_offset=TP, n_tok=TS)

    new_k_p = jnp.transpose(kt32, (0, 1, 5, 2, 3, 4))
    new_v_p = v32p.reshape(1, B, S, N_HEADS, HEAD_V)
    new_k_s = k32s.reshape(1, BS, L, N_HEADS, 2, DIFF_HEAD_DIM)
    new_v_s = v32s.reshape(1, BS, L, N_HEADS, HEAD_V)
    return (y_p.reshape(B, S, D), y_s.reshape(BS, L, D), new_k_p, new_v_p, ret_p[None],
            new_k_s, new_v_s, ret_s[None])
```

```python
import functools
import math

import jax
import jax.numpy as jnp
from jax import lax
from jax.experimental import pallas as pl
from jax.experimental.pallas import tpu as pltpu

F32 = jnp.float32
BF16 = jnp.bfloat16

D_MODEL = 2048
CHUNK = 64
N_HEADS = 8
RET_KEY_DIM = 64
HEAD_V = 128
DIFF_HEAD_DIM = 64
IN_WIDTH = 10240
NUM_BUCKETS = 32
MAX_DISTANCE = 128
N_GROUPS = 8
EXPERTS_PER_GROUP = 8
N_EXPERTS = 64
EXPERT_HIDDEN = 512
EXPERT_BLOCK = 128
NORM_EPS = 1e-6
NEG_INF = -1e30
LANES = 128
LOG2E = 1.4426950408889634
VT_ROWS = HEAD_V + 16

VMEM_LIMIT = 56 * 1024 * 1024


def _cparams(sem):
    return pltpu.CompilerParams(dimension_semantics=sem, vmem_limit_bytes=VMEM_LIMIT)


PROJ_TM = 512
PROJ_TN = 1024
PROJ_MC = 128
ATT_TK = 256


def _rotate_pairs(acc, cos, sin):
    outs = []
    lane = lax.broadcasted_iota(jnp.int32, (acc.shape[0], LANES), 1)
    first_half = (lane % 64) < 32
    for c in range(acc.shape[1] // LANES):
        xs = acc[:, c * LANES:(c + 1) * LANES]
        swapped = jnp.where(first_half, pltpu.roll(xs, 96, axis=1), pltpu.roll(xs, 32, axis=1))
        outs.append(xs * cos + swapped * sin)
    return outs


def _proj_kernel(x_ref, gain_ref, w_ref, rot_ref, *out_refs, transposed_k):
    if transposed_k:
        z_ref, g_ref, gates_ref, v32_ref, qt_ref, vt_ref, kt32_ref, h_scr = out_refs
    else:
        z_ref, g_ref, gates_ref, v32_ref, k32_ref, h_scr = out_refs
    j = pl.program_id(1)

    @pl.when(j == 0)
    def _():
        x = x_ref[...]
        ms = jnp.mean(x * x, axis=-1, keepdims=True)
        h_scr[...] = (x * lax.rsqrt(ms + NORM_EPS) * gain_ref[...]).astype(BF16)

    n_slab = PROJ_TN // LANES

    def for_row_chunks(epilogue):
        for c in range(PROJ_TM // PROJ_MC):
            rows = slice(c * PROJ_MC, (c + 1) * PROJ_MC)
            epilogue(c, rows, jnp.dot(h_scr[rows, :], w_ref[...], preferred_element_type=F32))

    def store_slabs(rows, vals):
        for s in range(n_slab):
            z_ref[s, rows, :] = vals[s].astype(BF16)

    def split(a):
        return [a[:, s * LANES:(s + 1) * LANES] for s in range(n_slab)]

    @pl.when(j == 0)
    def _():
        def epilogue(c, rows, acc):
            half = PROJ_TN // 2
            q = _rotate_pairs(acc[:, :half], rot_ref[rows, 0:128], rot_ref[rows, 128:256])
            k = _rotate_pairs(acc[:, half:], rot_ref[rows, 256:384], rot_ref[rows, 384:512])
            store_slabs(rows, q + k)
        for_row_chunks(epilogue)

    @pl.when(j == 1)
    def _():
        for_row_chunks(lambda c, rows, acc: store_slabs(rows, split(acc)))

    @pl.when(j == 2)
    def _():
        def epilogue(c, rows, acc):
            g_ref[rows, :] = (acc * jax.nn.sigmoid(acc)).astype(BF16)
        for_row_chunks(epilogue)

    @pl.when(j == 3)
    def _():
        def epilogue(c, rows, acc):
            qs = acc * (DIFF_HEAD_DIM ** -0.5 * LOG2E)
            store_slabs(rows, split(qs))
            if transposed_k:
                qs_t = qs.T.astype(BF16)
                for hh in range(n_slab):
                    qt_ref[0, hh, :, rows] = qs_t[hh * LANES:(hh + 1) * LANES, :]
        for_row_chunks(epilogue)

    @pl.when(j == 4)
    def _():
        def epilogue(c, rows, acc):
            store_slabs(rows, split(acc))
            if transposed_k:
                kt32_ref[0, 0, :, :, :, rows] = acc.T.reshape(n_slab, 2, DIFF_HEAD_DIM, PROJ_MC)
            else:
                k32_ref[rows, :] = acc
        for_row_chunks(epilogue)

    @pl.when(j == 5)
    def _():
        def epilogue(c, rows, acc):
            store_slabs(rows, split(acc))
            v32_ref[rows, :] = acc
            if transposed_k:
                acc_tb = acc.T.astype(BF16)
                t, cols = (c * PROJ_MC) // ATT_TK, slice((c * PROJ_MC) % ATT_TK, (c * PROJ_MC) % ATT_TK + PROJ_MC)
                ones = jnp.ones((VT_ROWS - HEAD_V, PROJ_MC), BF16)
                for hh in range(n_slab):
                    vt_ref[0, hh, t, 0:HEAD_V, cols] = acc_tb[hh * LANES:(hh + 1) * LANES, :]
                    vt_ref[0, hh, t, HEAD_V:VT_ROWS, cols] = ones
        for_row_chunks(epilogue)

    @pl.when(j >= 6)
    def _():
        def epilogue(c, rows, acc):
            gates_ref[rows, :] = jax.nn.sigmoid(acc).astype(BF16)
        for_row_chunks(epilogue)


def _proj(x2d, gain, w_bf, rot, *, seq, transposed_k):
    T = x2d.shape[0]
    tm, tn = PROJ_TM, PROJ_TN
    ni, nj = T // tm, IN_WIDTH // tn
    rot_blocks = rot.shape[0] // tm

    def zmap(i, j):
        zj = jnp.where(j < 2, j, jnp.where(j < 3, 1, jnp.where(j < 6, j - 1, 4)))
        return (zj, i, 0)

    out_shape = [
        jax.ShapeDtypeStruct((40, T, LANES), BF16),
        jax.ShapeDtypeStruct((T, 1024), BF16),
        jax.ShapeDtypeStruct((T, 4096), BF16),
        jax.ShapeDtypeStruct((T, 1024), F32),
    ]
    out_specs = [
        pl.BlockSpec((8, tm, LANES), zmap),
        pl.BlockSpec((tm, tn), lambda i, j: (i, 0)),
        pl.BlockSpec((tm, tn), lambda i, j: (i, jnp.clip(j - 6, 0, 3))),
        pl.BlockSpec((tm, tn), lambda i, j: (i, 0)),
    ]
    if transposed_k:
        B = T // seq
        spb = seq // tm
        out_shape += [
            jax.ShapeDtypeStruct((B, N_HEADS, LANES, seq), BF16),
            jax.ShapeDtypeStruct((B, N_HEADS, seq // ATT_TK, VT_ROWS, ATT_TK), BF16),
            jax.ShapeDtypeStruct((1, B, N_HEADS, 2, DIFF_HEAD_DIM, seq), F32),
        ]
        out_specs += [
            pl.BlockSpec((1, N_HEADS, LANES, tm), lambda i, j: (i // spb, 0, 0, i % spb)),
            pl.BlockSpec((1, N_HEADS, tm // ATT_TK, VT_ROWS, ATT_TK), lambda i, j: (i // spb, 0, i % spb, 0, 0)),
            pl.BlockSpec((1, 1, N_HEADS, 2, DIFF_HEAD_DIM, tm), lambda i, j: (0, i // spb, 0, 0, 0, i % spb)),
        ]
    else:
        out_shape += [jax.ShapeDtypeStruct((T, 1024), F32)]
        out_specs += [pl.BlockSpec((tm, tn), lambda i, j: (i, 0))]

    return pl.pallas_call(
        functools.partial(_proj_kernel, transposed_k=transposed_k),
        out_shape=out_shape,
        grid=(ni, nj),
        in_specs=[
            pl.BlockSpec((tm, D_MODEL), lambda i, j: (i, 0)),
            pl.BlockSpec((1, D_MODEL), lambda i, j: (0, 0)),
            pl.BlockSpec((D_MODEL, tn), lambda i, j: (0, j)),
            pl.BlockSpec((tm, 512), lambda i, j: (i % rot_blocks, 0)),
        ],
        out_specs=out_specs,
        scratch_shapes=[pltpu.VMEM((tm, D_MODEL), BF16)],
        compiler_params=_cparams(("arbitrary", "arbitrary")),
        name="proj_t" if transposed_k else "proj",
    )(x2d, gain, w_bf, rot)


def _retention_kernel(q_ref, k_ref, v_ref, decay_ref, inner_ref, outer_ref, st0_ref,
                      o_ref, st_out_ref, st_scr, *, chunk):
    c = pl.program_id(1)
    nc = pl.num_programs(1)

    @pl.when(c == 0)
    def _():
        st_scr[...] = jnp.zeros_like(st_scr)
        for h in range(N_HEADS):
            a = h % 2
            st_scr[h, a * RET_KEY_DIM:(a + 1) * RET_KEY_DIM, :] = st0_ref[0, h]

    lane_lo = lax.broadcasted_iota(jnp.int32, (chunk, LANES), 1) < RET_KEY_DIM
    row_lo = lax.broadcasted_iota(jnp.int32, (LANES, LANES), 0) < RET_KEY_DIM
    for h in range(N_HEADS):
        p, a = h // 2, h % 2
        q = q_ref[p]
        k = k_ref[p]
        v = v_ref[h]
        qa = jnp.where(lane_lo == (a == 0), q, jnp.zeros_like(q))
        s = lax.dot_general(qa, k, (((1,), (1,)), ((), ())), preferred_element_type=F32)
        s = s * decay_ref[h]
        st = st_scr[h]
        inner = inner_ref[h]
        o = (jnp.dot(s.astype(BF16), v, preferred_element_type=F32)
             + jnp.dot(qa, st.astype(BF16), preferred_element_type=F32) * inner)
        ko = (k.astype(F32) * outer_ref[p]).astype(BF16)
        upd = lax.dot_general(ko, v, (((0,), (0,)), ((), ())), preferred_element_type=F32)
        upd = jnp.where(row_lo == (a == 0), upd, 0.0)
        chunk_decay = inner[chunk - 1:chunk, :]
        st_scr[h] = chunk_decay * st + upd
        o_n = o * lax.rsqrt(jnp.mean(o * o, axis=-1, keepdims=True) + NORM_EPS)
        o_ref[:, h * HEAD_V:(h + 1) * HEAD_V] = o_n.astype(BF16)

    @pl.when(c == nc - 1)
    def _():
        for h in range(N_HEADS):
            a = h % 2
            st_out_ref[0, h] = st_scr[h, a * RET_KEY_DIM:(a + 1) * RET_KEY_DIM, :]


def _retention_consts(chunk):
    log_g = jnp.log(1.0 - 2.0 ** (-5.0 - jnp.arange(N_HEADS, dtype=F32)))
    j = jnp.arange(chunk, dtype=F32)
    diff = j[:, None] - j[None, :]
    decay = jnp.where(diff >= 0, jnp.exp(log_g[:, None, None] * jnp.maximum(diff, 0.0)), 0.0)
    inner = jnp.exp(log_g[:, None] * (j + 1.0))
    outer = jnp.exp(log_g[:, None] * (chunk - 1.0 - j))
    inner_b = jnp.broadcast_to(inner[:, :, None], (N_HEADS, chunk, LANES))
    outer_pair = jnp.repeat(outer.reshape(N_HEADS // 2, 2, chunk).transpose(0, 2, 1), RET_KEY_DIM, axis=2)
    return decay, inner_b, outer_pair


def _retention(z3, state0, *, batch, seq, chunk):
    T = batch * seq
    nc = seq // chunk
    decay, inner_b, outer_pair = _retention_consts(chunk)
    return pl.pallas_call(
        functools.partial(_retention_kernel, chunk=chunk),
        out_shape=[jax.ShapeDtypeStruct((T, 1024), BF16),
                   jax.ShapeDtypeStruct((batch, N_HEADS, RET_KEY_DIM, HEAD_V), F32)],
        grid=(batch, nc),
        in_specs=[
            pl.BlockSpec((4, chunk, LANES), lambda b, c: (0, b * nc + c, 0)),
            pl.BlockSpec((4, chunk, LANES), lambda b, c: (1, b * nc + c, 0)),
            pl.BlockSpec((8, chunk, LANES), lambda b, c: (1, b * nc + c, 0)),
            pl.BlockSpec((N_HEADS, chunk, chunk), lambda b, c: (0, 0, 0)),
            pl.BlockSpec((N_HEADS, chunk, LANES), lambda b, c: (0, 0, 0)),
            pl.BlockSpec((N_HEADS // 2, chunk, LANES), lambda b, c: (0, 0, 0)),
            pl.BlockSpec((1, N_HEADS, RET_KEY_DIM, HEAD_V), lambda b, c: (b, 0, 0, 0)),
        ],
        out_specs=[
            pl.BlockSpec((chunk, 1024), lambda b, c: (b * nc + c, 0)),
            pl.BlockSpec((1, N_HEADS, RET_KEY_DIM, HEAD_V), lambda b, c: (b, 0, 0, 0)),
        ],
        scratch_shapes=[pltpu.VMEM((N_HEADS, LANES, LANES), F32)],
        compiler_params=_cparams(("arbitrary", "arbitrary")),
        name=f"retention_c{chunk}",
    )(z3, z3, z3, decay, inner_b, outer_pair, state0)


def _lam_init(layer=0):
    return 0.8 - 0.6 * math.exp(-0.3 * layer)


def _lam_from_ref(lam_ref):
    lp = lam_ref[...]
    s1 = jnp.sum(lp[0:1] * lp[1:2], axis=-1, keepdims=True)
    s2 = jnp.sum(lp[2:3] * lp[3:4], axis=-1, keepdims=True)
    return jnp.exp(s1) - jnp.exp(s2) + _lam_init()


def _stack_q(q):
    lane_lo = lax.broadcasted_iota(jnp.int32, q.shape, 1) < DIFF_HEAD_DIM
    zero = jnp.zeros_like(q)
    return jnp.concatenate([jnp.where(lane_lo, q, zero), jnp.where(lane_lo, zero, q)], axis=0)


def _with_ones(v):
    return jnp.concatenate([v, jnp.ones_like(v)], axis=1)


def _diff_finish(acc_ref, lam, gain, n):
    o = acc_ref[:, 0:HEAD_V] / acc_ref[:, HEAD_V:2 * HEAD_V]
    o = o[:n] - lam * o[n:]
    o = o * lax.rsqrt(jnp.mean(o * o, axis=-1, keepdims=True) + NORM_EPS) * gain
    return o * (1.0 - _lam_init())


def _relative_bias(q_pos, k_pos, table):
    rel = k_pos[None, :] - q_pos[:, None]
    half = NUM_BUCKETS // 2
    max_exact = half // 2
    n = jnp.abs(rel)
    log_ratio = jnp.log(jnp.maximum(n, 1).astype(F32) / max_exact) / math.log(MAX_DISTANCE / max_exact)
    large = jnp.minimum(max_exact + (log_ratio * (half - max_exact)).astype(jnp.int32), half - 1)
    bucket = (rel > 0).astype(jnp.int32) * half + jnp.where(n < max_exact, n, large)
    out = jnp.zeros((table.shape[1],) + bucket.shape, F32)
    for bkt in range(NUM_BUCKETS):
        out = jnp.where(bucket[None] == bkt, table[bkt].astype(F32)[:, None, None], out)
    return out


def _far_bias(table):
    return table[NUM_BUCKETS // 2 - 1].astype(F32)


ATT_TQ = 256
ATT_HP = 8


def _attn_prompt_kernel(qt_ref, k_ref, vt_ref, bprev_ref, bdiag_ref, lam_ref, gain_ref, o_ref, acc_scr):
    i = pl.program_id(2)
    row_lo = lax.broadcasted_iota(jnp.int32, (LANES, ATT_TQ), 0) < DIFF_HEAD_DIM
    qs = []
    for hh in range(ATT_HP):
        qt = qt_ref[0, hh]
        zero = jnp.zeros_like(qt)
        qs.append(jnp.concatenate([jnp.where(row_lo, qt, zero), jnp.where(row_lo, zero, qt)], axis=1))
    acc_scr[...] = jnp.zeros_like(acc_scr)

    def tile(j, ms, biases):
        ss = []
        for hh in range(ATT_HP):
            kt = k_ref[hh, pl.ds(pl.multiple_of(j * ATT_TK, ATT_TK), ATT_TK), :]
            s = jnp.dot(kt, qs[hh], preferred_element_type=F32)
            ss.append(s if biases is None else s + biases[hh])
        ps, alphas, m_out = [], [], []
        for hh in range(ATT_HP):
            m_new = jnp.maximum(ms[hh], jnp.max(ss[hh], axis=0, keepdims=True))
            alphas.append(jnp.exp2(ms[hh] - m_new))
            ps.append(jnp.exp2(ss[hh] - m_new).astype(BF16))
            m_out.append(m_new)
        for hh in range(ATT_HP):
            acc_scr[hh] = alphas[hh] * acc_scr[hh] + jnp.dot(vt_ref[0, hh, j], ps[hh],
                                                             preferred_element_type=F32)
        return tuple(m_out)

    ms = tuple(jnp.full((1, 2 * ATT_TQ), NEG_INF, F32) for _ in range(ATT_HP))
    ms = lax.fori_loop(0, i - 1, lambda j, c: tile(j, c, None), ms)
    jp = jnp.maximum(i - 1, 0)
    ms = lax.cond(i >= 1,
                  lambda c: tile(jp, c, [bprev_ref[hh] for hh in range(ATT_HP)]),
                  lambda c: c, ms)
    tile(i, ms, [bdiag_ref[hh] for hh in range(ATT_HP)])

    lam = _lam_from_ref(lam_ref)
    for hh in range(ATT_HP):
        o = acc_scr[hh, 0:HEAD_V, :] / acc_scr[hh, HEAD_V:HEAD_V + 1, :]
        o = o[:, :ATT_TQ] - lam * o[:, ATT_TQ:]
        o = o * lax.rsqrt(jnp.mean(o * o, axis=0, keepdims=True) + NORM_EPS) * gain_ref[...]
        o_ref[:, hh * HEAD_V:(hh + 1) * HEAD_V] = (o * (1.0 - _lam_init())).T.astype(BF16)


def _attn_prompt(z3, qt, vt5, bias_prev, bias_diag, lam_rows, gain_col, *, batch, seq):
    T = batch * seq
    nq = seq // ATT_TQ
    hp = ATT_HP
    return pl.pallas_call(
        _attn_prompt_kernel,
        out_shape=jax.ShapeDtypeStruct((T, 1024), BF16),
        grid=(batch, N_HEADS // hp, nq),
        in_specs=[
            pl.BlockSpec((1, hp, LANES, ATT_TQ), lambda b, g, i: (b, g, 0, i)),
            pl.BlockSpec((hp, seq, LANES), lambda b, g, i: (24 // hp + g, b, 0), pipeline_mode=pl.Buffered(1)),
            pl.BlockSpec((1, hp, seq // ATT_TK, VT_ROWS, ATT_TK), lambda b, g, i: (b, g, 0, 0, 0),
                         pipeline_mode=pl.Buffered(1)),
            pl.BlockSpec((hp, ATT_TK, 2 * ATT_TQ), lambda b, g, i: (g, 0, 0), pipeline_mode=pl.Buffered(1)),
            pl.BlockSpec((hp, ATT_TK, 2 * ATT_TQ), lambda b, g, i: (g, 0, 0), pipeline_mode=pl.Buffered(1)),
            pl.BlockSpec((8, LANES), lambda b, g, i: (0, 0)),
            pl.BlockSpec((LANES, ATT_TQ), lambda b, g, i: (0, 0)),
        ],
        out_specs=pl.BlockSpec((ATT_TQ, hp * HEAD_V), lambda b, g, i: (b * nq + i, g)),
        scratch_shapes=[pltpu.VMEM((hp, VT_ROWS, 2 * ATT_TQ), F32)],
        compiler_params=_cparams(("arbitrary", "arbitrary", "arbitrary")),
        name="attn_prompt",
    )(qt, z3, vt5, bias_prev, bias_diag, lam_rows, gain_col)


SAMPLE_TK = 512


def _attn_sample_kernel(q_ref, kc_ref, vc_ref, kn_ref, vn_ref, bpast_ref, bnew_ref, lam_ref, gain_ref,
                        o_ref, m_scr, acc_scr, *, n_past_tiles, n_q):
    t = pl.program_id(1)

    @pl.when(t == 0)
    def _():
        m_scr[...] = jnp.full_like(m_scr, NEG_INF)
        acc_scr[...] = jnp.zeros_like(acc_scr)

    def update(logits, values):
        ps, alphas = [], []
        for h in range(N_HEADS):
            m_prev = m_scr[h]
            m_new = jnp.maximum(m_prev, jnp.max(logits[h], axis=1, keepdims=True))
            alpha = jnp.exp2(m_prev - m_new)
            ps.append(jnp.exp2(logits[h] - m_new[:, 0:1]).astype(BF16))
            alphas.append(jnp.concatenate([alpha, alpha], axis=1))
            m_scr[h] = m_new
        for h in range(N_HEADS):
            acc_scr[h] = alphas[h] * acc_scr[h] + jnp.dot(ps[h], _with_ones(values[h]()),
                                                          preferred_element_type=F32)

    @pl.when(t < n_past_tiles)
    def _():
        logits = []
        for h in range(N_HEADS):
            kt = kc_ref[0, h].astype(BF16)
            logits.append(jnp.dot(_stack_q(q_ref[h]), kt, preferred_element_type=F32) + bpast_ref[0, h])
        update(logits, [lambda h=h: vc_ref[0, pl.ds(h, SAMPLE_TK, stride=N_HEADS), :].astype(BF16)
                        for h in range(N_HEADS)])

    @pl.when(t == n_past_tiles)
    def _():
        logits = []
        for h in range(N_HEADS):
            s = lax.dot_general(_stack_q(q_ref[h]), kn_ref[h], (((1,), (1,)), ((), ())),
                                preferred_element_type=F32)
            logits.append(s + bnew_ref[h])
        update(logits, [lambda h=h: vn_ref[h] for h in range(N_HEADS)])
        lam = _lam_from_ref(lam_ref)
        for h in range(N_HEADS):
            o = _diff_finish(acc_scr.at[h], lam, gain_ref[...], n_q)
            o_ref[:, h * HEAD_V:(h + 1) * HEAD_V] = o.astype(BF16)


def _attn_sample(z3, kc, vc, bias_past, bias_new, lam_rows, gain, *, batch, n_q, past):
    npt = past // SAMPLE_TK
    return pl.pallas_call(
        functools.partial(_attn_sample_kernel, n_past_tiles=npt, n_q=n_q),
        out_shape=jax.ShapeDtypeStruct((batch * n_q, 1024), BF16),
        grid=(batch, npt + 1),
        in_specs=[
            pl.BlockSpec((8, n_q, LANES), lambda b, t: (2, b, 0)),
            pl.BlockSpec((1, N_HEADS, LANES, SAMPLE_TK), lambda b, t: (b, 0, 0, jnp.minimum(t, npt - 1))),
            pl.BlockSpec((1, SAMPLE_TK * N_HEADS, LANES), lambda b, t: (b, jnp.minimum(t, npt - 1), 0)),
            pl.BlockSpec((8, n_q, LANES), lambda b, t: (3, b, 0)),
            pl.BlockSpec((8, n_q, LANES), lambda b, t: (4, b, 0)),
            pl.BlockSpec((1, N_HEADS, 2 * n_q, SAMPLE_TK), lambda b, t: (jnp.where(t == npt - 1, 1, 0), 0, 0, 0)),
            pl.BlockSpec((N_HEADS, 2 * n_q, n_q), lambda b, t: (0, 0, 0)),
            pl.BlockSpec((8, LANES), lambda b, t: (0, 0)),
            pl.BlockSpec((1, LANES), lambda b, t: (0, 0)),
        ],
        out_specs=pl.BlockSpec((n_q, 1024), lambda b, t: (b, 0)),
        scratch_shapes=[pltpu.VMEM((N_HEADS, 2 * n_q, LANES), F32),
                        pltpu.VMEM((N_HEADS, 2 * n_q, 2 * LANES), F32)],
        compiler_params=_cparams(("arbitrary", "arbitrary")),
        name="attn_sample",
    )(z3, kc, vc, z3, z3, bias_past, bias_new, lam_rows, gain)


POST_TM = 256


def _post_a_kernel(or_ref, g_ref, od_ref, gr_ref, gd_ref, wr_ref, wd_ref, out_ref):
    a = jnp.dot(or_ref[...] * g_ref[...], wr_ref[...], preferred_element_type=F32)
    b = jnp.dot(od_ref[...], wd_ref[...], preferred_element_type=F32)
    out_ref[...] = (gr_ref[...].astype(F32) * a + gd_ref[...].astype(F32) * b).astype(BF16)


def _post_a(o_r, g, o_d, gates, wr_bf, wd_bf):
    T = o_r.shape[0]
    tm = POST_TM
    return pl.pallas_call(
        _post_a_kernel,
        out_shape=jax.ShapeDtypeStruct((T, D_MODEL), BF16),
        grid=(T // tm,),
        in_specs=[
            pl.BlockSpec((tm, 1024), lambda i: (i, 0)),
            pl.BlockSpec((tm, 1024), lambda i: (i, 0)),
            pl.BlockSpec((tm, 1024), lambda i: (i, 0)),
            pl.BlockSpec((tm, D_MODEL), lambda i: (i, 0)),
            pl.BlockSpec((tm, D_MODEL), lambda i: (i, 1)),
            pl.BlockSpec((1024, D_MODEL), lambda i: (0, 0)),
            pl.BlockSpec((1024, D_MODEL), lambda i: (0, 0)),
        ],
        out_specs=pl.BlockSpec((tm, D_MODEL), lambda i: (i, 0)),
        compiler_params=_cparams(("arbitrary",)),
        name="post_a",
    )(o_r, g, o_d, gates, gates, wr_bf, wd_bf)


def _pack_bf16_pairs(x):
    w = x.shape[1] // 2
    xb = x.astype(BF16).astype(F32)
    lo = pltpu.bitcast(xb[:, :w], jnp.uint32) >> 16
    hi = pltpu.bitcast(xb[:, w:], jnp.uint32) & jnp.uint32(0xFFFF0000)
    return lo | hi


def _unpack_bf16_pairs(words):
    lo = pltpu.bitcast(words << 16, F32)
    hi = pltpu.bitcast(words & jnp.uint32(0xFFFF0000), F32)
    return jnp.concatenate([lo, hi], axis=1)


def _post_b_kernel(xp_ref, mp_ref, xs_ref, ms_ref, wo_ref, gain_ref, wrt_ref, x1_ref, lg_ref, hp_ref, *,
                   n_prompt_blocks):
    i = pl.program_id(0)
    tm = POST_TM

    def body(x_ref, mg_ref):
        x1 = x_ref[...] + jnp.dot(mg_ref[...], wo_ref[...], preferred_element_type=F32)
        x1_ref[...] = x1
        h2 = x1 * lax.rsqrt(jnp.mean(x1 * x1, axis=-1, keepdims=True) + NORM_EPS) * gain_ref[...]
        hp_ref[...] = _pack_bf16_pairs(h2)
        h_hi = h2.astype(BF16)
        h_lo = (h2 - h_hi.astype(F32)).astype(BF16)
        r = jnp.dot(jnp.concatenate([h_hi, h_lo], axis=0), wrt_ref[...], preferred_element_type=F32)
        lg_ref[...] = (r[:tm, :LANES] + r[:tm, LANES:]) + (r[tm:, :LANES] + r[tm:, LANES:])

    @pl.when(i < n_prompt_blocks)
    def _():
        body(xp_ref, mp_ref)

    @pl.when(i >= n_prompt_blocks)
    def _():
        body(xs_ref, ms_ref)


def _post_b(xp2, mgp, xs2, mgs, wo_bf, gain, w_router2):
    tm = POST_TM
    npb, nsb = xp2.shape[0] // tm, xs2.shape[0] // tm
    T = xp2.shape[0] + xs2.shape[0]
    pmap = lambda i: (jnp.minimum(i, npb - 1), 0)
    smap = lambda i: (jnp.maximum(i - npb, 0), 0)
    return pl.pallas_call(
        functools.partial(_post_b_kernel, n_prompt_blocks=npb),
        out_shape=[jax.ShapeDtypeStruct((T, D_MODEL), F32),
                   jax.ShapeDtypeStruct((T, LANES), F32),
                   jax.ShapeDtypeStruct((T, D_MODEL // 2), jnp.uint32)],
        grid=(npb + nsb,),
        in_specs=[
            pl.BlockSpec((tm, D_MODEL), pmap),
            pl.BlockSpec((tm, D_MODEL), pmap),
            pl.BlockSpec((tm, D_MODEL), smap),
            pl.BlockSpec((tm, D_MODEL), smap),
            pl.BlockSpec((D_MODEL, D_MODEL), lambda i: (0, 0)),
            pl.BlockSpec((1, D_MODEL), lambda i: (0, 0)),
            pl.BlockSpec((D_MODEL, 2 * LANES), lambda i: (0, 0)),
        ],
        out_specs=[pl.BlockSpec((tm, D_MODEL), lambda i: (i, 0)),
                   pl.BlockSpec((tm, LANES), lambda i: (i, 0)),
                   pl.BlockSpec((tm, D_MODEL // 2), lambda i: (i, 0))],
        compiler_params=_cparams(("arbitrary",)),
        name="post_b",
    )(xp2, mgp, xs2, mgs, wo_bf, gain, w_router2)


ROUTE_TM = 512


def _route_kernel(lg_ref, bias_ref, info_ref, cnt_ref, tri_scr, carry_scr):
    i = pl.program_id(0)
    tm = ROUTE_TM

    @pl.when(i == 0)
    def _():
        r = lax.broadcasted_iota(jnp.int32, (tm, tm), 0)
        c = lax.broadcasted_iota(jnp.int32, (tm, tm), 1)
        tri_scr[...] = jnp.where(c < r, 1.0, 0.0).astype(BF16)
        carry_scr[...] = jnp.zeros_like(carry_scr)

    lg = lg_ref[...] + bias_ref[...]
    lane = lax.broadcasted_iota(jnp.int32, (tm, LANES), 1)
    lane_f = lane.astype(F32)
    neg = jnp.float32(-jnp.inf)

    def first_argmax(vals):
        top = jnp.max(vals, axis=1, keepdims=True)
        idx = jnp.min(jnp.where(vals == top, lane_f, float(LANES)), axis=1, keepdims=True)
        return top, idx

    is_group = lane < N_GROUPS
    gl = jnp.where(is_group, lg, neg)
    g_top, g_idx = first_argmax(gl)
    g_weight = 1.0 / jnp.sum(jnp.exp(gl - g_top), axis=1, keepdims=True)
    lane_group = ((lane - N_GROUPS) >> 3).astype(F32)
    in_group = (lane >= N_GROUPS) & (lane < N_GROUPS + N_EXPERTS) & (lane_group == g_idx)
    el = jnp.where(in_group, lg, neg)
    t1, i1 = first_argmax(el)
    el2 = jnp.where(lane_f == i1, neg, el)
    t2, i2 = first_argmax(el2)
    e2w = jnp.exp(t2 - t1)
    p1 = 1.0 / (1.0 + e2w)
    gate1 = g_weight * p1
    gate2 = g_weight * (e2w * p1)
    e1 = i1 - float(N_GROUPS)
    e2 = i2 - float(N_GROUPS)

    hot1 = lane_f == e1
    hot2 = lane_f == e2
    both = jnp.where(hot1 | hot2, 1.0, 0.0)
    prefix = jnp.dot(tri_scr[...], both.astype(BF16), preferred_element_type=F32) + carry_scr[...]
    rank1 = jnp.sum(jnp.where(hot1, prefix, 0.0), axis=1, keepdims=True)
    rank2 = jnp.sum(jnp.where(hot2, prefix, 0.0), axis=1, keepdims=True)
    carry_scr[...] = carry_scr[...] + jnp.sum(both, axis=0, keepdims=True)

    info = jnp.where(lane == 0, e1, jnp.where(lane == 1, e2, jnp.where(lane == 2, gate1, jnp.where(
        lane == 3, gate2, jnp.where(lane == 4, rank1, jnp.where(lane == 5, rank2, 0.0))))))
    info_ref[...] = info
    cnt_ref[...] = carry_scr[...]


def _route(lg_all, bias_row):
    T = lg_all.shape[0]
    tm = ROUTE_TM
    return pl.pallas_call(
        _route_kernel,
        out_shape=[jax.ShapeDtypeStruct((T, LANES), F32), jax.ShapeDtypeStruct((1, LANES), F32)],
        grid=(T // tm,),
        in_specs=[pl.BlockSpec((tm, LANES), lambda i: (i, 0)), pl.BlockSpec((1, LANES), lambda i: (0, 0))],
        out_specs=[pl.BlockSpec((tm, LANES), lambda i: (i, 0)), pl.BlockSpec((1, LANES), lambda i: (0, 0))],
        scratch_shapes=[pltpu.VMEM((tm, tm), BF16), pltpu.VMEM((1, LANES), F32)],
        compiler_params=_cparams(("arbitrary",)),
        name="route",
    )(lg_all, bias_row)


def _moe_kernel(tok_ref, bexp_ref, nxt_ref, nused_ref, hp_hbm, wg_hbm, wu_hbm, wd_hbm,
                out_ref, xbuf, gsem, wgbuf, wubuf, wdbuf, wsem, wg_bf, wu_bf, wd_bf, wslot, *, n_blocks):
    b = pl.program_id(0)
    n_used = nused_ref[0]
    slot = b % 2

    def gather_copy(row, s, r):
        return pltpu.make_async_copy(hp_hbm.at[pl.ds(row, 1)], xbuf.at[s, pl.ds(r, 1)], gsem.at[s])

    def weight_copies(e, s):
        return (pltpu.make_async_copy(wg_hbm.at[e], wgbuf.at[s], wsem.at[s]),
                pltpu.make_async_copy(wu_hbm.at[e], wubuf.at[s], wsem.at[s]),
                pltpu.make_async_copy(wd_hbm.at[e], wdbuf.at[s], wsem.at[s]))

    def issue(blk, s):
        base = blk * EXPERT_BLOCK

        def body(r, carry):
            gather_copy(tok_ref[base + r], s, r).start()
            return carry

        lax.fori_loop(0, EXPERT_BLOCK, body, 0, unroll=8)

    @pl.when(b == 0)
    def _():
        issue(0, 0)
        for cp in weight_copies(bexp_ref[0], 0):
            cp.start()
        wslot[0] = 0

    @pl.when(b < n_used)
    def _():
        e = bexp_ref[b]
        new_expert = (b == 0) | (e != bexp_ref[jnp.maximum(b - 1, 0)])

        @pl.when(new_expert)
        def _():
            @pl.when(b > 0)
            def _():
                wslot[0] = 1 - wslot[0]

            s = wslot[0]
            for cp in weight_copies(e, s):
                cp.wait()
            nxt = nxt_ref[b]

            @pl.when(nxt >= 0)
            def _():
                for cp in weight_copies(nxt, 1 - s):
                    cp.start()

            wg_bf[...] = wgbuf[s].astype(BF16)
            wu_bf[...] = wubuf[s].astype(BF16)
            wd_bf[...] = wdbuf[s].astype(BF16)

    def expert_block(cur):
        pltpu.make_async_copy(hp_hbm.at[pl.ds(0, EXPERT_BLOCK)], xbuf.at[cur], gsem.at[cur]).wait()
        base = jnp.minimum(b + 1, n_blocks - 1) * EXPERT_BLOCK
        for r in range(EXPERT_BLOCK):
            gather_copy(tok_ref[base + r], 1 - cur, r).start()
        h = _unpack_bf16_pairs(xbuf[cur]).astype(BF16)
        g = jnp.dot(h, wg_bf[...], preferred_element_type=F32)
        u = jnp.dot(h, wu_bf[...], preferred_element_type=F32)
        a = (g * jax.nn.sigmoid(g) * u).astype(BF16)
        out_ref[...] = _pack_bf16_pairs(jnp.dot(a, wd_bf[...], preferred_element_type=F32))

    for cur in range(2):
        pl.when((b < n_used) & (slot == cur))(functools.partial(expert_block, cur))

    @pl.when(b == n_used)
    def _():
        pltpu.make_async_copy(hp_hbm.at[pl.ds(0, EXPERT_BLOCK)], xbuf.at[slot], gsem.at[slot]).wait()

    @pl.when((b >= n_used) & (b < n_blocks))
    def _():
        out_ref[...] = jnp.zeros_like(out_ref)


def _moe(row_token, block_expert, next_expert, n_used, h_packed, wg, wu, wd):
    n_rows = row_token.shape[0]
    n_blocks = n_rows // EXPERT_BLOCK
    half = D_MODEL // 2
    grid_spec = pltpu.PrefetchScalarGridSpec(
        num_scalar_prefetch=4,
        grid=(n_blocks + 1,),
        in_specs=[
            pl.BlockSpec(memory_space=pl.ANY),
            pl.BlockSpec(memory_space=pl.ANY),
            pl.BlockSpec(memory_space=pl.ANY),
            pl.BlockSpec(memory_space=pl.ANY),
        ],
        out_specs=pl.BlockSpec((EXPERT_BLOCK, half), lambda b, *_: (jnp.minimum(b, n_blocks - 1), 0)),
        scratch_shapes=[
            pltpu.VMEM((2, EXPERT_BLOCK, half), jnp.uint32),
            pltpu.SemaphoreType.DMA((2,)),
            pltpu.VMEM((2, D_MODEL, EXPERT_HIDDEN), F32),
            pltpu.VMEM((2, D_MODEL, EXPERT_HIDDEN), F32),
            pltpu.VMEM((2, EXPERT_HIDDEN, D_MODEL), F32),
            pltpu.SemaphoreType.DMA((2,)),
            pltpu.VMEM((D_MODEL, EXPERT_HIDDEN), BF16),
            pltpu.VMEM((D_MODEL, EXPERT_HIDDEN), BF16),
            pltpu.VMEM((EXPERT_HIDDEN, D_MODEL), BF16),
            pltpu.SMEM((1,), jnp.int32),
        ],
    )
    return pl.pallas_call(
        functools.partial(_moe_kernel, n_blocks=n_blocks),
        out_shape=jax.ShapeDtypeStruct((n_rows, half), jnp.uint32),
        grid_spec=grid_spec,
        compiler_params=_cparams(("arbitrary",)),
        name="moe",
    )(row_token, block_expert, next_expert, n_used, h_packed, wg, wu, wd)


FINAL_TM = 256


def _final_kernel(dest_ref, x1_ref, info_ref, gain_ref, rows_hbm, y_ref, rbuf, sem, *, tok_offset):
    i = pl.program_id(0)
    n = pl.num_programs(0)
    tm = FINAL_TM
    slot = i % 2

    def issue(blk, s):
        base = 2 * (tok_offset + blk * tm)
        for r in range(tm):
            for kk in range(2):
                pltpu.make_async_copy(rows_hbm.at[pl.ds(dest_ref[base + 2 * r + kk], 1)],
                                      rbuf.at[s, kk, pl.ds(r, 1)], sem.at[s]).start()

    @pl.when(i == 0)
    def _():
        issue(0, 0)

    @pl.when(i + 1 < n)
    def _():
        issue(i + 1, 1 - slot)

    for kk in range(2):
        pltpu.make_async_copy(rows_hbm.at[pl.ds(0, tm)], rbuf.at[slot, kk], sem.at[slot]).wait()
    info = info_ref[...]
    x = (x1_ref[...] + info[:, 2:3] * _unpack_bf16_pairs(rbuf[slot, 0])
         + info[:, 3:4] * _unpack_bf16_pairs(rbuf[slot, 1]))
    y_ref[...] = x * lax.rsqrt(jnp.mean(x * x, axis=-1, keepdims=True) + NORM_EPS) * gain_ref[...]


def _final(dest, x1_all, info, gain, out_rows, *, tok_offset, n_tok):
    tm = FINAL_TM
    off = tok_offset // tm
    grid_spec = pltpu.PrefetchScalarGridSpec(
        num_scalar_prefetch=1,
        grid=(n_tok // tm,),
        in_specs=[
            pl.BlockSpec((tm, D_MODEL), lambda i, d: (i + off, 0)),
            pl.BlockSpec((tm, LANES), lambda i, d: (i + off, 0)),
            pl.BlockSpec((1, D_MODEL), lambda i, d: (0, 0)),
            pl.BlockSpec(memory_space=pl.ANY),
        ],
        out_specs=pl.BlockSpec((tm, D_MODEL), lambda i, d: (i, 0)),
        scratch_shapes=[pltpu.VMEM((2, 2, tm, D_MODEL // 2), jnp.uint32), pltpu.SemaphoreType.DMA((2,))],
    )
    return pl.pallas_call(
        functools.partial(_final_kernel, tok_offset=tok_offset),
        out_shape=jax.ShapeDtypeStruct((n_tok, D_MODEL), F32),
        grid_spec=grid_spec,
        compiler_params=_cparams(("arbitrary",)),
        name="final",
    )(dest, x1_all, info, gain, out_rows)


def _rot_table(pos, n_rows):
    half = RET_KEY_DIM // 2
    theta = 10000.0 ** (-jnp.linspace(0.0, 1.0, half, dtype=F32))
    ang = pos.astype(F32)[:, None] * theta[None, :]
    cos, sin = jnp.cos(ang), jnp.sin(ang)
    cos128 = jnp.tile(cos, (1, 4))
    sin128 = jnp.tile(jnp.concatenate([-sin, sin], axis=1), (1, 2))
    k_scale = RET_KEY_DIM ** -0.5
    tab = jnp.concatenate([cos128, sin128, cos128 * k_scale, sin128 * k_scale], axis=1)
    return jnp.tile(tab, (n_rows // tab.shape[0], 1))


def kernel(x_prompt, x_sample, cache_diff_k, cache_diff_v, state_retention, norm_mix_gain, w_in, lambda_q1,
           lambda_k1, lambda_q2, lambda_k2, diff_subln_gain, w_ret_out, w_diff_out, w_out, rel_bias_table,
           norm_ffn_gain, w_group_router, b_group_router, w_expert_router, b_expert_router, w_expert_gate,
           w_expert_up, w_expert_down, norm_final_gain):
    B, S, D = x_prompt.shape
    BS, L, _ = x_sample.shape
    past = cache_diff_k.shape[2]
    TP, TS = B * S, BS * L
    T = TP + TS

    w_in_bf = w_in[0].astype(BF16)
    wr_bf = w_ret_out[0].astype(BF16)
    wd_bf = w_diff_out[0].astype(BF16)
    wo_bf = w_out[0].astype(BF16)
    gain_mix = norm_mix_gain[0][None, :]
    gain_ffn = norm_ffn_gain[0][None, :]
    gain_fin = norm_final_gain[None, :]
    gain_sub = diff_subln_gain[0][None, :]
    lam_rows = jnp.zeros((8, LANES), F32).at[0:4, 0:DIFF_HEAD_DIM].set(
        jnp.stack([lambda_q1[0], lambda_k1[0], lambda_q2[0], lambda_k2[0]]))
    rot_p = _rot_table(jnp.arange(S), S)
    rot_s = _rot_table(past + jnp.arange(L), PROJ_TM)

    far = _far_bias(rel_bias_table)[:, None, None]
    qp = ATT_TQ + jnp.arange(ATT_TQ)
    def tile_bias(k_pos):
        bias = (_relative_bias(qp, k_pos, rel_bias_table) - far) * LOG2E
        visible = (k_pos[None, :] // CHUNK) <= (qp[:, None] // CHUNK)
        bias = jnp.swapaxes(jnp.where(visible[None], bias, NEG_INF), 1, 2)
        return jnp.concatenate([bias, bias], axis=2)
    bias_prev = tile_bias(jnp.arange(ATT_TK))
    bias_diag = tile_bias(ATT_TQ + jnp.arange(ATT_TK))
    q_pos_s = past + jnp.arange(L)
    b_last = (_relative_bias(q_pos_s, past - SAMPLE_TK + jnp.arange(SAMPLE_TK), rel_bias_table) - far) * LOG2E
    b_last = jnp.concatenate([b_last, b_last], axis=1)
    bias_past = jnp.stack([jnp.zeros_like(b_last), b_last])
    b_new = (_relative_bias(q_pos_s, q_pos_s, rel_bias_table) - far) * LOG2E
    bias_new = jnp.concatenate([b_new, b_new], axis=1)

    xp2 = x_prompt.reshape(TP, D)
    xs2 = x_sample.reshape(TS, D)
    z3p, gp, gatesp, v32p, qt, vt5, kt32 = _proj(xp2, gain_mix, w_in_bf, rot_p, seq=S, transposed_k=True)
    z3s, gs, gatess, v32s, k32s = _proj(xs2, gain_mix, w_in_bf, rot_s, seq=L, transposed_k=False)

    zero_state = jnp.zeros((B, N_HEADS, RET_KEY_DIM, HEAD_V), F32)
    orp, ret_p = _retention(z3p, zero_state, batch=B, seq=S, chunk=256)
    ors, ret_s = _retention(z3s, state_retention[0], batch=BS, seq=L, chunk=L)

    gain_col = jnp.broadcast_to(diff_subln_gain[0][:, None], (HEAD_V, ATT_TQ))
    odp = _attn_prompt(z3p, qt, vt5, bias_prev, bias_diag, lam_rows, gain_col, batch=B, seq=S)
    kc = jnp.transpose(cache_diff_k[0], (0, 2, 3, 4, 1)).reshape(BS, N_HEADS, LANES, past)
    vc = cache_diff_v[0].reshape(BS, past * N_HEADS, LANES)
    ods = _attn_sample(z3s, kc, vc, bias_past, bias_new, lam_rows, gain_sub, batch=BS, n_q=L, past=past)

    mgp = _post_a(orp, gp, odp, gatesp, wr_bf, wd_bf)
    mgs = _post_a(ors, gs, ods, gatess, wr_bf, wd_bf)

    w_rt = jnp.zeros((D, LANES), F32)
    w_rt = w_rt.at[:, 0:N_GROUPS].set(w_group_router[0])
    w_rt = w_rt.at[:, N_GROUPS:N_GROUPS + N_EXPERTS].set(
        jnp.transpose(w_expert_router[0], (1, 0, 2)).reshape(D, N_EXPERTS))
    w_rt_hi = w_rt.astype(BF16)
    w_rt_lo = (w_rt - w_rt_hi.astype(F32)).astype(BF16)
    w_router2 = jnp.concatenate([w_rt_hi, w_rt_lo], axis=1)
    bias_row = jnp.zeros((1, LANES), F32)
    bias_row = bias_row.at[0, 0:N_GROUPS].set(b_group_router[0])
    bias_row = bias_row.at[0, N_GROUPS:N_GROUPS + N_EXPERTS].set(b_expert_router[0].reshape(-1))

    x1_all, lg_all, h_packed = _post_b(xp2, mgp, xs2, mgs, wo_bf, gain_ffn, w_router2)

    info, counts = _route(lg_all, bias_row)
    counts = counts[0, :N_EXPERTS].astype(jnp.int32)
    padded = (counts + EXPERT_BLOCK - 1) // EXPERT_BLOCK * EXPERT_BLOCK
    pad_end = jnp.cumsum(padded)
    offs = pad_end - padded
    e12 = info[:, 0:2].astype(jnp.int32)
    rank12 = info[:, 4:6].astype(jnp.int32)
    expert_ids = jnp.arange(N_EXPERTS, dtype=jnp.int32)
    offs_e = jnp.sum(jnp.where(e12[:, :, None] == expert_ids, offs, 0), axis=-1)
    dest = (offs_e + rank12).reshape(-1)
    n_assign = 2 * T
    n_rows = -(-n_assign // EXPERT_BLOCK) * EXPERT_BLOCK + N_EXPERTS * EXPERT_BLOCK
    n_blocks = n_rows // EXPERT_BLOCK
    token_id = (jnp.arange(n_assign, dtype=jnp.int32) // 2)
    row_token = jnp.zeros((n_rows,), jnp.int32).at[dest].set(token_id)
    block_start = jnp.arange(n_blocks, dtype=jnp.int32) * EXPERT_BLOCK
    block_expert = jnp.minimum(jnp.sum((pad_end[None, :] <= block_start[:, None]).astype(jnp.int32), axis=1),
                               N_EXPERTS - 1)
    n_used = (pad_end[-1:] // EXPERT_BLOCK).astype(jnp.int32)
    later_used = (expert_ids[None, :] > expert_ids[:, None]) & (counts[None, :] > 0)
    next_of_expert = jnp.min(jnp.where(later_used, expert_ids[None, :], N_EXPERTS), axis=1)
    next_of_expert = jnp.where(next_of_expert == N_EXPERTS, -1, next_of_expert)
    next_expert = jnp.sum(jnp.where(block_expert[:, None] == expert_ids[None, :], next_of_expert[None, :], 0),
                          axis=1).astype(jnp.int32)

    out_rows = _moe(row_token, block_expert, next_expert, n_used, h_packed,
                    w_expert_gate[0], w_expert_up[0], w_expert_down[0])

    y_p = _final(dest, x1_all, info, gain_fin, out_rows, tok_offset=0, n_tok=TP)
    y_s = _final(dest, x1_all, info, gain_fin, out_rows, tok_offset=TP, n_tok=TS)

    new_k_p = jnp.transpose(kt32, (0, 1, 5, 2, 3, 4))
    new_v_p = v32p.reshape(1, B, S, N_HEADS, HEAD_V)
    new_k_s = k32s.reshape(1, BS, L, N_HEADS, 2, DIFF_HEAD_DIM)
    new_v_s = v32s.reshape(1, BS, L, N_HEADS, HEAD_V)
    return (y_p.reshape(B, S, D), y_s.reshape(BS, L, D), new_k_p, new_v_p, ret_p[None],
            new_k_s, new_v_s, ret_s[None])
```

```python
import functools
import math

import jax
import jax.numpy as jnp
from jax import lax
from jax.experimental import pallas as pl
from jax.experimental.pallas import tpu as pltpu

F32 = jnp.float32
BF16 = jnp.bfloat16

D_MODEL = 2048
CHUNK = 64
N_HEADS = 8
RET_KEY_DIM = 64
HEAD_V = 128
DIFF_HEAD_DIM = 64
IN_WIDTH = 10240
NUM_BUCKETS = 32
MAX_DISTANCE = 128
N_GROUPS = 8
EXPERTS_PER_GROUP = 8
N_EXPERTS = 64
EXPERT_HIDDEN = 512
EXPERT_BLOCK = 128
NORM_EPS = 1e-6
NEG_INF = -1e30
LANES = 128
LOG2E = 1.4426950408889634
VT_ROWS = HEAD_V + 16

VMEM_LIMIT = 56 * 1024 * 1024


def _cparams(sem):
    return pltpu.CompilerParams(dimension_semantics=sem, vmem_limit_bytes=VMEM_LIMIT)


PROJ_TM = 512
PROJ_TN = 1024
PROJ_MC = 128
ATT_TK = 256


def _rotate_pairs(acc, cos, sin):
    outs = []
    lane = lax.broadcasted_iota(jnp.int32, (acc.shape[0], LANES), 1)
    first_half = (lane % 64) < 32
    for c in range(acc.shape[1] // LANES):
        xs = acc[:, c * LANES:(c + 1) * LANES]
        swapped = jnp.where(first_half, pltpu.roll(xs, 96, axis=1), pltpu.roll(xs, 32, axis=1))
        outs.append(xs * cos + swapped * sin)
    return outs


def _proj_kernel(x_ref, gain_ref, w_ref, rot_ref, *out_refs, transposed_k):
    if transposed_k:
        z_ref, g_ref, gates_ref, v32_ref, qt_ref, vt_ref, kt32_ref, h_scr = out_refs
    else:
        z_ref, g_ref, gates_ref, v32_ref, k32_ref, h_scr = out_refs
    j = pl.program_id(1)

    @pl.when(j == 0)
    def _():
        x = x_ref[...]
        ms = jnp.mean(x * x, axis=-1, keepdims=True)
        h_scr[...] = (x * lax.rsqrt(ms + NORM_EPS) * gain_ref[...]).astype(BF16)

    n_slab = PROJ_TN // LANES

    def for_row_chunks(epilogue):
        for c in range(PROJ_TM // PROJ_MC):
            rows = slice(c * PROJ_MC, (c + 1) * PROJ_MC)
            epilogue(c, rows, jnp.dot(h_scr[rows, :], w_ref[...], preferred_element_type=F32))

    def store_slabs(rows, vals):
        for s in range(n_slab):
            z_ref[s, rows, :] = vals[s].astype(BF16)

    def split(a):
        return [a[:, s * LANES:(s + 1) * LANES] for s in range(n_slab)]

    @pl.when(j == 0)
    def _():
        def epilogue(c, rows, acc):
            half = PROJ_TN // 2
            q = _rotate_pairs(acc[:, :half], rot_ref[rows, 0:128], rot_ref[rows, 128:256])
            k = _rotate_pairs(acc[:, half:], rot_ref[rows, 256:384], rot_ref[rows, 384:512])
            store_slabs(rows, q + k)
        for_row_chunks(epilogue)

    @pl.when(j == 1)
    def _():
        for_row_chunks(lambda c, rows, acc: store_slabs(rows, split(acc)))

    @pl.when(j == 2)
    def _():
        def epilogue(c, rows, acc):
            g_ref[rows, :] = (acc * jax.nn.sigmoid(acc)).astype(BF16)
        for_row_chunks(epilogue)

    @pl.when(j == 3)
    def _():
        def epilogue(c, rows, acc):
            qs = acc * (DIFF_HEAD_DIM ** -0.5 * LOG2E)
            store_slabs(rows, split(qs))
            if transposed_k:
                qs_t = qs.T.astype(BF16)
                for hh in range(n_slab):
                    qt_ref[0, hh, :, rows] = qs_t[hh * LANES:(hh + 1) * LANES, :]
        for_row_chunks(epilogue)

    @pl.when(j == 4)
    def _():
        def epilogue(c, rows, acc):
            store_slabs(rows, split(acc))
            if transposed_k:
                kt32_ref[0, 0, :, :, :, rows] = acc.T.reshape(n_slab, 2, DIFF_HEAD_DIM, PROJ_MC)
            else:
                k32_ref[rows, :] = acc
        for_row_chunks(epilogue)

    @pl.when(j == 5)
    def _():
        def epilogue(c, rows, acc):
            store_slabs(rows, split(acc))
            v32_ref[rows, :] = acc
            if transposed_k:
                acc_tb = acc.T.astype(BF16)
                t, cols = (c * PROJ_MC) // ATT_TK, slice((c * PROJ_MC) % ATT_TK, (c * PROJ_MC) % ATT_TK + PROJ_MC)
                ones = jnp.ones((VT_ROWS - HEAD_V, PROJ_MC), BF16)
                for hh in range(n_slab):
                    vt_ref[0, hh, t, 0:HEAD_V, cols] = acc_tb[hh * LANES:(hh + 1) * LANES, :]
                    vt_ref[0, hh, t, HEAD_V:VT_ROWS, cols] = ones
        for_row_chunks(epilogue)

    @pl.when(j >= 6)
    def _():
        def epilogue(c, rows, acc):
            gates_ref[rows, :] = jax.nn.sigmoid(acc).astype(BF16)
        for_row_chunks(epilogue)


def _proj(x2d, gain, w_bf, rot, *, seq, transposed_k):
    T = x2d.shape[0]
    tm, tn = PROJ_TM, PROJ_TN
    ni, nj = T // tm, IN_WIDTH // tn
    rot_blocks = rot.shape[0] // tm

    def zmap(i, j):
        zj = jnp.where(j < 2, j, jnp.where(j < 3, 1, jnp.where(j < 6, j - 1, 4)))
        return (zj, i, 0)

    out_shape = [
        jax.ShapeDtypeStruct((40, T, LANES), BF16),
        jax.ShapeDtypeStruct((T, 1024), BF16),
        jax.ShapeDtypeStruct((T, 4096), BF16),
        jax.ShapeDtypeStruct((T, 1024), F32),
    ]
    out_specs = [
        pl.BlockSpec((8, tm, LANES), zmap),
        pl.BlockSpec((tm, tn), lambda i, j: (i, 0)),
        pl.BlockSpec((tm, tn), lambda i, j: (i, jnp.clip(j - 6, 0, 3))),
        pl.BlockSpec((tm, tn), lambda i, j: (i, 0)),
    ]
    if transposed_k:
        B = T // seq
        spb = seq // tm
        out_shape += [
            jax.ShapeDtypeStruct((B, N_HEADS, LANES, seq), BF16),
            jax.ShapeDtypeStruct((B, N_HEADS, seq // ATT_TK, VT_ROWS, ATT_TK), BF16),
            jax.ShapeDtypeStruct((1, B, N_HEADS, 2, DIFF_HEAD_DIM, seq), F32),
        ]
        out_specs += [
            pl.BlockSpec((1, N_HEADS, LANES, tm), lambda i, j: (i // spb, 0, 0, i % spb)),
            pl.BlockSpec((1, N_HEADS, tm // ATT_TK, VT_ROWS, ATT_TK), lambda i, j: (i // spb, 0, i % spb, 0, 0)),
            pl.BlockSpec((1, 1, N_HEADS, 2, DIFF_HEAD_DIM, tm), lambda i, j: (0, i // spb, 0, 0, 0, i % spb)),
        ]
    else:
        out_shape += [jax.ShapeDtypeStruct((T, 1024), F32)]
        out_specs += [pl.BlockSpec((tm, tn), lambda i, j: (i, 0))]

    return pl.pallas_call(
        functools.partial(_proj_kernel, transposed_k=transposed_k),
        out_shape=out_shape,
        grid=(ni, nj),
        in_specs=[
            pl.BlockSpec((tm, D_MODEL), lambda i, j: (i, 0)),
            pl.BlockSpec((1, D_MODEL), lambda i, j: (0, 0)),
            pl.BlockSpec((D_MODEL, tn), lambda i, j: (0, j)),
            pl.BlockSpec((tm, 512), lambda i, j: (i % rot_blocks, 0)),
        ],
        out_specs=out_specs,
        scratch_shapes=[pltpu.VMEM((tm, D_MODEL), BF16)],
        compiler_params=_cparams(("arbitrary", "arbitrary")),
        name="proj_t" if transposed_k else "proj",
    )(x2d, gain, w_bf, rot)


def _retention_kernel(q_ref, k_ref, v_ref, decay_ref, inner_ref, outer_ref, st0_ref,
                      o_ref, st_out_ref, st_scr, *, chunk):
    c = pl.program_id(1)
    nc = pl.num_programs(1)

    @pl.when(c == 0)
    def _():
        st_scr[...] = jnp.zeros_like(st_scr)
        for h in range(N_HEADS):
            a = h % 2
            st_scr[h, a * RET_KEY_DIM:(a + 1) * RET_KEY_DIM, :] = st0_ref[0, h]

    lane_lo = lax.broadcasted_iota(jnp.int32, (chunk, LANES), 1) < RET_KEY_DIM
    row_lo = lax.broadcasted_iota(jnp.int32, (LANES, LANES), 0) < RET_KEY_DIM
    for h in range(N_HEADS):
        p, a = h // 2, h % 2
        q = q_ref[p]
        k = k_ref[p]
        v = v_ref[h]
        qa = jnp.where(lane_lo == (a == 0), q, jnp.zeros_like(q))
        s = lax.dot_general(qa, k, (((1,), (1,)), ((), ())), preferred_element_type=F32)
        s = s * decay_ref[h]
        st = st_scr[h]
        inner = inner_ref[h]
        o = (jnp.dot(s.astype(BF16), v, preferred_element_type=F32)
             + jnp.dot(qa, st.astype(BF16), preferred_element_type=F32) * inner)
        ko = (k.astype(F32) * outer_ref[p]).astype(BF16)
        upd = lax.dot_general(ko, v, (((0,), (0,)), ((), ())), preferred_element_type=F32)
        upd = jnp.where(row_lo == (a == 0), upd, 0.0)
        chunk_decay = inner[chunk - 1:chunk, :]
        st_scr[h] = chunk_decay * st + upd
        o_n = o * lax.rsqrt(jnp.mean(o * o, axis=-1, keepdims=True) + NORM_EPS)
        o_ref[:, h * HEAD_V:(h + 1) * HEAD_V] = o_n.astype(BF16)

    @pl.when(c == nc - 1)
    def _():
        for h in range(N_HEADS):
            a = h % 2
            st_out_ref[0, h] = st_scr[h, a * RET_KEY_DIM:(a + 1) * RET_KEY_DIM, :]


def _retention_consts(chunk):
    log_g = jnp.log(1.0 - 2.0 ** (-5.0 - jnp.arange(N_HEADS, dtype=F32)))
    j = jnp.arange(chunk, dtype=F32)
    diff = j[:, None] - j[None, :]
    decay = jnp.where(diff >= 0, jnp.exp(log_g[:, None, None] * jnp.maximum(diff, 0.0)), 0.0)
    inner = jnp.exp(log_g[:, None] * (j + 1.0))
    outer = jnp.exp(log_g[:, None] * (chunk - 1.0 - j))
    inner_b = jnp.broadcast_to(inner[:, :, None], (N_HEADS, chunk, LANES))
    outer_pair = jnp.repeat(outer.reshape(N_HEADS // 2, 2, chunk).transpose(0, 2, 1), RET_KEY_DIM, axis=2)
    return decay, inner_b, outer_pair


def _retention(z3, state0, *, batch, seq, chunk):
    T = batch * seq
    nc = seq // chunk
    decay, inner_b, outer_pair = _retention_consts(chunk)
    return pl.pallas_call(
        functools.partial(_retention_kernel, chunk=chunk),
        out_shape=[jax.ShapeDtypeStruct((T, 1024), BF16),
                   jax.ShapeDtypeStruct((batch, N_HEADS, RET_KEY_DIM, HEAD_V), F32)],
        grid=(batch, nc),
        in_specs=[
            pl.BlockSpec((4, chunk, LANES), lambda b, c: (0, b * nc + c, 0)),
            pl.BlockSpec((4, chunk, LANES), lambda b, c: (1, b * nc + c, 0)),
            pl.BlockSpec((8, chunk, LANES), lambda b, c: (1, b * nc + c, 0)),
            pl.BlockSpec((N_HEADS, chunk, chunk), lambda b, c: (0, 0, 0)),
            pl.BlockSpec((N_HEADS, chunk, LANES), lambda b, c: (0, 0, 0)),
            pl.BlockSpec((N_HEADS // 2, chunk, LANES), lambda b, c: (0, 0, 0)),
            pl.BlockSpec((1, N_HEADS, RET_KEY_DIM, HEAD_V), lambda b, c: (b, 0, 0, 0)),
        ],
        out_specs=[
            pl.BlockSpec((chunk, 1024), lambda b, c: (b * nc + c, 0)),
            pl.BlockSpec((1, N_HEADS, RET_KEY_DIM, HEAD_V), lambda b, c: (b, 0, 0, 0)),
        ],
        scratch_shapes=[pltpu.VMEM((N_HEADS, LANES, LANES), F32)],
        compiler_params=_cparams(("arbitrary", "arbitrary")),
        name=f"retention_c{chunk}",
    )(z3, z3, z3, decay, inner_b, outer_pair, state0)


def _lam_init(layer=0):
    return 0.8 - 0.6 * math.exp(-0.3 * layer)


def _lam_from_ref(lam_ref):
    lp = lam_ref[...]
    s1 = jnp.sum(lp[0:1] * lp[1:2], axis=-1, keepdims=True)
    s2 = jnp.sum(lp[2:3] * lp[3:4], axis=-1, keepdims=True)
    return jnp.exp(s1) - jnp.exp(s2) + _lam_init()


def _stack_q(q):
    lane_lo = lax.broadcasted_iota(jnp.int32, q.shape, 1) < DIFF_HEAD_DIM
    zero = jnp.zeros_like(q)
    return jnp.concatenate([jnp.where(lane_lo, q, zero), jnp.where(lane_lo, zero, q)], axis=0)


def _with_ones(v):
    return jnp.concatenate([v, jnp.ones_like(v)], axis=1)


def _diff_finish(acc_ref, lam, gain, n):
    o = acc_ref[:, 0:HEAD_V] / acc_ref[:, HEAD_V:2 * HEAD_V]
    o = o[:n] - lam * o[n:]
    o = o * lax.rsqrt(jnp.mean(o * o, axis=-1, keepdims=True) + NORM_EPS) * gain
    return o * (1.0 - _lam_init())


def _relative_bias(q_pos, k_pos, table):
    rel = k_pos[None, :] - q_pos[:, None]
    half = NUM_BUCKETS // 2
    max_exact = half // 2
    n = jnp.abs(rel)
    log_ratio = jnp.log(jnp.maximum(n, 1).astype(F32) / max_exact) / math.log(MAX_DISTANCE / max_exact)
    large = jnp.minimum(max_exact + (log_ratio * (half - max_exact)).astype(jnp.int32), half - 1)
    bucket = (rel > 0).astype(jnp.int32) * half + jnp.where(n < max_exact, n, large)
    out = jnp.zeros((table.shape[1],) + bucket.shape, F32)
    for bkt in range(NUM_BUCKETS):
        out = jnp.where(bucket[None] == bkt, table[bkt].astype(F32)[:, None, None], out)
    return out


def _far_bias(table):
    return table[NUM_BUCKETS // 2 - 1].astype(F32)


ATT_TQ = 256
ATT_HP = 8


def _attn_prompt_kernel(qt_ref, k_ref, vt_ref, bprev_ref, bdiag_ref, lam_ref, gain_ref, o_ref, acc_scr):
    i = pl.program_id(2)
    row_lo = lax.broadcasted_iota(jnp.int32, (LANES, ATT_TQ), 0) < DIFF_HEAD_DIM
    qs = []
    for hh in range(ATT_HP):
        qt = qt_ref[0, hh]
        zero = jnp.zeros_like(qt)
        qs.append(jnp.concatenate([jnp.where(row_lo, qt, zero), jnp.where(row_lo, zero, qt)], axis=1))
    acc_scr[...] = jnp.zeros_like(acc_scr)

    def tile(j, ms, biases):
        ss = []
        for hh in range(ATT_HP):
            kt = k_ref[hh, pl.ds(pl.multiple_of(j * ATT_TK, ATT_TK), ATT_TK), :]
            s = jnp.dot(kt, qs[hh], preferred_element_type=F32)
            ss.append(s if biases is None else s + biases[hh])
        ps, alphas, m_out = [], [], []
        for hh in range(ATT_HP):
            m_new = jnp.maximum(ms[hh], jnp.max(ss[hh], axis=0, keepdims=True))
            alphas.append(jnp.exp2(ms[hh] - m_new))
            ps.append(jnp.exp2(ss[hh] - m_new).astype(BF16))
            m_out.append(m_new)
        for hh in range(ATT_HP):
            acc_scr[hh] = alphas[hh] * acc_scr[hh] + jnp.dot(vt_ref[0, hh, j], ps[hh],
                                                             preferred_element_type=F32)
        return tuple(m_out)

    ms = tuple(jnp.full((1, 2 * ATT_TQ), NEG_INF, F32) for _ in range(ATT_HP))
    ms = lax.fori_loop(0, i - 1, lambda j, c: tile(j, c, None), ms)
    jp = jnp.maximum(i - 1, 0)
    ms = lax.cond(i >= 1,
                  lambda c: tile(jp, c, [bprev_ref[hh] for hh in range(ATT_HP)]),
                  lambda c: c, ms)
    tile(i, ms, [bdiag_ref[hh] for hh in range(ATT_HP)])

    lam = _lam_from_ref(lam_ref)
    for hh in range(ATT_HP):
        o = acc_scr[hh, 0:HEAD_V, :] / acc_scr[hh, HEAD_V:HEAD_V + 1, :]
        o = o[:, :ATT_TQ] - lam * o[:, ATT_TQ:]
        o = o * lax.rsqrt(jnp.mean(o * o, axis=0, keepdims=True) + NORM_EPS) * gain_ref[...]
        o_ref[:, hh * HEAD_V:(hh + 1) * HEAD_V] = (o * (1.0 - _lam_init())).T.astype(BF16)


def _attn_prompt(z3, qt, vt5, bias_prev, bias_diag, lam_rows, gain_col, *, batch, seq):
    T = batch * seq
    nq = seq // ATT_TQ
    hp = ATT_HP
    return pl.pallas_call(
        _attn_prompt_kernel,
        out_shape=jax.ShapeDtypeStruct((T, 1024), BF16),
        grid=(batch, N_HEADS // hp, nq),
        in_specs=[
            pl.BlockSpec((1, hp, LANES, ATT_TQ), lambda b, g, i: (b, g, 0, i)),
            pl.BlockSpec((hp, seq, LANES), lambda b, g, i: (24 // hp + g, b, 0), pipeline_mode=pl.Buffered(1)),
            pl.BlockSpec((1, hp, seq // ATT_TK, VT_ROWS, ATT_TK), lambda b, g, i: (b, g, 0, 0, 0),
                         pipeline_mode=pl.Buffered(1)),
            pl.BlockSpec((hp, ATT_TK, 2 * ATT_TQ), lambda b, g, i: (g, 0, 0), pipeline_mode=pl.Buffered(1)),
            pl.BlockSpec((hp, ATT_TK, 2 * ATT_TQ), lambda b, g, i: (g, 0, 0), pipeline_mode=pl.Buffered(1)),
            pl.BlockSpec((8, LANES), lambda b, g, i: (0, 0)),
            pl.BlockSpec((LANES, ATT_TQ), lambda b, g, i: (0, 0)),
        ],
        out_specs=pl.BlockSpec((ATT_TQ, hp * HEAD_V), lambda b, g, i: (b * nq + i, g)),
        scratch_shapes=[pltpu.VMEM((hp, VT_ROWS, 2 * ATT_TQ), F32)],
        compiler_params=_cparams(("arbitrary", "arbitrary", "arbitrary")),
        name="attn_prompt",
    )(qt, z3, vt5, bias_prev, bias_diag, lam_rows, gain_col)


SAMPLE_TK = 512


def _attn_sample_kernel(q_ref, kc_ref, vc_ref, kn_ref, vn_ref, bpast_ref, bnew_ref, lam_ref, gain_ref,
                        o_ref, m_scr, acc_scr, *, n_past_tiles, n_q):
    t = pl.program_id(1)

    @pl.when(t == 0)
    def _():
        m_scr[...] = jnp.full_like(m_scr, NEG_INF)
        acc_scr[...] = jnp.zeros_like(acc_scr)

    def update(logits, values):
        ps, alphas = [], []
        for h in range(N_HEADS):
            m_prev = m_scr[h]
            m_new = jnp.maximum(m_prev, jnp.max(logits[h], axis=1, keepdims=True))
            alpha = jnp.exp2(m_prev - m_new)
            ps.append(jnp.exp2(logits[h] - m_new[:, 0:1]).astype(BF16))
            alphas.append(jnp.concatenate([alpha, alpha], axis=1))
            m_scr[h] = m_new
        for h in range(N_HEADS):
            acc_scr[h] = alphas[h] * acc_scr[h] + jnp.dot(ps[h], _with_ones(values[h]()),
                                                          preferred_element_type=F32)

    @pl.when(t < n_past_tiles)
    def _():
        logits = []
        for h in range(N_HEADS):
            kt = kc_ref[0, h].astype(BF16)
            logits.append(jnp.dot(_stack_q(q_ref[h]), kt, preferred_element_type=F32) + bpast_ref[0, h])
        update(logits, [lambda h=h: vc_ref[0, pl.ds(h, SAMPLE_TK, stride=N_HEADS), :].astype(BF16)
                        for h in range(N_HEADS)])

    @pl.when(t == n_past_tiles)
    def _():
        logits = []
        for h in range(N_HEADS):
            s = lax.dot_general(_stack_q(q_ref[h]), kn_ref[h], (((1,), (1,)), ((), ())),
                                preferred_element_type=F32)
            logits.append(s + bnew_ref[h])
        update(logits, [lambda h=h: vn_ref[h] for h in range(N_HEADS)])
        lam = _lam_from_ref(lam_ref)
        for h in range(N_HEADS):
            o = _diff_finish(acc_scr.at[h], lam, gain_ref[...], n_q)
            o_ref[:, h * HEAD_V:(h + 1) * HEAD_V] = o.astype(BF16)


def _attn_sample(z3, kc, vc, bias_past, bias_new, lam_rows, gain, *, batch, n_q, past):
    npt = past // SAMPLE_TK
    return pl.pallas_call(
        functools.partial(_attn_sample_kernel, n_past_tiles=npt, n_q=n_q),
        out_shape=jax.ShapeDtypeStruct((batch * n_q, 1024), BF16),
        grid=(batch, npt + 1),
        in_specs=[
            pl.BlockSpec((8, n_q, LANES), lambda b, t: (2, b, 0)),
            pl.BlockSpec((1, N_HEADS, LANES, SAMPLE_TK), lambda b, t: (b, 0, 0, jnp.minimum(t, npt - 1))),
            pl.BlockSpec((1, SAMPLE_TK * N_HEADS, LANES), lambda b, t: (b, jnp.minimum(t, npt - 1), 0)),
            pl.BlockSpec((8, n_q, LANES), lambda b, t: (3, b, 0)),
            pl.BlockSpec((8, n_q, LANES), lambda b, t: (4, b, 0)),
            pl.BlockSpec((1, N_HEADS, 2 * n_q, SAMPLE_TK), lambda b, t: (jnp.where(t == npt - 1, 1, 0), 0, 0, 0)),
            pl.BlockSpec((N_HEADS, 2 * n_q, n_q), lambda b, t: (0, 0, 0)),
            pl.BlockSpec((8, LANES), lambda b, t: (0, 0)),
            pl.BlockSpec((1, LANES), lambda b, t: (0, 0)),
        ],
        out_specs=pl.BlockSpec((n_q, 1024), lambda b, t: (b, 0)),
        scratch_shapes=[pltpu.VMEM((N_HEADS, 2 * n_q, LANES), F32),
                        pltpu.VMEM((N_HEADS, 2 * n_q, 2 * LANES), F32)],
        compiler_params=_cparams(("arbitrary", "arbitrary")),
        name="attn_sample",
    )(z3, kc, vc, z3, z3, bias_past, bias_new, lam_rows, gain)


POST_TM = 256


def _post_a_kernel(or_ref, g_ref, od_ref, gr_ref, gd_ref, wr_ref, wd_ref, out_ref):
    a = jnp.dot(or_ref[...] * g_ref[...], wr_ref[...], preferred_element_type=F32)
    b = jnp.dot(od_ref[...], wd_ref[...], preferred_element_type=F32)
    out_ref[...] = (gr_ref[...].astype(F32) * a + gd_ref[...].astype(F32) * b).astype(BF16)


def _post_a(o_r, g, o_d, gates, wr_bf, wd_bf):
    T = o_r.shape[0]
    tm = POST_TM
    return pl.pallas_call(
        _post_a_kernel,
        out_shape=jax.ShapeDtypeStruct((T, D_MODEL), BF16),
        grid=(T // tm,),
        in_specs=[
            pl.BlockSpec((tm, 1024), lambda i: (i, 0)),
            pl.BlockSpec((tm, 1024), lambda i: (i, 0)),
            pl.BlockSpec((tm, 1024), lambda i: (i, 0)),
            pl.BlockSpec((tm, D_MODEL), lambda i: (i, 0)),
            pl.BlockSpec((tm, D_MODEL), lambda i: (i, 1)),
            pl.BlockSpec((1024, D_MODEL), lambda i: (0, 0)),
            pl.BlockSpec((1024, D_MODEL), lambda i: (0, 0)),
        ],
        out_specs=pl.BlockSpec((tm, D_MODEL), lambda i: (i, 0)),
        compiler_params=_cparams(("arbitrary",)),
        name="post_a",
    )(o_r, g, o_d, gates, gates, wr_bf, wd_bf)


def _pack_bf16_pairs(x):
    w = x.shape[1] // 2
    xb = x.astype(BF16).astype(F32)
    lo = pltpu.bitcast(xb[:, :w], jnp.uint32) >> 16
    hi = pltpu.bitcast(xb[:, w:], jnp.uint32) & jnp.uint32(0xFFFF0000)
    return lo | hi


def _unpack_bf16_pairs(words):
    lo = pltpu.bitcast(words << 16, F32)
    hi = pltpu.bitcast(words & jnp.uint32(0xFFFF0000), F32)
    return jnp.concatenate([lo, hi], axis=1)


def _post_b_kernel(xp_ref, mp_ref, xs_ref, ms_ref, wo_ref, gain_ref, wrt_ref, x1_ref, lg_ref, hp_ref, *,
                   n_prompt_blocks):
    i = pl.program_id(0)
    tm = POST_TM

    def body(x_ref, mg_ref):
        x1 = x_ref[...] + jnp.dot(mg_ref[...], wo_ref[...], preferred_element_type=F32)
        x1_ref[...] = x1
        h2 = x1 * lax.rsqrt(jnp.mean(x1 * x1, axis=-1, keepdims=True) + NORM_EPS) * gain_ref[...]
        hp_ref[...] = _pack_bf16_pairs(h2)
        h_hi = h2.astype(BF16)
        h_lo = (h2 - h_hi.astype(F32)).astype(BF16)
        r = jnp.dot(jnp.concatenate([h_hi, h_lo], axis=0), wrt_ref[...], preferred_element_type=F32)
        lg_ref[...] = (r[:tm, :LANES] + r[:tm, LANES:]) + (r[tm:, :LANES] + r[tm:, LANES:])

    @pl.when(i < n_prompt_blocks)
    def _():
        body(xp_ref, mp_ref)

    @pl.when(i >= n_prompt_blocks)
    def _():
        body(xs_ref, ms_ref)


def _post_b(xp2, mgp, xs2, mgs, wo_bf, gain, w_router2):
    tm = POST_TM
    npb, nsb = xp2.shape[0] // tm, xs2.shape[0] // tm
    T = xp2.shape[0] + xs2.shape[0]
    pmap = lambda i: (jnp.minimum(i, npb - 1), 0)
    smap = lambda i: (jnp.maximum(i - npb, 0), 0)
    return pl.pallas_call(
        functools.partial(_post_b_kernel, n_prompt_blocks=npb),
        out_shape=[jax.ShapeDtypeStruct((T, D_MODEL), F32),
                   jax.ShapeDtypeStruct((T, LANES), F32),
                   jax.ShapeDtypeStruct((T, D_MODEL // 2), jnp.uint32)],
        grid=(npb + nsb,),
        in_specs=[
            pl.BlockSpec((tm, D_MODEL), pmap),
            pl.BlockSpec((tm, D_MODEL), pmap),
            pl.BlockSpec((tm, D_MODEL), smap),
            pl.BlockSpec((tm, D_MODEL), smap),
            pl.BlockSpec((D_MODEL, D_MODEL), lambda i: (0, 0)),
            pl.BlockSpec((1, D_MODEL), lambda i: (0, 0)),
            pl.BlockSpec((D_MODEL, 2 * LANES), lambda i: (0, 0)),
        ],
        out_specs=[pl.BlockSpec((tm, D_MODEL), lambda i: (i, 0)),
                   pl.BlockSpec((tm, LANES), lambda i: (i, 0)),
                   pl.BlockSpec((tm, D_MODEL // 2), lambda i: (i, 0))],
        compiler_params=_cparams(("arbitrary",)),
        name="post_b",
    )(xp2, mgp, xs2, mgs, wo_bf, gain, w_router2)


ROUTE_TM = 512


def _route_kernel(lg_ref, bias_ref, info_ref, cnt_ref, tri_scr, carry_scr):
    i = pl.program_id(0)
    tm = ROUTE_TM

    @pl.when(i == 0)
    def _():
        r = lax.broadcasted_iota(jnp.int32, (tm, tm), 0)
        c = lax.broadcasted_iota(jnp.int32, (tm, tm), 1)
        tri_scr[...] = jnp.where(c < r, 1.0, 0.0).astype(BF16)
        carry_scr[...] = jnp.zeros_like(carry_scr)

    lg = lg_ref[...] + bias_ref[...]
    lane = lax.broadcasted_iota(jnp.int32, (tm, LANES), 1)
    lane_f = lane.astype(F32)
    neg = jnp.float32(-jnp.inf)

    def first_argmax(vals):
        top = jnp.max(vals, axis=1, keepdims=True)
        idx = jnp.min(jnp.where(vals == top, lane_f, float(LANES)), axis=1, keepdims=True)
        return top, idx

    is_group = lane < N_GROUPS
    gl = jnp.where(is_group, lg, neg)
    g_top, g_idx = first_argmax(gl)
    g_weight = 1.0 / jnp.sum(jnp.exp(gl - g_top), axis=1, keepdims=True)
    lane_group = ((lane - N_GROUPS) >> 3).astype(F32)
    in_group = (lane >= N_GROUPS) & (lane < N_GROUPS + N_EXPERTS) & (lane_group == g_idx)
    el = jnp.where(in_group, lg, neg)
    t1, i1 = first_argmax(el)
    el2 = jnp.where(lane_f == i1, neg, el)
    t2, i2 = first_argmax(el2)
    e2w = jnp.exp(t2 - t1)
    p1 = 1.0 / (1.0 + e2w)
    gate1 = g_weight * p1
    gate2 = g_weight * (e2w * p1)
    e1 = i1 - float(N_GROUPS)
    e2 = i2 - float(N_GROUPS)

    hot1 = lane_f == e1
    hot2 = lane_f == e2
    both = jnp.where(hot1 | hot2, 1.0, 0.0)
    prefix = jnp.dot(tri_scr[...], both.astype(BF16), preferred_element_type=F32) + carry_scr[...]
    rank1 = jnp.sum(jnp.where(hot1, prefix, 0.0), axis=1, keepdims=True)
    rank2 = jnp.sum(jnp.where(hot2, prefix, 0.0), axis=1, keepdims=True)
    carry_scr[...] = carry_scr[...] + jnp.sum(both, axis=0, keepdims=True)

    info = jnp.where(lane == 0, e1, jnp.where(lane == 1, e2, jnp.where(lane == 2, gate1, jnp.where(
        lane == 3, gate2, jnp.where(lane == 4, rank1, jnp.where(lane == 5, rank2, 0.0))))))
    info_ref[...] = info
    cnt_ref[...] = carry_scr[...]


def _route(lg_all, bias_row):
    T = lg_all.shape[0]
    tm = ROUTE_TM
    return pl.pallas_call(
        _route_kernel,
        out_shape=[jax.ShapeDtypeStruct((T, LANES), F32), jax.ShapeDtypeStruct((1, LANES), F32)],
        grid=(T // tm,),
        in_specs=[pl.BlockSpec((tm, LANES), lambda i: (i, 0)), pl.BlockSpec((1, LANES), lambda i: (0, 0))],
        out_specs=[pl.BlockSpec((tm, LANES), lambda i: (i, 0)), pl.BlockSpec((1, LANES), lambda i: (0, 0))],
        scratch_shapes=[pltpu.VMEM((tm, tm), BF16), pltpu.VMEM((1, LANES), F32)],
        compiler_params=_cparams(("arbitrary",)),
        name="route",
    )(lg_all, bias_row)


WEIGHT_DMA_PRIORITY = 1


def _moe_kernel(dest_ref, bexp_ref, nxt_ref, pad_ref, nused_ref, hp_hbm, wg_hbm, wu_hbm, wd_hbm,
                out_ref, xbuf, gsem, wgbuf, wubuf, wdbuf, wsem, wg_bf, wu_bf, wd_bf, wslot, tok_ref, *,
                n_blocks):
    b = pl.program_id(0)
    n_used = nused_ref[0]
    slot = b % 2

    def gather_copy(row, s, r):
        return pltpu.make_async_copy(hp_hbm.at[pl.ds(row, 1)], xbuf.at[s, pl.ds(r, 1)], gsem.at[s])

    def weight_copies(e, s):
        return (pltpu.make_async_copy(wg_hbm.at[e], wgbuf.at[s], wsem.at[s]),
                pltpu.make_async_copy(wu_hbm.at[e], wubuf.at[s], wsem.at[s]),
                pltpu.make_async_copy(wd_hbm.at[e], wdbuf.at[s], wsem.at[s]))

    def issue(blk, s):
        base = blk * EXPERT_BLOCK

        def body(r, carry):
            gather_copy(tok_ref[base + r], s, r).start()
            return carry

        lax.fori_loop(0, EXPERT_BLOCK, body, 0, unroll=8)

    @pl.when(b == 0)
    def _():
        for cp in weight_copies(bexp_ref[0], 0):
            cp.start(priority=WEIGHT_DMA_PRIORITY)
        wslot[0] = 0

        def zero_rows(lo, hi):
            def zero_body(r, carry):
                tok_ref[r] = 0
                return carry

            lax.fori_loop(lo, hi, zero_body, 0)

        def expert_body(e, carry):
            zero_rows(pad_ref[e], pad_ref[N_EXPERTS + e])
            return carry

        lax.fori_loop(0, N_EXPERTS, expert_body, 0)
        tail = n_used * EXPERT_BLOCK
        zero_rows(tail, jnp.minimum(tail + EXPERT_BLOCK, n_blocks * EXPERT_BLOCK))

        def fill_body(a, carry):
            tok_ref[dest_ref[a]] = a >> 1
            return carry

        lax.fori_loop(0, dest_ref.shape[0], fill_body, 0, unroll=8)
        issue(0, 0)

    @pl.when(b < n_used)
    def _():
        e = bexp_ref[b]
        new_expert = (b == 0) | (e != bexp_ref[jnp.maximum(b - 1, 0)])

        @pl.when(new_expert)
        def _():
            @pl.when(b > 0)
            def _():
                wslot[0] = 1 - wslot[0]

            s = wslot[0]
            for cp in weight_copies(e, s):
                cp.wait()
            nxt = nxt_ref[b]

            @pl.when(nxt >= 0)
            def _():
                for cp in weight_copies(nxt, 1 - s):
                    cp.start(priority=WEIGHT_DMA_PRIORITY)

            wg_bf[...] = wgbuf[s].astype(BF16)
            wu_bf[...] = wubuf[s].astype(BF16)
            wd_bf[...] = wdbuf[s].astype(BF16)

    def expert_block(cur):
        pltpu.make_async_copy(hp_hbm.at[pl.ds(0, EXPERT_BLOCK)], xbuf.at[cur], gsem.at[cur]).wait()
        base = jnp.minimum(b + 1, n_blocks - 1) * EXPERT_BLOCK
        for r in range(EXPERT_BLOCK):
            gather_copy(tok_ref[base + r], 1 - cur, r).start()
        h = _unpack_bf16_pairs(xbuf[cur]).astype(BF16)
        g = jnp.dot(h, wg_bf[...], preferred_element_type=F32)
        u = jnp.dot(h, wu_bf[...], preferred_element_type=F32)
        a = (g * jax.nn.sigmoid(g) * u).astype(BF16)
        out_ref[...] = _pack_bf16_pairs(jnp.dot(a, wd_bf[...], preferred_element_type=F32))

    for cur in range(2):
        pl.when((b < n_used) & (slot == cur))(functools.partial(expert_block, cur))

    @pl.when(b == n_used)
    def _():
        pltpu.make_async_copy(hp_hbm.at[pl.ds(0, EXPERT_BLOCK)], xbuf.at[slot], gsem.at[slot]).wait()

    @pl.when((b >= n_used) & (b < n_blocks))
    def _():
        out_ref[...] = jnp.zeros_like(out_ref)


def _moe(dest, block_expert, next_expert, pad_bounds, n_used, h_packed, wg, wu, wd):
    n_blocks = block_expert.shape[0]
    n_rows = n_blocks * EXPERT_BLOCK
    half = D_MODEL // 2
    grid_spec = pltpu.PrefetchScalarGridSpec(
        num_scalar_prefetch=5,
        grid=(n_blocks + 1,),
        in_specs=[
            pl.BlockSpec(memory_space=pl.ANY),
            pl.BlockSpec(memory_space=pl.ANY),
            pl.BlockSpec(memory_space=pl.ANY),
            pl.BlockSpec(memory_space=pl.ANY),
        ],
        out_specs=pl.BlockSpec((EXPERT_BLOCK, half), lambda b, *_: (jnp.minimum(b, n_blocks - 1), 0)),
        scratch_shapes=[
            pltpu.VMEM((2, EXPERT_BLOCK, half), jnp.uint32),
            pltpu.SemaphoreType.DMA((2,)),
            pltpu.VMEM((2, D_MODEL, EXPERT_HIDDEN), F32),
            pltpu.VMEM((2, D_MODEL, EXPERT_HIDDEN), F32),
            pltpu.VMEM((2, EXPERT_HIDDEN, D_MODEL), F32),
            pltpu.SemaphoreType.DMA((2,)),
            pltpu.VMEM((D_MODEL, EXPERT_HIDDEN), BF16),
            pltpu.VMEM((D_MODEL, EXPERT_HIDDEN), BF16),
            pltpu.VMEM((EXPERT_HIDDEN, D_MODEL), BF16),
            pltpu.SMEM((1,), jnp.int32),
            pltpu.SMEM((n_rows,), jnp.int32),
        ],
    )
    return pl.pallas_call(
        functools.partial(_moe_kernel, n_blocks=n_blocks),
        out_shape=jax.ShapeDtypeStruct((n_rows, half), jnp.uint32),
        grid_spec=grid_spec,
        compiler_params=_cparams(("arbitrary",)),
        name="moe",
    )(dest, block_expert, next_expert, pad_bounds, n_used, h_packed, wg, wu, wd)


FINAL_TM = 256


def _final_kernel(dest_ref, x1_ref, info_ref, gain_ref, rows_hbm, y_ref, rbuf, sem, *, tok_offset):
    i = pl.program_id(0)
    n = pl.num_programs(0)
    tm = FINAL_TM
    slot = i % 2

    def issue(blk, s):
        base = 2 * (tok_offset + blk * tm)
        for r in range(tm):
            for kk in range(2):
                pltpu.make_async_copy(rows_hbm.at[pl.ds(dest_ref[base + 2 * r + kk], 1)],
                                      rbuf.at[s, kk, pl.ds(r, 1)], sem.at[s]).start(priority=kk)

    @pl.when(i == 0)
    def _():
        issue(0, 0)

    @pl.when(i + 1 < n)
    def _():
        issue(i + 1, 1 - slot)

    for kk in range(2):
        pltpu.make_async_copy(rows_hbm.at[pl.ds(0, tm)], rbuf.at[slot, kk], sem.at[slot]).wait()
    info = info_ref[...]
    x = (x1_ref[...] + info[:, 2:3] * _unpack_bf16_pairs(rbuf[slot, 0])
         + info[:, 3:4] * _unpack_bf16_pairs(rbuf[slot, 1]))
    y_ref[...] = x * lax.rsqrt(jnp.mean(x * x, axis=-1, keepdims=True) + NORM_EPS) * gain_ref[...]


def _final(dest, x1_all, info, gain, out_rows, *, tok_offset, n_tok):
    tm = FINAL_TM
    off = tok_offset // tm
    grid_spec = pltpu.PrefetchScalarGridSpec(
        num_scalar_prefetch=1,
        grid=(n_tok // tm,),
        in_specs=[
            pl.BlockSpec((tm, D_MODEL), lambda i, d: (i + off, 0)),
            pl.BlockSpec((tm, LANES), lambda i, d: (i + off, 0)),
            pl.BlockSpec((1, D_MODEL), lambda i, d: (0, 0)),
            pl.BlockSpec(memory_space=pl.ANY),
        ],
        out_specs=pl.BlockSpec((tm, D_MODEL), lambda i, d: (i, 0)),
        scratch_shapes=[pltpu.VMEM((2, 2, tm, D_MODEL // 2), jnp.uint32), pltpu.SemaphoreType.DMA((2,))],
    )
    return pl.pallas_call(
        functools.partial(_final_kernel, tok_offset=tok_offset),
        out_shape=jax.ShapeDtypeStruct((n_tok, D_MODEL), F32),
        grid_spec=grid_spec,
        compiler_params=_cparams(("arbitrary",)),
        name="final",
    )(dest, x1_all, info, gain, out_rows)


def _rot_table(pos, n_rows):
    half = RET_KEY_DIM // 2
    theta = 10000.0 ** (-jnp.linspace(0.0, 1.0, half, dtype=F32))
    ang = pos.astype(F32)[:, None] * theta[None, :]
    cos, sin = jnp.cos(ang), jnp.sin(ang)
    cos128 = jnp.tile(cos, (1, 4))
    sin128 = jnp.tile(jnp.concatenate([-sin, sin], axis=1), (1, 2))
    k_scale = RET_KEY_DIM ** -0.5
    tab = jnp.concatenate([cos128, sin128, cos128 * k_scale, sin128 * k_scale], axis=1)
    return jnp.tile(tab, (n_rows // tab.shape[0], 1))


def kernel(x_prompt, x_sample, cache_diff_k, cache_diff_v, state_retention, norm_mix_gain, w_in, lambda_q1,
           lambda_k1, lambda_q2, lambda_k2, diff_subln_gain, w_ret_out, w_diff_out, w_out, rel_bias_table,
           norm_ffn_gain, w_group_router, b_group_router, w_expert_router, b_expert_router, w_expert_gate,
           w_expert_up, w_expert_down, norm_final_gain):
    B, S, D = x_prompt.shape
    BS, L, _ = x_sample.shape
    past = cache_diff_k.shape[2]
    TP, TS = B * S, BS * L
    T = TP + TS

    w_in_bf = w_in[0].astype(BF16)
    wr_bf = w_ret_out[0].astype(BF16)
    wd_bf = w_diff_out[0].astype(BF16)
    wo_bf = w_out[0].astype(BF16)
    gain_mix = norm_mix_gain[0][None, :]
    gain_ffn = norm_ffn_gain[0][None, :]
    gain_fin = norm_final_gain[None, :]
    gain_sub = diff_subln_gain[0][None, :]
    lam_rows = jnp.zeros((8, LANES), F32).at[0:4, 0:DIFF_HEAD_DIM].set(
        jnp.stack([lambda_q1[0], lambda_k1[0], lambda_q2[0], lambda_k2[0]]))
    rot_p = _rot_table(jnp.arange(S), S)
    rot_s = _rot_table(past + jnp.arange(L), PROJ_TM)

    far = _far_bias(rel_bias_table)[:, None, None]
    qp = ATT_TQ + jnp.arange(ATT_TQ)
    def tile_bias(k_pos):
        bias = (_relative_bias(qp, k_pos, rel_bias_table) - far) * LOG2E
        visible = (k_pos[None, :] // CHUNK) <= (qp[:, None] // CHUNK)
        bias = jnp.swapaxes(jnp.where(visible[None], bias, NEG_INF), 1, 2)
        return jnp.concatenate([bias, bias], axis=2)
    bias_prev = tile_bias(jnp.arange(ATT_TK))
    bias_diag = tile_bias(ATT_TQ + jnp.arange(ATT_TK))
    q_pos_s = past + jnp.arange(L)
    b_last = (_relative_bias(q_pos_s, past - SAMPLE_TK + jnp.arange(SAMPLE_TK), rel_bias_table) - far) * LOG2E
    b_last = jnp.concatenate([b_last, b_last], axis=1)
    bias_past = jnp.stack([jnp.zeros_like(b_last), b_last])
    b_new = (_relative_bias(q_pos_s, q_pos_s, rel_bias_table) - far) * LOG2E
    bias_new = jnp.concatenate([b_new, b_new], axis=1)

    xp2 = x_prompt.reshape(TP, D)
    xs2 = x_sample.reshape(TS, D)
    z3p, gp, gatesp, v32p, qt, vt5, kt32 = _proj(xp2, gain_mix, w_in_bf, rot_p, seq=S, transposed_k=True)
    z3s, gs, gatess, v32s, k32s = _proj(xs2, gain_mix, w_in_bf, rot_s, seq=L, transposed_k=False)

    zero_state = jnp.zeros((B, N_HEADS, RET_KEY_DIM, HEAD_V), F32)
    orp, ret_p = _retention(z3p, zero_state, batch=B, seq=S, chunk=256)
    ors, ret_s = _retention(z3s, state_retention[0], batch=BS, seq=L, chunk=L)

    gain_col = jnp.broadcast_to(diff_subln_gain[0][:, None], (HEAD_V, ATT_TQ))
    odp = _attn_prompt(z3p, qt, vt5, bias_prev, bias_diag, lam_rows, gain_col, batch=B, seq=S)
    kc = jnp.transpose(cache_diff_k[0], (0, 2, 3, 4, 1)).reshape(BS, N_HEADS, LANES, past)
    vc = cache_diff_v[0].reshape(BS, past * N_HEADS, LANES)
    ods = _attn_sample(z3s, kc, vc, bias_past, bias_new, lam_rows, gain_sub, batch=BS, n_q=L, past=past)

    mgp = _post_a(orp, gp, odp, gatesp, wr_bf, wd_bf)
    mgs = _post_a(ors, gs, ods, gatess, wr_bf, wd_bf)

    w_rt = jnp.zeros((D, LANES), F32)
    w_rt = w_rt.at[:, 0:N_GROUPS].set(w_group_router[0])
    w_rt = w_rt.at[:, N_GROUPS:N_GROUPS + N_EXPERTS].set(
        jnp.transpose(w_expert_router[0], (1, 0, 2)).reshape(D, N_EXPERTS))
    w_rt_hi = w_rt.astype(BF16)
    w_rt_lo = (w_rt - w_rt_hi.astype(F32)).astype(BF16)
    w_router2 = jnp.concatenate([w_rt_hi, w_rt_lo], axis=1)
    bias_row = jnp.zeros((1, LANES), F32)
    bias_row = bias_row.at[0, 0:N_GROUPS].set(b_group_router[0])
    bias_row = bias_row.at[0, N_GROUPS:N_GROUPS + N_EXPERTS].set(b_expert_router[0].reshape(-1))

    x1_all, lg_all, h_packed = _post_b(xp2, mgp, xs2, mgs, wo_bf, gain_ffn, w_router2)

    info, counts = _route(lg_all, bias_row)
    counts = counts[0, :N_EXPERTS].astype(jnp.int32)
    padded = (counts + EXPERT_BLOCK - 1) // EXPERT_BLOCK * EXPERT_BLOCK
    pad_end = jnp.cumsum(padded)
    offs = pad_end - padded
    e12 = info[:, 0:2].astype(jnp.int32)
    rank12 = info[:, 4:6].astype(jnp.int32)
    expert_ids = jnp.arange(N_EXPERTS, dtype=jnp.int32)
    offs_e = jnp.sum(jnp.where(e12[:, :, None] == expert_ids, offs, 0), axis=-1)
    dest = (offs_e + rank12).reshape(-1)
    n_assign = 2 * T
    n_rows = -(-n_assign // EXPERT_BLOCK) * EXPERT_BLOCK + N_EXPERTS * EXPERT_BLOCK
    n_blocks = n_rows // EXPERT_BLOCK
    block_start = jnp.arange(n_blocks, dtype=jnp.int32) * EXPERT_BLOCK
    block_expert = jnp.minimum(jnp.sum((pad_end[None, :] <= block_start[:, None]).astype(jnp.int32), axis=1),
                               N_EXPERTS - 1)
    n_used = (pad_end[-1:] // EXPERT_BLOCK).astype(jnp.int32)
    later_used = (expert_ids[None, :] > expert_ids[:, None]) & (counts[None, :] > 0)
    next_of_expert = jnp.min(jnp.where(later_used, expert_ids[None, :], N_EXPERTS), axis=1)
    next_of_expert = jnp.where(next_of_expert == N_EXPERTS, -1, next_of_expert)
    next_expert = jnp.sum(jnp.where(block_expert[:, None] == expert_ids[None, :], next_of_expert[None, :], 0),
                          axis=1).astype(jnp.int32)

    pad_bounds = jnp.concatenate([offs + counts, pad_end]).astype(jnp.int32)
    out_rows = _moe(dest, block_expert, next_expert, pad_bounds, n_used, h_packed,
                    w_expert_gate[0], w_expert_up[0], w_expert_down[0])

    y_p = _final(dest, x1_all, info, gain_fin, out_rows, tok_offset=0, n_tok=TP)
    y_s = _final(dest, x1_all, info, gain_fin, out_rows, tok_offset=TP, n_tok=TS)

    new_k_p = jnp.transpose(kt32, (0, 1, 5, 2, 3, 4))
    new_v_p = v32p.reshape(1, B, S, N_HEADS, HEAD_V)
    new_k_s = k32s.reshape(1, BS, L, N_HEADS, 2, DIFF_HEAD_DIM)
    new_v_s = v32s.reshape(1, BS, L, N_HEADS, HEAD_V)
    return (y_p.reshape(B, S, D), y_s.reshape(BS, L, D), new_k_p, new_v_p, ret_p[None],
            new_k_s, new_v_s, ret_s[None])
```

```python
import functools
import math

import jax
import jax.numpy as jnp
from jax import lax
from jax.experimental import pallas as pl
from jax.experimental.pallas import tpu as pltpu

F32 = jnp.float32
BF16 = jnp.bfloat16

D_MODEL = 2048
CHUNK = 64
N_HEADS = 8
RET_KEY_DIM = 64
HEAD_V = 128
DIFF_HEAD_DIM = 64
IN_WIDTH = 10240
NUM_BUCKETS = 32
MAX_DISTANCE = 128
N_GROUPS = 8
EXPERTS_PER_GROUP = 8
N_EXPERTS = 64
EXPERT_HIDDEN = 512
EXPERT_BLOCK = 128
NORM_EPS = 1e-6
NEG_INF = -1e30
LANES = 128
LOG2E = 1.4426950408889634
VT_ROWS = HEAD_V + 16

VMEM_LIMIT = 56 * 1024 * 1024


def _cparams(sem):
    return pltpu.CompilerParams(dimension_semantics=sem, vmem_limit_bytes=VMEM_LIMIT)


PROJ_TM = 512
PROJ_TN = 1024
PROJ_MC = 128
ATT_TK = 256


def _rotate_pairs(acc, cos, sin):
    outs = []
    lane = lax.broadcasted_iota(jnp.int32, (acc.shape[0], LANES), 1)
    first_half = (lane % 64) < 32
    for c in range(acc.shape[1] // LANES):
        xs = acc[:, c * LANES:(c + 1) * LANES]
        swapped = jnp.where(first_half, pltpu.roll(xs, 96, axis=1), pltpu.roll(xs, 32, axis=1))
        outs.append(xs * cos + swapped * sin)
    return outs


def _proj_kernel(x_ref, gain_ref, w_ref, rot_ref, *out_refs, transposed_k):
    if transposed_k:
        z_ref, g_ref, gates_ref, v32_ref, qt_ref, vt_ref, kt32_ref, h_scr = out_refs
    else:
        z_ref, g_ref, gates_ref, v32_ref, k32_ref, h_scr = out_refs
    j = pl.program_id(1)

    @pl.when(j == 0)
    def _():
        x = x_ref[...]
        ms = jnp.mean(x * x, axis=-1, keepdims=True)
        h_scr[...] = (x * lax.rsqrt(ms + NORM_EPS) * gain_ref[...]).astype(BF16)

    n_slab = PROJ_TN // LANES

    def for_row_chunks(epilogue):
        for c in range(PROJ_TM // PROJ_MC):
            rows = slice(c * PROJ_MC, (c + 1) * PROJ_MC)
            epilogue(c, rows, jnp.dot(h_scr[rows, :], w_ref[...], preferred_element_type=F32))

    def store_slabs(rows, vals):
        for s in range(n_slab):
            z_ref[s, rows, :] = vals[s].astype(BF16)

    def split(a):
        return [a[:, s * LANES:(s + 1) * LANES] for s in range(n_slab)]

    @pl.when(j == 0)
    def _():
        def epilogue(c, rows, acc):
            half = PROJ_TN // 2
            q = _rotate_pairs(acc[:, :half], rot_ref[rows, 0:128], rot_ref[rows, 128:256])
            k = _rotate_pairs(acc[:, half:], rot_ref[rows, 256:384], rot_ref[rows, 384:512])
            store_slabs(rows, q + k)
        for_row_chunks(epilogue)

    @pl.when(j == 1)
    def _():
        for_row_chunks(lambda c, rows, acc: store_slabs(rows, split(acc)))

    @pl.when(j == 2)
    def _():
        def epilogue(c, rows, acc):
            g_ref[rows, :] = (acc * jax.nn.sigmoid(acc)).astype(BF16)
        for_row_chunks(epilogue)

    @pl.when(j == 3)
    def _():
        def epilogue(c, rows, acc):
            qs = acc * (DIFF_HEAD_DIM ** -0.5 * LOG2E)
            store_slabs(rows, split(qs))
            if transposed_k:
                qs_t = qs.T.astype(BF16)
                for hh in range(n_slab):
                    qt_ref[0, hh, :, rows] = qs_t[hh * LANES:(hh + 1) * LANES, :]
        for_row_chunks(epilogue)

    @pl.when(j == 4)
    def _():
        def epilogue(c, rows, acc):
            store_slabs(rows, split(acc))
            if transposed_k:
                kt32_ref[0, 0, :, :, :, rows] = acc.T.reshape(n_slab, 2, DIFF_HEAD_DIM, PROJ_MC)
            else:
                k32_ref[rows, :] = acc
        for_row_chunks(epilogue)

    @pl.when(j == 5)
    def _():
        def epilogue(c, rows, acc):
            store_slabs(rows, split(acc))
            v32_ref[rows, :] = acc
            if transposed_k:
                acc_tb = acc.T.astype(BF16)
                t, cols = (c * PROJ_MC) // ATT_TK, slice((c * PROJ_MC) % ATT_TK, (c * PROJ_MC) % ATT_TK + PROJ_MC)
                ones = jnp.ones((VT_ROWS - HEAD_V, PROJ_MC), BF16)
                for hh in range(n_slab):
                    vt_ref[0, hh, t, 0:HEAD_V, cols] = acc_tb[hh * LANES:(hh + 1) * LANES, :]
                    vt_ref[0, hh, t, HEAD_V:VT_ROWS, cols] = ones
        for_row_chunks(epilogue)

    @pl.when(j >= 6)
    def _():
        def epilogue(c, rows, acc):
            gates_ref[rows, :] = jax.nn.sigmoid(acc).astype(BF16)
        for_row_chunks(epilogue)


def _proj(x2d, gain, w_bf, rot, *, seq, transposed_k):
    T = x2d.shape[0]
    tm, tn = PROJ_TM, PROJ_TN
    ni, nj = T // tm, IN_WIDTH // tn
    rot_blocks = rot.shape[0] // tm

    def zmap(i, j):
        zj = jnp.where(j < 2, j, jnp.where(j < 3, 1, jnp.where(j < 6, j - 1, 4)))
        return (zj, i, 0)

    out_shape = [
        jax.ShapeDtypeStruct((40, T, LANES), BF16),
        jax.ShapeDtypeStruct((T, 1024), BF16),
        jax.ShapeDtypeStruct((T, 4096), BF16),
        jax.ShapeDtypeStruct((T, 1024), F32),
    ]
    out_specs = [
        pl.BlockSpec((8, tm, LANES), zmap),
        pl.BlockSpec((tm, tn), lambda i, j: (i, 0)),
        pl.BlockSpec((tm, tn), lambda i, j: (i, jnp.clip(j - 6, 0, 3))),
        pl.BlockSpec((tm, tn), lambda i, j: (i, 0)),
    ]
    if transposed_k:
        B = T // seq
        spb = seq // tm
        out_shape += [
            jax.ShapeDtypeStruct((B, N_HEADS, LANES, seq), BF16),
            jax.ShapeDtypeStruct((B, N_HEADS, seq // ATT_TK, VT_ROWS, ATT_TK), BF16),
            jax.ShapeDtypeStruct((1, B, N_HEADS, 2, DIFF_HEAD_DIM, seq), F32),
        ]
        out_specs += [
            pl.BlockSpec((1, N_HEADS, LANES, tm), lambda i, j: (i // spb, 0, 0, i % spb)),
            pl.BlockSpec((1, N_HEADS, tm // ATT_TK, VT_ROWS, ATT_TK), lambda i, j: (i // spb, 0, i % spb, 0, 0)),
            pl.BlockSpec((1, 1, N_HEADS, 2, DIFF_HEAD_DIM, tm), lambda i, j: (0, i // spb, 0, 0, 0, i % spb)),
        ]
    else:
        out_shape += [jax.ShapeDtypeStruct((T, 1024), F32)]
        out_specs += [pl.BlockSpec((tm, tn), lambda i, j: (i, 0))]

    return pl.pallas_call(
        functools.partial(_proj_kernel, transposed_k=transposed_k),
        out_shape=out_shape,
        grid=(ni, nj),
        in_specs=[
            pl.BlockSpec((tm, D_MODEL), lambda i, j: (i, 0)),
            pl.BlockSpec((1, D_MODEL), lambda i, j: (0, 0)),
            pl.BlockSpec((D_MODEL, tn), lambda i, j: (0, j)),
            pl.BlockSpec((tm, 512), lambda i, j: (i % rot_blocks, 0)),
        ],
        out_specs=out_specs,
        scratch_shapes=[pltpu.VMEM((tm, D_MODEL), BF16)],
        compiler_params=_cparams(("arbitrary", "arbitrary")),
        name="proj_t" if transposed_k else "proj",
    )(x2d, gain, w_bf, rot)


def _retention_kernel(q_ref, k_ref, v_ref, decay_ref, inner_ref, outer_ref, st0_ref,
                      o_ref, st_out_ref, st_scr, *, chunk):
    c = pl.program_id(1)
    nc = pl.num_programs(1)

    @pl.when(c == 0)
    def _():
        st_scr[...] = jnp.zeros_like(st_scr)
        for h in range(N_HEADS):
            a = h % 2
            st_scr[h, a * RET_KEY_DIM:(a + 1) * RET_KEY_DIM, :] = st0_ref[0, h]

    lane_lo = lax.broadcasted_iota(jnp.int32, (chunk, LANES), 1) < RET_KEY_DIM
    row_lo = lax.broadcasted_iota(jnp.int32, (LANES, LANES), 0) < RET_KEY_DIM
    for h in range(N_HEADS):
        p, a = h // 2, h % 2
        q = q_ref[p]
        k = k_ref[p]
        v = v_ref[h]
        qa = jnp.where(lane_lo == (a == 0), q, jnp.zeros_like(q))
        s = lax.dot_general(qa, k, (((1,), (1,)), ((), ())), preferred_element_type=F32)
        s = s * decay_ref[h]
        st = st_scr[h]
        inner = inner_ref[h]
        o = (jnp.dot(s.astype(BF16), v, preferred_element_type=F32)
             + jnp.dot(qa, st.astype(BF16), preferred_element_type=F32) * inner)
        ko = (k.astype(F32) * outer_ref[p]).astype(BF16)
        upd = lax.dot_general(ko, v, (((0,), (0,)), ((), ())), preferred_element_type=F32)
        upd = jnp.where(row_lo == (a == 0), upd, 0.0)
        chunk_decay = inner[chunk - 1:chunk, :]
        st_scr[h] = chunk_decay * st + upd
        o_n = o * lax.rsqrt(jnp.mean(o * o, axis=-1, keepdims=True) + NORM_EPS)
        o_ref[:, h * HEAD_V:(h + 1) * HEAD_V] = o_n.astype(BF16)

    @pl.when(c == nc - 1)
    def _():
        for h in range(N_HEADS):
            a = h % 2
            st_out_ref[0, h] = st_scr[h, a * RET_KEY_DIM:(a + 1) * RET_KEY_DIM, :]


def _retention_consts(chunk):
    log_g = jnp.log(1.0 - 2.0 ** (-5.0 - jnp.arange(N_HEADS, dtype=F32)))
    j = jnp.arange(chunk, dtype=F32)
    diff = j[:, None] - j[None, :]
    decay = jnp.where(diff >= 0, jnp.exp(log_g[:, None, None] * jnp.maximum(diff, 0.0)), 0.0)
    inner = jnp.exp(log_g[:, None] * (j + 1.0))
    outer = jnp.exp(log_g[:, None] * (chunk - 1.0 - j))
    inner_b = jnp.broadcast_to(inner[:, :, None], (N_HEADS, chunk, LANES))
    outer_pair = jnp.repeat(outer.reshape(N_HEADS // 2, 2, chunk).transpose(0, 2, 1), RET_KEY_DIM, axis=2)
    return decay, inner_b, outer_pair


def _retention(z3, state0, *, batch, seq, chunk):
    T = batch * seq
    nc = seq // chunk
    decay, inner_b, outer_pair = _retention_consts(chunk)
    return pl.pallas_call(
        functools.partial(_retention_kernel, chunk=chunk),
        out_shape=[jax.ShapeDtypeStruct((T, 1024), BF16),
                   jax.ShapeDtypeStruct((batch, N_HEADS, RET_KEY_DIM, HEAD_V), F32)],
        grid=(batch, nc),
        in_specs=[
            pl.BlockSpec((4, chunk, LANES), lambda b, c: (0, b * nc + c, 0)),
            pl.BlockSpec((4, chunk, LANES), lambda b, c: (1, b * nc + c, 0)),
            pl.BlockSpec((8, chunk, LANES), lambda b, c: (1, b * nc + c, 0)),
            pl.BlockSpec((N_HEADS, chunk, chunk), lambda b, c: (0, 0, 0)),
            pl.BlockSpec((N_HEADS, chunk, LANES), lambda b, c: (0, 0, 0)),
            pl.BlockSpec((N_HEADS // 2, chunk, LANES), lambda b, c: (0, 0, 0)),
            pl.BlockSpec((1, N_HEADS, RET_KEY_DIM, HEAD_V), lambda b, c: (b, 0, 0, 0)),
        ],
        out_specs=[
            pl.BlockSpec((chunk, 1024), lambda b, c: (b * nc + c, 0)),
            pl.BlockSpec((1, N_HEADS, RET_KEY_DIM, HEAD_V), lambda b, c: (b, 0, 0, 0)),
        ],
        scratch_shapes=[pltpu.VMEM((N_HEADS, LANES, LANES), F32)],
        compiler_params=_cparams(("arbitrary", "arbitrary")),
        name=f"retention_c{chunk}",
    )(z3, z3, z3, decay, inner_b, outer_pair, state0)


def _lam_init(layer=0):
    return 0.8 - 0.6 * math.exp(-0.3 * layer)


def _lam_from_ref(lam_ref):
    lp = lam_ref[...]
    s1 = jnp.sum(lp[0:1] * lp[1:2], axis=-1, keepdims=True)
    s2 = jnp.sum(lp[2:3] * lp[3:4], axis=-1, keepdims=True)
    return jnp.exp(s1) - jnp.exp(s2) + _lam_init()


def _stack_q(q):
    lane_lo = lax.broadcasted_iota(jnp.int32, q.shape, 1) < DIFF_HEAD_DIM
    zero = jnp.zeros_like(q)
    return jnp.concatenate([jnp.where(lane_lo, q, zero), jnp.where(lane_lo, zero, q)], axis=0)


def _with_ones(v):
    return jnp.concatenate([v, jnp.ones_like(v)], axis=1)


def _diff_finish(acc_ref, lam, gain, n):
    o = acc_ref[:, 0:HEAD_V] / acc_ref[:, HEAD_V:2 * HEAD_V]
    o = o[:n] - lam * o[n:]
    o = o * lax.rsqrt(jnp.mean(o * o, axis=-1, keepdims=True) + NORM_EPS) * gain
    return o * (1.0 - _lam_init())


def _relative_bias(q_pos, k_pos, table):
    rel = k_pos[None, :] - q_pos[:, None]
    half = NUM_BUCKETS // 2
    max_exact = half // 2
    n = jnp.abs(rel)
    log_ratio = jnp.log(jnp.maximum(n, 1).astype(F32) / max_exact) / math.log(MAX_DISTANCE / max_exact)
    large = jnp.minimum(max_exact + (log_ratio * (half - max_exact)).astype(jnp.int32), half - 1)
    bucket = (rel > 0).astype(jnp.int32) * half + jnp.where(n < max_exact, n, large)
    out = jnp.zeros((table.shape[1],) + bucket.shape, F32)
    for bkt in range(NUM_BUCKETS):
        out = jnp.where(bucket[None] == bkt, table[bkt].astype(F32)[:, None, None], out)
    return out


def _far_bias(table):
    return table[NUM_BUCKETS // 2 - 1].astype(F32)


ATT_TQ = 256
ATT_HP = 8


def _attn_prompt_kernel(qt_ref, k_ref, vt_ref, bprev_ref, bdiag_ref, lam_ref, gain_ref, o_ref, acc_scr):
    i = pl.program_id(2)
    row_lo = lax.broadcasted_iota(jnp.int32, (LANES, ATT_TQ), 0) < DIFF_HEAD_DIM
    qs = []
    for hh in range(ATT_HP):
        qt = qt_ref[0, hh]
        zero = jnp.zeros_like(qt)
        qs.append(jnp.concatenate([jnp.where(row_lo, qt, zero), jnp.where(row_lo, zero, qt)], axis=1))
    acc_scr[...] = jnp.zeros_like(acc_scr)

    def tile(j, ms, biases):
        ss = []
        for hh in range(ATT_HP):
            kt = k_ref[hh, pl.ds(pl.multiple_of(j * ATT_TK, ATT_TK), ATT_TK), :]
            s = jnp.dot(kt, qs[hh], preferred_element_type=F32)
            ss.append(s if biases is None else s + biases[hh])
        ps, alphas, m_out = [], [], []
        for hh in range(ATT_HP):
            m_new = jnp.maximum(ms[hh], jnp.max(ss[hh], axis=0, keepdims=True))
            alphas.append(jnp.exp2(ms[hh] - m_new))
            ps.append(jnp.exp2(ss[hh] - m_new).astype(BF16))
            m_out.append(m_new)
        for hh in range(ATT_HP):
            acc_scr[hh] = alphas[hh] * acc_scr[hh] + jnp.dot(vt_ref[0, hh, j], ps[hh],
                                                             preferred_element_type=F32)
        return tuple(m_out)

    ms = tuple(jnp.full((1, 2 * ATT_TQ), NEG_INF, F32) for _ in range(ATT_HP))
    ms = lax.fori_loop(0, i - 1, lambda j, c: tile(j, c, None), ms)
    jp = jnp.maximum(i - 1, 0)
    ms = lax.cond(i >= 1,
                  lambda c: tile(jp, c, [bprev_ref[hh] for hh in range(ATT_HP)]),
                  lambda c: c, ms)
    tile(i, ms, [bdiag_ref[hh] for hh in range(ATT_HP)])

    lam = _lam_from_ref(lam_ref)
    for hh in range(ATT_HP):
        o = acc_scr[hh, 0:HEAD_V, :] / acc_scr[hh, HEAD_V:HEAD_V + 1, :]
        o = o[:, :ATT_TQ] - lam * o[:, ATT_TQ:]
        o = o * lax.rsqrt(jnp.mean(o * o, axis=0, keepdims=True) + NORM_EPS) * gain_ref[...]
        o_ref[:, hh * HEAD_V:(hh + 1) * HEAD_V] = (o * (1.0 - _lam_init())).T.astype(BF16)


def _attn_prompt(z3, qt, vt5, bias_prev, bias_diag, lam_rows, gain_col, *, batch, seq):
    T = batch * seq
    nq = seq // ATT_TQ
    hp = ATT_HP
    return pl.pallas_call(
        _attn_prompt_kernel,
        out_shape=jax.ShapeDtypeStruct((T, 1024), BF16),
        grid=(batch, N_HEADS // hp, nq),
        in_specs=[
            pl.BlockSpec((1, hp, LANES, ATT_TQ), lambda b, g, i: (b, g, 0, i)),
            pl.BlockSpec((hp, seq, LANES), lambda b, g, i: (24 // hp + g, b, 0), pipeline_mode=pl.Buffered(1)),
            pl.BlockSpec((1, hp, seq // ATT_TK, VT_ROWS, ATT_TK), lambda b, g, i: (b, g, 0, 0, 0),
                         pipeline_mode=pl.Buffered(1)),
            pl.BlockSpec((hp, ATT_TK, 2 * ATT_TQ), lambda b, g, i: (g, 0, 0), pipeline_mode=pl.Buffered(1)),
            pl.BlockSpec((hp, ATT_TK, 2 * ATT_TQ), lambda b, g, i: (g, 0, 0), pipeline_mode=pl.Buffered(1)),
            pl.BlockSpec((8, LANES), lambda b, g, i: (0, 0)),
            pl.BlockSpec((LANES, ATT_TQ), lambda b, g, i: (0, 0)),
        ],
        out_specs=pl.BlockSpec((ATT_TQ, hp * HEAD_V), lambda b, g, i: (b * nq + i, g)),
        scratch_shapes=[pltpu.VMEM((hp, VT_ROWS, 2 * ATT_TQ), F32)],
        compiler_params=_cparams(("arbitrary", "arbitrary", "arbitrary")),
        name="attn_prompt",
    )(qt, z3, vt5, bias_prev, bias_diag, lam_rows, gain_col)


SAMPLE_TK = 1024
SAMPLE_NEAR = 256


def _attn_sample_kernel(q_ref, kc_ref, vc_ref, kn_ref, vn_ref, bpast_ref, bnew_ref, lam_ref, gain_ref,
                        o_ref, m_scr, acc_scr, *, n_past_tiles, n_q):
    t = pl.program_id(1)

    @pl.when(t == 0)
    def _():
        m_scr[...] = jnp.full_like(m_scr, NEG_INF)
        acc_scr[...] = jnp.zeros_like(acc_scr)

    def update(logits, values):
        ps, alphas = [], []
        for h in range(N_HEADS):
            m_prev = m_scr[h]
            m_new = jnp.maximum(m_prev, jnp.max(logits[h], axis=1, keepdims=True))
            alpha = jnp.exp2(m_prev - m_new)
            ps.append(jnp.exp2(logits[h] - m_new[:, 0:1]).astype(BF16))
            alphas.append(jnp.concatenate([alpha, alpha], axis=1))
            m_scr[h] = m_new
        for h in range(N_HEADS):
            acc_scr[h] = alphas[h] * acc_scr[h] + jnp.dot(ps[h], _with_ones(values[h]()),
                                                          preferred_element_type=F32)

    @pl.when(t < n_past_tiles)
    def _():
        is_last = jnp.where(t == n_past_tiles - 1, 1.0, 0.0)
        far_w = SAMPLE_TK - SAMPLE_NEAR
        logits = []
        for h in range(N_HEADS):
            kt = kc_ref[0, h].astype(BF16)
            s = jnp.dot(_stack_q(q_ref[h]), kt, preferred_element_type=F32)
            logits.append(jnp.concatenate([s[:, :far_w], s[:, far_w:] + is_last * bpast_ref[h]], axis=1))
        update(logits, [lambda h=h: vc_ref[0, pl.ds(h, SAMPLE_TK, stride=N_HEADS), :].astype(BF16)
                        for h in range(N_HEADS)])

    @pl.when(t == n_past_tiles)
    def _():
        logits = []
        for h in range(N_HEADS):
            s = lax.dot_general(_stack_q(q_ref[h]), kn_ref[h], (((1,), (1,)), ((), ())),
                                preferred_element_type=F32)
            logits.append(s + bnew_ref[h])
        update(logits, [lambda h=h: vn_ref[h] for h in range(N_HEADS)])
        lam = _lam_from_ref(lam_ref)
        for h in range(N_HEADS):
            o = _diff_finish(acc_scr.at[h], lam, gain_ref[...], n_q)
            o_ref[:, h * HEAD_V:(h + 1) * HEAD_V] = o.astype(BF16)


def _attn_sample(z3, kc, vc, bias_past, bias_new, lam_rows, gain, *, batch, n_q, past):
    npt = past // SAMPLE_TK
    return pl.pallas_call(
        functools.partial(_attn_sample_kernel, n_past_tiles=npt, n_q=n_q),
        out_shape=jax.ShapeDtypeStruct((batch * n_q, 1024), BF16),
        grid=(batch, npt + 1),
        in_specs=[
            pl.BlockSpec((8, n_q, LANES), lambda b, t: (2, b, 0)),
            pl.BlockSpec((1, N_HEADS, LANES, SAMPLE_TK), lambda b, t: (b, 0, 0, jnp.minimum(t, npt - 1))),
            pl.BlockSpec((1, SAMPLE_TK * N_HEADS, LANES), lambda b, t: (b, jnp.minimum(t, npt - 1), 0)),
            pl.BlockSpec((8, n_q, LANES), lambda b, t: (3, b, 0)),
            pl.BlockSpec((8, n_q, LANES), lambda b, t: (4, b, 0)),
            pl.BlockSpec((N_HEADS, 2 * n_q, SAMPLE_NEAR), lambda b, t: (0, 0, 0)),
            pl.BlockSpec((N_HEADS, 2 * n_q, n_q), lambda b, t: (0, 0, 0)),
            pl.BlockSpec((8, LANES), lambda b, t: (0, 0)),
            pl.BlockSpec((1, LANES), lambda b, t: (0, 0)),
        ],
        out_specs=pl.BlockSpec((n_q, 1024), lambda b, t: (b, 0)),
        scratch_shapes=[pltpu.VMEM((N_HEADS, 2 * n_q, LANES), F32),
                        pltpu.VMEM((N_HEADS, 2 * n_q, 2 * LANES), F32)],
        compiler_params=_cparams(("arbitrary", "arbitrary")),
        name="attn_sample",
    )(z3, kc, vc, z3, z3, bias_past, bias_new, lam_rows, gain)


POST_TM = 256


def _post_a_kernel(or_ref, g_ref, od_ref, gr_ref, gd_ref, wr_ref, wd_ref, out_ref):
    a = jnp.dot(or_ref[...] * g_ref[...], wr_ref[...], preferred_element_type=F32)
    b = jnp.dot(od_ref[...], wd_ref[...], preferred_element_type=F32)
    out_ref[...] = (gr_ref[...].astype(F32) * a + gd_ref[...].astype(F32) * b).astype(BF16)


def _post_a(o_r, g, o_d, gates, wr_bf, wd_bf):
    T = o_r.shape[0]
    tm = POST_TM
    return pl.pallas_call(
        _post_a_kernel,
        out_shape=jax.ShapeDtypeStruct((T, D_MODEL), BF16),
        grid=(T // tm,),
        in_specs=[
            pl.BlockSpec((tm, 1024), lambda i: (i, 0)),
            pl.BlockSpec((tm, 1024), lambda i: (i, 0)),
            pl.BlockSpec((tm, 1024), lambda i: (i, 0)),
            pl.BlockSpec((tm, D_MODEL), lambda i: (i, 0)),
            pl.BlockSpec((tm, D_MODEL), lambda i: (i, 1)),
            pl.BlockSpec((1024, D_MODEL), lambda i: (0, 0)),
            pl.BlockSpec((1024, D_MODEL), lambda i: (0, 0)),
        ],
        out_specs=pl.BlockSpec((tm, D_MODEL), lambda i: (i, 0)),
        compiler_params=_cparams(("arbitrary",)),
        name="post_a",
    )(o_r, g, o_d, gates, gates, wr_bf, wd_bf)


def _pack_bf16_pairs(x):
    w = x.shape[1] // 2
    xb = x.astype(BF16).astype(F32)
    lo = pltpu.bitcast(xb[:, :w], jnp.uint32) >> 16
    hi = pltpu.bitcast(xb[:, w:], jnp.uint32) & jnp.uint32(0xFFFF0000)
    return lo | hi


def _unpack_bf16_pairs(words):
    lo = pltpu.bitcast(words << 16, F32)
    hi = pltpu.bitcast(words & jnp.uint32(0xFFFF0000), F32)
    return jnp.concatenate([lo, hi], axis=1)


def _post_b_kernel(xp_ref, mp_ref, xs_ref, ms_ref, wo_ref, gain_ref, wrt_ref, x1_ref, lg_ref, hp_ref, *,
                   n_prompt_blocks):
    i = pl.program_id(0)
    tm = POST_TM

    def body(x_ref, mg_ref):
        x1 = x_ref[...] + jnp.dot(mg_ref[...], wo_ref[...], preferred_element_type=F32)
        x1_ref[...] = x1
        h2 = x1 * lax.rsqrt(jnp.mean(x1 * x1, axis=-1, keepdims=True) + NORM_EPS) * gain_ref[...]
        hp_ref[...] = _pack_bf16_pairs(h2)
        h_hi = h2.astype(BF16)
        h_lo = (h2 - h_hi.astype(F32)).astype(BF16)
        r = jnp.dot(jnp.concatenate([h_hi, h_lo], axis=0), wrt_ref[...], preferred_element_type=F32)
        lg_ref[...] = (r[:tm, :LANES] + r[:tm, LANES:]) + (r[tm:, :LANES] + r[tm:, LANES:])

    @pl.when(i < n_prompt_blocks)
    def _():
        body(xp_ref, mp_ref)

    @pl.when(i >= n_prompt_blocks)
    def _():
        body(xs_ref, ms_ref)


def _post_b(xp2, mgp, xs2, mgs, wo_bf, gain, w_router2):
    tm = POST_TM
    npb, nsb = xp2.shape[0] // tm, xs2.shape[0] // tm
    T = xp2.shape[0] + xs2.shape[0]
    pmap = lambda i: (jnp.minimum(i, npb - 1), 0)
    smap = lambda i: (jnp.maximum(i - npb, 0), 0)
    return pl.pallas_call(
        functools.partial(_post_b_kernel, n_prompt_blocks=npb),
        out_shape=[jax.ShapeDtypeStruct((T, D_MODEL), F32),
                   jax.ShapeDtypeStruct((T, LANES), F32),
                   jax.ShapeDtypeStruct((T, D_MODEL // 2), jnp.uint32)],
        grid=(npb + nsb,),
        in_specs=[
            pl.BlockSpec((tm, D_MODEL), pmap),
            pl.BlockSpec((tm, D_MODEL), pmap),
            pl.BlockSpec((tm, D_MODEL), smap),
            pl.BlockSpec((tm, D_MODEL), smap),
            pl.BlockSpec((D_MODEL, D_MODEL), lambda i: (0, 0)),
            pl.BlockSpec((1, D_MODEL), lambda i: (0, 0)),
            pl.BlockSpec((D_MODEL, 2 * LANES), lambda i: (0, 0)),
        ],
        out_specs=[pl.BlockSpec((tm, D_MODEL), lambda i: (i, 0)),
                   pl.BlockSpec((tm, LANES), lambda i: (i, 0)),
                   pl.BlockSpec((tm, D_MODEL // 2), lambda i: (i, 0))],
        compiler_params=_cparams(("arbitrary",)),
        name="post_b",
    )(xp2, mgp, xs2, mgs, wo_bf, gain, w_router2)


ROUTE_TM = 512


def _route_kernel(lg_ref, bias_ref, info_ref, cnt_ref, tri_scr, carry_scr):
    i = pl.program_id(0)
    tm = ROUTE_TM

    @pl.when(i == 0)
    def _():
        r = lax.broadcasted_iota(jnp.int32, (tm, tm), 0)
        c = lax.broadcasted_iota(jnp.int32, (tm, tm), 1)
        tri_scr[...] = jnp.where(c < r, 1.0, 0.0).astype(BF16)
        carry_scr[...] = jnp.zeros_like(carry_scr)

    lg = lg_ref[...] + bias_ref[...]
    lane = lax.broadcasted_iota(jnp.int32, (tm, LANES), 1)
    lane_f = lane.astype(F32)
    neg = jnp.float32(-jnp.inf)

    def first_argmax(vals):
        top = jnp.max(vals, axis=1, keepdims=True)
        idx = jnp.min(jnp.where(vals == top, lane_f, float(LANES)), axis=1, keepdims=True)
        return top, idx

    is_group = lane < N_GROUPS
    gl = jnp.where(is_group, lg, neg)
    g_top, g_idx = first_argmax(gl)
    g_weight = 1.0 / jnp.sum(jnp.exp(gl - g_top), axis=1, keepdims=True)
    lane_group = ((lane - N_GROUPS) >> 3).astype(F32)
    in_group = (lane >= N_GROUPS) & (lane < N_GROUPS + N_EXPERTS) & (lane_group == g_idx)
    el = jnp.where(in_group, lg, neg)
    t1, i1 = first_argmax(el)
    el2 = jnp.where(lane_f == i1, neg, el)
    t2, i2 = first_argmax(el2)
    e2w = jnp.exp(t2 - t1)
    p1 = 1.0 / (1.0 + e2w)
    gate1 = g_weight * p1
    gate2 = g_weight * (e2w * p1)
    e1 = i1 - float(N_GROUPS)
    e2 = i2 - float(N_GROUPS)

    hot1 = lane_f == e1
    hot2 = lane_f == e2
    both = jnp.where(hot1 | hot2, 1.0, 0.0)
    prefix = jnp.dot(tri_scr[...], both.astype(BF16), preferred_element_type=F32) + carry_scr[...]
    rank1 = jnp.sum(jnp.where(hot1, prefix, 0.0), axis=1, keepdims=True)
    rank2 = jnp.sum(jnp.where(hot2, prefix, 0.0), axis=1, keepdims=True)
    carry_scr[...] = carry_scr[...] + jnp.sum(both, axis=0, keepdims=True)

    info = jnp.where(lane == 0, e1, jnp.where(lane == 1, e2, jnp.where(lane == 2, gate1, jnp.where(
        lane == 3, gate2, jnp.where(lane == 4, rank1, jnp.where(lane == 5, rank2, 0.0))))))
    info_ref[...] = info
    cnt_ref[...] = carry_scr[...]


def _route(lg_all, bias_row):
    T = lg_all.shape[0]
    tm = ROUTE_TM
    return pl.pallas_call(
        _route_kernel,
        out_shape=[jax.ShapeDtypeStruct((T, LANES), F32), jax.ShapeDtypeStruct((1, LANES), F32)],
        grid=(T // tm,),
        in_specs=[pl.BlockSpec((tm, LANES), lambda i: (i, 0)), pl.BlockSpec((1, LANES), lambda i: (0, 0))],
        out_specs=[pl.BlockSpec((tm, LANES), lambda i: (i, 0)), pl.BlockSpec((1, LANES), lambda i: (0, 0))],
        scratch_shapes=[pltpu.VMEM((tm, tm), BF16), pltpu.VMEM((1, LANES), F32)],
        compiler_params=_cparams(("arbitrary",)),
        name="route",
    )(lg_all, bias_row)


WEIGHT_DMA_PRIORITY = 1


def _moe_kernel(dest_ref, bexp_ref, nxt_ref, pad_ref, nused_ref, hp_hbm, wg_hbm, wu_hbm, wd_hbm,
                out_ref, xbuf, gsem, wgbuf, wubuf, wdbuf, wsem, wg_bf, wu_bf, wd_bf, wslot, tok_ref, *,
                n_blocks):
    b = pl.program_id(0)
    n_used = nused_ref[0]
    slot = b % 2

    def gather_copy(row, s, r):
        return pltpu.make_async_copy(hp_hbm.at[pl.ds(row, 1)], xbuf.at[s, pl.ds(r, 1)], gsem.at[s])

    def weight_copies(e, s):
        return (pltpu.make_async_copy(wg_hbm.at[e], wgbuf.at[s], wsem.at[s]),
                pltpu.make_async_copy(wu_hbm.at[e], wubuf.at[s], wsem.at[s]),
                pltpu.make_async_copy(wd_hbm.at[e], wdbuf.at[s], wsem.at[s]))

    def issue(blk, s):
        base = blk * EXPERT_BLOCK

        def body(r, carry):
            gather_copy(tok_ref[base + r], s, r).start()
            return carry

        lax.fori_loop(0, EXPERT_BLOCK, body, 0, unroll=8)

    @pl.when(b == 0)
    def _():
        for cp in weight_copies(bexp_ref[0], 0):
            cp.start(priority=WEIGHT_DMA_PRIORITY)
        wslot[0] = 0

        def zero_rows(lo, hi):
            def zero_body(r, carry):
                tok_ref[r] = 0
                return carry

            lax.fori_loop(lo, hi, zero_body, 0)

        def expert_body(e, carry):
            zero_rows(pad_ref[e], pad_ref[N_EXPERTS + e])
            return carry

        lax.fori_loop(0, N_EXPERTS, expert_body, 0)
        tail = n_used * EXPERT_BLOCK
        zero_rows(tail, jnp.minimum(tail + EXPERT_BLOCK, n_blocks * EXPERT_BLOCK))

        def fill_body(a, carry):
            tok_ref[dest_ref[a]] = a >> 1
            return carry

        lax.fori_loop(0, dest_ref.shape[0], fill_body, 0, unroll=8)
        issue(0, 0)

    @pl.when(b < n_used)
    def _():
        e = bexp_ref[b]
        new_expert = (b == 0) | (e != bexp_ref[jnp.maximum(b - 1, 0)])

        @pl.when(new_expert)
        def _():
            @pl.when(b > 0)
            def _():
                wslot[0] = 1 - wslot[0]

            s = wslot[0]
            for cp in weight_copies(e, s):
                cp.wait()
            nxt = nxt_ref[b]

            @pl.when(nxt >= 0)
            def _():
                for cp in weight_copies(nxt, 1 - s):
                    cp.start(priority=WEIGHT_DMA_PRIORITY)

            wg_bf[...] = wgbuf[s].astype(BF16)
            wu_bf[...] = wubuf[s].astype(BF16)
            wd_bf[...] = wdbuf[s].astype(BF16)

    def expert_block(cur):
        pltpu.make_async_copy(hp_hbm.at[pl.ds(0, EXPERT_BLOCK)], xbuf.at[cur], gsem.at[cur]).wait()
        base = jnp.minimum(b + 1, n_blocks - 1) * EXPERT_BLOCK
        for r in range(EXPERT_BLOCK):
            gather_copy(tok_ref[base + r], 1 - cur, r).start()
        h = _unpack_bf16_pairs(xbuf[cur]).astype(BF16)
        g = jnp.dot(h, wg_bf[...], preferred_element_type=F32)
        u = jnp.dot(h, wu_bf[...], preferred_element_type=F32)
        a = (g * jax.nn.sigmoid(g) * u).astype(BF16)
        out_ref[...] = _pack_bf16_pairs(jnp.dot(a, wd_bf[...], preferred_element_type=F32))

    for cur in range(2):
        pl.when((b < n_used) & (slot == cur))(functools.partial(expert_block, cur))

    @pl.when(b == n_used)
    def _():
        pltpu.make_async_copy(hp_hbm.at[pl.ds(0, EXPERT_BLOCK)], xbuf.at[slot], gsem.at[slot]).wait()

    @pl.when((b >= n_used) & (b < n_blocks))
    def _():
        out_ref[...] = jnp.zeros_like(out_ref)


def _moe(dest, block_expert, next_expert, pad_bounds, n_used, h_packed, wg, wu, wd):
    n_blocks = block_expert.shape[0]
    n_rows = n_blocks * EXPERT_BLOCK
    half = D_MODEL // 2
    grid_spec = pltpu.PrefetchScalarGridSpec(
        num_scalar_prefetch=5,
        grid=(n_blocks + 1,),
        in_specs=[
            pl.BlockSpec(memory_space=pl.ANY),
            pl.BlockSpec(memory_space=pl.ANY),
            pl.BlockSpec(memory_space=pl.ANY),
            pl.BlockSpec(memory_space=pl.ANY),
        ],
        out_specs=pl.BlockSpec((EXPERT_BLOCK, half), lambda b, *_: (jnp.minimum(b, n_blocks - 1), 0)),
        scratch_shapes=[
            pltpu.VMEM((2, EXPERT_BLOCK, half), jnp.uint32),
            pltpu.SemaphoreType.DMA((2,)),
            pltpu.VMEM((2, D_MODEL, EXPERT_HIDDEN), F32),
            pltpu.VMEM((2, D_MODEL, EXPERT_HIDDEN), F32),
            pltpu.VMEM((2, EXPERT_HIDDEN, D_MODEL), F32),
            pltpu.SemaphoreType.DMA((2,)),
            pltpu.VMEM((D_MODEL, EXPERT_HIDDEN), BF16),
            pltpu.VMEM((D_MODEL, EXPERT_HIDDEN), BF16),
            pltpu.VMEM((EXPERT_HIDDEN, D_MODEL), BF16),
            pltpu.SMEM((1,), jnp.int32),
            pltpu.SMEM((n_rows,), jnp.int32),
        ],
    )
    return pl.pallas_call(
        functools.partial(_moe_kernel, n_blocks=n_blocks),
        out_shape=jax.ShapeDtypeStruct((n_rows, half), jnp.uint32),
        grid_spec=grid_spec,
        compiler_params=_cparams(("arbitrary",)),
        name="moe",
    )(dest, block_expert, next_expert, pad_bounds, n_used, h_packed, wg, wu, wd)


FINAL_TM = 256


def _final_kernel(dest_ref, x1_ref, info_ref, gain_ref, rows_hbm, y_ref, rbuf, sem, *, tok_offset):
    i = pl.program_id(0)
    n = pl.num_programs(0)
    tm = FINAL_TM
    slot = i % 2

    def issue(blk, s):
        base = 2 * (tok_offset + blk * tm)
        for r in range(tm):
            for kk in range(2):
                pltpu.make_async_copy(rows_hbm.at[pl.ds(dest_ref[base + 2 * r + kk], 1)],
                                      rbuf.at[s, kk, pl.ds(r, 1)], sem.at[s]).start(priority=kk)

    @pl.when(i == 0)
    def _():
        issue(0, 0)

    @pl.when(i + 1 < n)
    def _():
        issue(i + 1, 1 - slot)

    for kk in range(2):
        pltpu.make_async_copy(rows_hbm.at[pl.ds(0, tm)], rbuf.at[slot, kk], sem.at[slot]).wait()
    info = info_ref[...]
    x = (x1_ref[...] + info[:, 2:3] * _unpack_bf16_pairs(rbuf[slot, 0])
         + info[:, 3:4] * _unpack_bf16_pairs(rbuf[slot, 1]))
    y_ref[...] = x * lax.rsqrt(jnp.mean(x * x, axis=-1, keepdims=True) + NORM_EPS) * gain_ref[...]


def _final(dest, x1_all, info, gain, out_rows, *, tok_offset, n_tok):
    tm = FINAL_TM
    off = tok_offset // tm
    grid_spec = pltpu.PrefetchScalarGridSpec(
        num_scalar_prefetch=1,
        grid=(n_tok // tm,),
        in_specs=[
            pl.BlockSpec((tm, D_MODEL), lambda i, d: (i + off, 0)),
            pl.BlockSpec((tm, LANES), lambda i, d: (i + off, 0)),
            pl.BlockSpec((1, D_MODEL), lambda i, d: (0, 0)),
            pl.BlockSpec(memory_space=pl.ANY),
        ],
        out_specs=pl.BlockSpec((tm, D_MODEL), lambda i, d: (i, 0)),
        scratch_shapes=[pltpu.VMEM((2, 2, tm, D_MODEL // 2), jnp.uint32), pltpu.SemaphoreType.DMA((2,))],
    )
    return pl.pallas_call(
        functools.partial(_final_kernel, tok_offset=tok_offset),
        out_shape=jax.ShapeDtypeStruct((n_tok, D_MODEL), F32),
        grid_spec=grid_spec,
        compiler_params=_cparams(("arbitrary",)),
        name="final",
    )(dest, x1_all, info, gain, out_rows)


def _rot_table(pos, n_rows):
    half = RET_KEY_DIM // 2
    theta = 10000.0 ** (-jnp.linspace(0.0, 1.0, half, dtype=F32))
    ang = pos.astype(F32)[:, None] * theta[None, :]
    cos, sin = jnp.cos(ang), jnp.sin(ang)
    cos128 = jnp.tile(cos, (1, 4))
    sin128 = jnp.tile(jnp.concatenate([-sin, sin], axis=1), (1, 2))
    k_scale = RET_KEY_DIM ** -0.5
    tab = jnp.concatenate([cos128, sin128, cos128 * k_scale, sin128 * k_scale], axis=1)
    return jnp.tile(tab, (n_rows // tab.shape[0], 1))


def kernel(x_prompt, x_sample, cache_diff_k, cache_diff_v, state_retention, norm_mix_gain, w_in, lambda_q1,
           lambda_k1, lambda_q2, lambda_k2, diff_subln_gain, w_ret_out, w_diff_out, w_out, rel_bias_table,
           norm_ffn_gain, w_group_router, b_group_router, w_expert_router, b_expert_router, w_expert_gate,
           w_expert_up, w_expert_down, norm_final_gain):
    B, S, D = x_prompt.shape
    BS, L, _ = x_sample.shape
    past = cache_diff_k.shape[2]
    TP, TS = B * S, BS * L
    T = TP + TS

    w_in_bf = w_in[0].astype(BF16)
    wr_bf = w_ret_out[0].astype(BF16)
    wd_bf = w_diff_out[0].astype(BF16)
    wo_bf = w_out[0].astype(BF16)
    gain_mix = norm_mix_gain[0][None, :]
    gain_ffn = norm_ffn_gain[0][None, :]
    gain_fin = norm_final_gain[None, :]
    gain_sub = diff_subln_gain[0][None, :]
    lam_rows = jnp.zeros((8, LANES), F32).at[0:4, 0:DIFF_HEAD_DIM].set(
        jnp.stack([lambda_q1[0], lambda_k1[0], lambda_q2[0], lambda_k2[0]]))
    rot_p = _rot_table(jnp.arange(S), S)
    rot_s = _rot_table(past + jnp.arange(L), PROJ_TM)

    far = _far_bias(rel_bias_table)[:, None, None]
    qp = ATT_TQ + jnp.arange(ATT_TQ)
    def tile_bias(k_pos):
        bias = (_relative_bias(qp, k_pos, rel_bias_table) - far) * LOG2E
        visible = (k_pos[None, :] // CHUNK) <= (qp[:, None] // CHUNK)
        bias = jnp.swapaxes(jnp.where(visible[None], bias, NEG_INF), 1, 2)
        return jnp.concatenate([bias, bias], axis=2)
    bias_prev = tile_bias(jnp.arange(ATT_TK))
    bias_diag = tile_bias(ATT_TQ + jnp.arange(ATT_TK))
    q_pos_s = past + jnp.arange(L)
    assert SAMPLE_NEAR - L >= MAX_DISTANCE
    b_last = (_relative_bias(q_pos_s, past - SAMPLE_NEAR + jnp.arange(SAMPLE_NEAR), rel_bias_table) - far) * LOG2E
    bias_past = jnp.concatenate([b_last, b_last], axis=1)
    b_new = (_relative_bias(q_pos_s, q_pos_s, rel_bias_table) - far) * LOG2E
    bias_new = jnp.concatenate([b_new, b_new], axis=1)

    xp2 = x_prompt.reshape(TP, D)
    xs2 = x_sample.reshape(TS, D)
    z3p, gp, gatesp, v32p, qt, vt5, kt32 = _proj(xp2, gain_mix, w_in_bf, rot_p, seq=S, transposed_k=True)
    z3s, gs, gatess, v32s, k32s = _proj(xs2, gain_mix, w_in_bf, rot_s, seq=L, transposed_k=False)

    zero_state = jnp.zeros((B, N_HEADS, RET_KEY_DIM, HEAD_V), F32)
    orp, ret_p = _retention(z3p, zero_state, batch=B, seq=S, chunk=256)
    ors, ret_s = _retention(z3s, state_retention[0], batch=BS, seq=L, chunk=L)

    gain_col = jnp.broadcast_to(diff_subln_gain[0][:, None], (HEAD_V, ATT_TQ))
    odp = _attn_prompt(z3p, qt, vt5, bias_prev, bias_diag, lam_rows, gain_col, batch=B, seq=S)
    kc = jnp.transpose(cache_diff_k[0], (0, 2, 3, 4, 1)).reshape(BS, N_HEADS, LANES, past)
    vc = cache_diff_v[0].reshape(BS, past * N_HEADS, LANES)
    ods = _attn_sample(z3s, kc, vc, bias_past, bias_new, lam_rows, gain_sub, batch=BS, n_q=L, past=past)

    mgp = _post_a(orp, gp, odp, gatesp, wr_bf, wd_bf)
    mgs = _post_a(ors, gs, ods, gatess, wr_bf, wd_bf)

    w_rt = jnp.zeros((D, LANES), F32)
    w_rt = w_rt.at[:, 0:N_GROUPS].set(w_group_router[0])
    w_rt = w_rt.at[:, N_GROUPS:N_GROUPS + N_EXPERTS].set(
        jnp.transpose(w_expert_router[0], (1, 0, 2)).reshape(D, N_EXPERTS))
    w_rt_hi = w_rt.astype(BF16)
    w_rt_lo = (w_rt - w_rt_hi.astype(F32)).astype(BF16)
    w_router2 = jnp.concatenate([w_rt_hi, w_rt_lo], axis=1)
    bias_row = jnp.zeros((1, LANES), F32)
    bias_row = bias_row.at[0, 0:N_GROUPS].set(b_group_router[0])
    bias_row = bias_row.at[0, N_GROUPS:N_GROUPS + N_EXPERTS].set(b_expert_router[0].reshape(-1))

    x1_all, lg_all, h_packed = _post_b(xp2, mgp, xs2, mgs, wo_bf, gain_ffn, w_router2)

    info, counts = _route(lg_all, bias_row)
    counts = counts[0, :N_EXPERTS].astype(jnp.int32)
    padded = (counts + EXPERT_BLOCK - 1) // EXPERT_BLOCK * EXPERT_BLOCK
    pad_end = jnp.cumsum(padded)
    offs = pad_end - padded
    e12 = info[:, 0:2].astype(jnp.int32)
    rank12 = info[:, 4:6].astype(jnp.int32)
    expert_ids = jnp.arange(N_EXPERTS, dtype=jnp.int32)
    offs_e = jnp.sum(jnp.where(e12[:, :, None] == expert_ids, offs, 0), axis=-1)
    dest = (offs_e + rank12).reshape(-1)
    n_assign = 2 * T
    n_rows = -(-n_assign // EXPERT_BLOCK) * EXPERT_BLOCK + N_EXPERTS * EXPERT_BLOCK
    n_blocks = n_rows // EXPERT_BLOCK
    block_start = jnp.arange(n_blocks, dtype=jnp.int32) * EXPERT_BLOCK
    block_expert = jnp.minimum(jnp.sum((pad_end[None, :] <= block_start[:, None]).astype(jnp.int32), axis=1),
                               N_EXPERTS - 1)
    n_used = (pad_end[-1:] // EXPERT_BLOCK).astype(jnp.int32)
    later_used = (expert_ids[None, :] > expert_ids[:, None]) & (counts[None, :] > 0)
    next_of_expert = jnp.min(jnp.where(later_used, expert_ids[None, :], N_EXPERTS), axis=1)
    next_of_expert = jnp.where(next_of_expert == N_EXPERTS, -1, next_of_expert)
    next_expert = jnp.sum(jnp.where(block_expert[:, None] == expert_ids[None, :], next_of_expert[None, :], 0),
                          axis=1).astype(jnp.int32)

    pad_bounds = jnp.concatenate([offs + counts, pad_end]).astype(jnp.int32)
    out_rows = _moe(dest, block_expert, next_expert, pad_bounds, n_used, h_packed,
                    w_expert_gate[0], w_expert_up[0], w_expert_down[0])

    y_p = _final(dest, x1_all, info, gain_fin, out_rows, tok_offset=0, n_tok=TP)
    y_s = _final(dest, x1_all, info, gain_fin, out_rows, tok_offset=TP, n_tok=TS)

    new_k_p = jnp.transpose(kt32, (0, 1, 5, 2, 3, 4))
    new_v_p = v32p.reshape(1, B, S, N_HEADS, HEAD_V)
    new_k_s = k32s.reshape(1, BS, L, N_HEADS, 2, DIFF_HEAD_DIM)
    new_v_s = v32s.reshape(1, BS, L, N_HEADS, HEAD_V)
    return (y_p.reshape(B, S, D), y_s.reshape(BS, L, D), new_k_p, new_v_p, ret_p[None],
            new_k_s, new_v_s, ret_s[None])
```

```python
import functools
import math

import jax
import jax.numpy as jnp
from jax import lax
from jax.experimental import pallas as pl
from jax.experimental.pallas import tpu as pltpu

F32 = jnp.float32
BF16 = jnp.bfloat16

D_MODEL = 2048
CHUNK = 64
N_HEADS = 8
RET_KEY_DIM = 64
HEAD_V = 128
DIFF_HEAD_DIM = 64
IN_WIDTH = 10240
NUM_BUCKETS = 32
MAX_DISTANCE = 128
N_GROUPS = 8
EXPERTS_PER_GROUP = 8
N_EXPERTS = 64
EXPERT_HIDDEN = 512
EXPERT_BLOCK = 128
NORM_EPS = 1e-6
NEG_INF = -1e30
LANES = 128
LOG2E = 1.4426950408889634
VT_ROWS = HEAD_V + 16

VMEM_LIMIT = 56 * 1024 * 1024


def _cparams(sem):
    return pltpu.CompilerParams(dimension_semantics=sem, vmem_limit_bytes=VMEM_LIMIT)


PROJ_TM = 512
PROJ_TN = 1024
PROJ_MC = 128
ATT_TK = 256


def _rotate_pairs(acc, cos, sin):
    outs = []
    lane = lax.broadcasted_iota(jnp.int32, (acc.shape[0], LANES), 1)
    first_half = (lane % 64) < 32
    for c in range(acc.shape[1] // LANES):
        xs = acc[:, c * LANES:(c + 1) * LANES]
        swapped = jnp.where(first_half, pltpu.roll(xs, 96, axis=1), pltpu.roll(xs, 32, axis=1))
        outs.append(xs * cos + swapped * sin)
    return outs


def _proj_kernel(x_ref, gain_ref, w_ref, rot_ref, *out_refs, transposed_k):
    if transposed_k:
        z_ref, g_ref, gates_ref, v32_ref, qt_ref, vt_ref, kt32_ref, h_scr = out_refs
    else:
        z_ref, g_ref, gates_ref, v32_ref, k32_ref, h_scr = out_refs
    j = pl.program_id(1)

    @pl.when(j == 0)
    def _():
        x = x_ref[...]
        ms = jnp.mean(x * x, axis=-1, keepdims=True)
        h_scr[...] = (x * lax.rsqrt(ms + NORM_EPS) * gain_ref[...]).astype(BF16)

    n_slab = PROJ_TN // LANES

    def for_row_chunks(epilogue):
        for c in range(PROJ_TM // PROJ_MC):
            rows = slice(c * PROJ_MC, (c + 1) * PROJ_MC)
            epilogue(c, rows, jnp.dot(h_scr[rows, :], w_ref[...], preferred_element_type=F32))

    def store_slabs(rows, vals):
        for s in range(n_slab):
            z_ref[s, rows, :] = vals[s].astype(BF16)

    def split(a):
        return [a[:, s * LANES:(s + 1) * LANES] for s in range(n_slab)]

    @pl.when(j == 0)
    def _():
        def epilogue(c, rows, acc):
            half = PROJ_TN // 2
            q = _rotate_pairs(acc[:, :half], rot_ref[rows, 0:128], rot_ref[rows, 128:256])
            k = _rotate_pairs(acc[:, half:], rot_ref[rows, 256:384], rot_ref[rows, 384:512])
            store_slabs(rows, q + k)
        for_row_chunks(epilogue)

    @pl.when(j == 1)
    def _():
        for_row_chunks(lambda c, rows, acc: store_slabs(rows, split(acc)))

    @pl.when(j == 2)
    def _():
        def epilogue(c, rows, acc):
            g_ref[rows, :] = (acc * jax.nn.sigmoid(acc)).astype(BF16)
        for_row_chunks(epilogue)

    @pl.when(j == 3)
    def _():
        def epilogue(c, rows, acc):
            qs = acc * (DIFF_HEAD_DIM ** -0.5 * LOG2E)
            store_slabs(rows, split(qs))
            if transposed_k:
                qs_t = qs.T.astype(BF16)
                for hh in range(n_slab):
                    qt_ref[0, hh, :, rows] = qs_t[hh * LANES:(hh + 1) * LANES, :]
        for_row_chunks(epilogue)

    @pl.when(j == 4)
    def _():
        def epilogue(c, rows, acc):
            store_slabs(rows, split(acc))
            if transposed_k:
                kt32_ref[0, 0, :, :, :, rows] = acc.T.reshape(n_slab, 2, DIFF_HEAD_DIM, PROJ_MC)
            else:
                k32_ref[rows, :] = acc
        for_row_chunks(epilogue)

    @pl.when(j == 5)
    def _():
        def epilogue(c, rows, acc):
            store_slabs(rows, split(acc))
            v32_ref[rows, :] = acc
            if transposed_k:
                acc_tb = acc.T.astype(BF16)
                t, cols = (c * PROJ_MC) // ATT_TK, slice((c * PROJ_MC) % ATT_TK, (c * PROJ_MC) % ATT_TK + PROJ_MC)
                ones = jnp.ones((VT_ROWS - HEAD_V, PROJ_MC), BF16)
                for hh in range(n_slab):
                    vt_ref[0, hh, t, 0:HEAD_V, cols] = acc_tb[hh * LANES:(hh + 1) * LANES, :]
                    vt_ref[0, hh, t, HEAD_V:VT_ROWS, cols] = ones
        for_row_chunks(epilogue)

    @pl.when(j >= 6)
    def _():
        def epilogue(c, rows, acc):
            gates_ref[rows, :] = jax.nn.sigmoid(acc).astype(BF16)
        for_row_chunks(epilogue)


def _proj(x2d, gain, w_bf, rot, *, seq, transposed_k):
    T = x2d.shape[0]
    tm, tn = PROJ_TM, PROJ_TN
    ni, nj = T // tm, IN_WIDTH // tn
    rot_blocks = rot.shape[0] // tm

    def zmap(i, j):
        zj = jnp.where(j < 2, j, jnp.where(j < 3, 1, jnp.where(j < 6, j - 1, 4)))
        return (zj, i, 0)

    out_shape = [
        jax.ShapeDtypeStruct((40, T, LANES), BF16),
        jax.ShapeDtypeStruct((T, 1024), BF16),
        jax.ShapeDtypeStruct((T, 4096), BF16),
        jax.ShapeDtypeStruct((T, 1024), F32),
    ]
    out_specs = [
        pl.BlockSpec((8, tm, LANES), zmap),
        pl.BlockSpec((tm, tn), lambda i, j: (i, 0)),
        pl.BlockSpec((tm, tn), lambda i, j: (i, jnp.clip(j - 6, 0, 3))),
        pl.BlockSpec((tm, tn), lambda i, j: (i, 0)),
    ]
    if transposed_k:
        B = T // seq
        spb = seq // tm
        out_shape += [
            jax.ShapeDtypeStruct((B, N_HEADS, LANES, seq), BF16),
            jax.ShapeDtypeStruct((B, N_HEADS, seq // ATT_TK, VT_ROWS, ATT_TK), BF16),
            jax.ShapeDtypeStruct((1, B, N_HEADS, 2, DIFF_HEAD_DIM, seq), F32),
        ]
        out_specs += [
            pl.BlockSpec((1, N_HEADS, LANES, tm), lambda i, j: (i // spb, 0, 0, i % spb)),
            pl.BlockSpec((1, N_HEADS, tm // ATT_TK, VT_ROWS, ATT_TK), lambda i, j: (i // spb, 0, i % spb, 0, 0)),
            pl.BlockSpec((1, 1, N_HEADS, 2, DIFF_HEAD_DIM, tm), lambda i, j: (0, i // spb, 0, 0, 0, i % spb)),
        ]
    else:
        out_shape += [jax.ShapeDtypeStruct((T, 1024), F32)]
        out_specs += [pl.BlockSpec((tm, tn), lambda i, j: (i, 0))]

    return pl.pallas_call(
        functools.partial(_proj_kernel, transposed_k=transposed_k),
        out_shape=out_shape,
        grid=(ni, nj),
        in_specs=[
            pl.BlockSpec((tm, D_MODEL), lambda i, j: (i, 0)),
            pl.BlockSpec((1, D_MODEL), lambda i, j: (0, 0)),
            pl.BlockSpec((D_MODEL, tn), lambda i, j: (0, j)),
            pl.BlockSpec((tm, 512), lambda i, j: (i % rot_blocks, 0)),
        ],
        out_specs=out_specs,
        scratch_shapes=[pltpu.VMEM((tm, D_MODEL), BF16)],
        compiler_params=_cparams(("arbitrary", "arbitrary")),
        name="proj_t" if transposed_k else "proj",
    )(x2d, gain, w_bf, rot)


def _retention_kernel(q_ref, k_ref, v_ref, decay_ref, inner_ref, outer_ref, st0_ref,
                      o_ref, st_out_ref, st_scr, *, chunk):
    c = pl.program_id(1)
    nc = pl.num_programs(1)

    @pl.when(c == 0)
    def _():
        st_scr[...] = jnp.zeros_like(st_scr)
        for h in range(N_HEADS):
            a = h % 2
            st_scr[h, a * RET_KEY_DIM:(a + 1) * RET_KEY_DIM, :] = st0_ref[0, h]

    lane_lo = lax.broadcasted_iota(jnp.int32, (chunk, LANES), 1) < RET_KEY_DIM
    row_lo = lax.broadcasted_iota(jnp.int32, (LANES, LANES), 0) < RET_KEY_DIM
    for h in range(N_HEADS):
        p, a = h // 2, h % 2
        q = q_ref[p]
        k = k_ref[p]
        v = v_ref[h]
        qa = jnp.where(lane_lo == (a == 0), q, jnp.zeros_like(q))
        s = lax.dot_general(qa, k, (((1,), (1,)), ((), ())), preferred_element_type=F32)
        s = s * decay_ref[h]
        st = st_scr[h]
        inner = inner_ref[h]
        o = (jnp.dot(s.astype(BF16), v, preferred_element_type=F32)
             + jnp.dot(qa, st.astype(BF16), preferred_element_type=F32) * inner)
        ko = (k.astype(F32) * outer_ref[p]).astype(BF16)
        upd = lax.dot_general(ko, v, (((0,), (0,)), ((), ())), preferred_element_type=F32)
        upd = jnp.where(row_lo == (a == 0), upd, 0.0)
        chunk_decay = inner[chunk - 1:chunk, :]
        st_scr[h] = chunk_decay * st + upd
        o_n = o * lax.rsqrt(jnp.mean(o * o, axis=-1, keepdims=True) + NORM_EPS)
        o_ref[:, h * HEAD_V:(h + 1) * HEAD_V] = o_n.astype(BF16)

    @pl.when(c == nc - 1)
    def _():
        for h in range(N_HEADS):
            a = h % 2
            st_out_ref[0, h] = st_scr[h, a * RET_KEY_DIM:(a + 1) * RET_KEY_DIM, :]


def _retention_consts(chunk):
    log_g = jnp.log(1.0 - 2.0 ** (-5.0 - jnp.arange(N_HEADS, dtype=F32)))
    j = jnp.arange(chunk, dtype=F32)
    diff = j[:, None] - j[None, :]
    decay = jnp.where(diff >= 0, jnp.exp(log_g[:, None, None] * jnp.maximum(diff, 0.0)), 0.0)
    inner = jnp.exp(log_g[:, None] * (j + 1.0))
    outer = jnp.exp(log_g[:, None] * (chunk - 1.0 - j))
    inner_b = jnp.broadcast_to(inner[:, :, None], (N_HEADS, chunk, LANES))
    outer_pair = jnp.repeat(outer.reshape(N_HEADS // 2, 2, chunk).transpose(0, 2, 1), RET_KEY_DIM, axis=2)
    return decay, inner_b, outer_pair


def _retention(z3, state0, *, batch, seq, chunk):
    T = batch * seq
    nc = seq // chunk
    decay, inner_b, outer_pair = _retention_consts(chunk)
    return pl.pallas_call(
        functools.partial(_retention_kernel, chunk=chunk),
        out_shape=[jax.ShapeDtypeStruct((T, 1024), BF16),
                   jax.ShapeDtypeStruct((batch, N_HEADS, RET_KEY_DIM, HEAD_V), F32)],
        grid=(batch, nc),
        in_specs=[
            pl.BlockSpec((4, chunk, LANES), lambda b, c: (0, b * nc + c, 0)),
            pl.BlockSpec((4, chunk, LANES), lambda b, c: (1, b * nc + c, 0)),
            pl.BlockSpec((8, chunk, LANES), lambda b, c: (1, b * nc + c, 0)),
            pl.BlockSpec((N_HEADS, chunk, chunk), lambda b, c: (0, 0, 0)),
            pl.BlockSpec((N_HEADS, chunk, LANES), lambda b, c: (0, 0, 0)),
            pl.BlockSpec((N_HEADS // 2, chunk, LANES), lambda b, c: (0, 0, 0)),
            pl.BlockSpec((1, N_HEADS, RET_KEY_DIM, HEAD_V), lambda b, c: (b, 0, 0, 0)),
        ],
        out_specs=[
            pl.BlockSpec((chunk, 1024), lambda b, c: (b * nc + c, 0)),
            pl.BlockSpec((1, N_HEADS, RET_KEY_DIM, HEAD_V), lambda b, c: (b, 0, 0, 0)),
        ],
        scratch_shapes=[pltpu.VMEM((N_HEADS, LANES, LANES), F32)],
        compiler_params=_cparams(("arbitrary", "arbitrary")),
        name=f"retention_c{chunk}",
    )(z3, z3, z3, decay, inner_b, outer_pair, state0)


def _lam_init(layer=0):
    return 0.8 - 0.6 * math.exp(-0.3 * layer)


def _lam_from_ref(lam_ref):
    lp = lam_ref[...]
    s1 = jnp.sum(lp[0:1] * lp[1:2], axis=-1, keepdims=True)
    s2 = jnp.sum(lp[2:3] * lp[3:4], axis=-1, keepdims=True)
    return jnp.exp(s1) - jnp.exp(s2) + _lam_init()


def _stack_q(q):
    lane_lo = lax.broadcasted_iota(jnp.int32, q.shape, 1) < DIFF_HEAD_DIM
    zero = jnp.zeros_like(q)
    return jnp.concatenate([jnp.where(lane_lo, q, zero), jnp.where(lane_lo, zero, q)], axis=0)


def _with_ones(v):
    return jnp.concatenate([v, jnp.ones_like(v)], axis=1)


def _diff_finish(acc_ref, lam, gain, n):
    o = acc_ref[:, 0:HEAD_V] / acc_ref[:, HEAD_V:2 * HEAD_V]
    o = o[:n] - lam * o[n:]
    o = o * lax.rsqrt(jnp.mean(o * o, axis=-1, keepdims=True) + NORM_EPS) * gain
    return o * (1.0 - _lam_init())


def _relative_bias(q_pos, k_pos, table):
    rel = k_pos[None, :] - q_pos[:, None]
    half = NUM_BUCKETS // 2
    max_exact = half // 2
    n = jnp.abs(rel)
    log_ratio = jnp.log(jnp.maximum(n, 1).astype(F32) / max_exact) / math.log(MAX_DISTANCE / max_exact)
    large = jnp.minimum(max_exact + (log_ratio * (half - max_exact)).astype(jnp.int32), half - 1)
    bucket = (rel > 0).astype(jnp.int32) * half + jnp.where(n < max_exact, n, large)
    out = jnp.zeros((table.shape[1],) + bucket.shape, F32)
    for bkt in range(NUM_BUCKETS):
        out = jnp.where(bucket[None] == bkt, table[bkt].astype(F32)[:, None, None], out)
    return out


def _far_bias(table):
    return table[NUM_BUCKETS // 2 - 1].astype(F32)


ATT_TQ = 256
ATT_HP = 8


def _attn_prompt_kernel(qt_ref, k_ref, vt_ref, bprev_ref, bdiag_ref, lam_ref, gain_ref, o_ref, acc_scr):
    i = pl.program_id(2)
    row_lo = lax.broadcasted_iota(jnp.int32, (LANES, ATT_TQ), 0) < DIFF_HEAD_DIM
    qs = []
    for hh in range(ATT_HP):
        qt = qt_ref[0, hh]
        zero = jnp.zeros_like(qt)
        qs.append(jnp.concatenate([jnp.where(row_lo, qt, zero), jnp.where(row_lo, zero, qt)], axis=1))
    acc_scr[...] = jnp.zeros_like(acc_scr)

    def tile(j, ms, biases):
        ss = []
        for hh in range(ATT_HP):
            kt = k_ref[hh, pl.ds(pl.multiple_of(j * ATT_TK, ATT_TK), ATT_TK), :]
            s = jnp.dot(kt, qs[hh], preferred_element_type=F32)
            ss.append(s if biases is None else s + biases[hh])
        ps, alphas, m_out = [], [], []
        for hh in range(ATT_HP):
            m_new = jnp.maximum(ms[hh], jnp.max(ss[hh], axis=0, keepdims=True))
            alphas.append(jnp.exp2(ms[hh] - m_new))
            ps.append(jnp.exp2(ss[hh] - m_new).astype(BF16))
            m_out.append(m_new)
        for hh in range(ATT_HP):
            acc_scr[hh] = alphas[hh] * acc_scr[hh] + jnp.dot(vt_ref[0, hh, j], ps[hh],
                                                             preferred_element_type=F32)
        return tuple(m_out)

    ms = tuple(jnp.full((1, 2 * ATT_TQ), NEG_INF, F32) for _ in range(ATT_HP))
    ms = lax.fori_loop(0, i - 1, lambda j, c: tile(j, c, None), ms)
    jp = jnp.maximum(i - 1, 0)
    ms = lax.cond(i >= 1,
                  lambda c: tile(jp, c, [bprev_ref[hh] for hh in range(ATT_HP)]),
                  lambda c: c, ms)
    tile(i, ms, [bdiag_ref[hh] for hh in range(ATT_HP)])

    lam = _lam_from_ref(lam_ref)
    for hh in range(ATT_HP):
        o = acc_scr[hh, 0:HEAD_V, :] / acc_scr[hh, HEAD_V:HEAD_V + 1, :]
        o = o[:, :ATT_TQ] - lam * o[:, ATT_TQ:]
        o = o * lax.rsqrt(jnp.mean(o * o, axis=0, keepdims=True) + NORM_EPS) * gain_ref[...]
        o_ref[:, hh * HEAD_V:(hh + 1) * HEAD_V] = (o * (1.0 - _lam_init())).T.astype(BF16)


def _attn_prompt(z3, qt, vt5, bias_prev, bias_diag, lam_rows, gain_col, *, batch, seq):
    T = batch * seq
    nq = seq // ATT_TQ
    hp = ATT_HP
    return pl.pallas_call(
        _attn_prompt_kernel,
        out_shape=jax.ShapeDtypeStruct((T, 1024), BF16),
        grid=(batch, N_HEADS // hp, nq),
        in_specs=[
            pl.BlockSpec((1, hp, LANES, ATT_TQ), lambda b, g, i: (b, g, 0, i)),
            pl.BlockSpec((hp, seq, LANES), lambda b, g, i: (24 // hp + g, b, 0), pipeline_mode=pl.Buffered(1)),
            pl.BlockSpec((1, hp, seq // ATT_TK, VT_ROWS, ATT_TK), lambda b, g, i: (b, g, 0, 0, 0),
                         pipeline_mode=pl.Buffered(1)),
            pl.BlockSpec((hp, ATT_TK, 2 * ATT_TQ), lambda b, g, i: (g, 0, 0), pipeline_mode=pl.Buffered(1)),
            pl.BlockSpec((hp, ATT_TK, 2 * ATT_TQ), lambda b, g, i: (g, 0, 0), pipeline_mode=pl.Buffered(1)),
            pl.BlockSpec((8, LANES), lambda b, g, i: (0, 0)),
            pl.BlockSpec((LANES, ATT_TQ), lambda b, g, i: (0, 0)),
        ],
        out_specs=pl.BlockSpec((ATT_TQ, hp * HEAD_V), lambda b, g, i: (b * nq + i, g)),
        scratch_shapes=[pltpu.VMEM((hp, VT_ROWS, 2 * ATT_TQ), F32)],
        compiler_params=_cparams(("arbitrary", "arbitrary", "arbitrary")),
        name="attn_prompt",
    )(qt, z3, vt5, bias_prev, bias_diag, lam_rows, gain_col)


SAMPLE_TK = 1024
SAMPLE_NEAR = 256


def _attn_sample_kernel(q_ref, kc_ref, vc_ref, kn_ref, vn_ref, bpast_ref, bnew_ref, lam_ref, gain_ref,
                        o_ref, m_scr, acc_scr, *, n_past_tiles, n_q):
    t = pl.program_id(1)

    @pl.when(t == 0)
    def _():
        m_scr[...] = jnp.full_like(m_scr, NEG_INF)
        acc_scr[...] = jnp.zeros_like(acc_scr)

    def update(logits, values):
        ps, alphas = [], []
        for h in range(N_HEADS):
            m_prev = m_scr[h]
            m_new = jnp.maximum(m_prev, jnp.max(logits[h], axis=1, keepdims=True))
            alpha = jnp.exp2(m_prev - m_new)
            ps.append(jnp.exp2(logits[h] - m_new[:, 0:1]).astype(BF16))
            alphas.append(jnp.concatenate([alpha, alpha], axis=1))
            m_scr[h] = m_new
        for h in range(N_HEADS):
            acc_scr[h] = alphas[h] * acc_scr[h] + jnp.dot(ps[h], _with_ones(values[h]()),
                                                          preferred_element_type=F32)

    @pl.when(t < n_past_tiles)
    def _():
        is_last = jnp.where(t == n_past_tiles - 1, 1.0, 0.0)
        far_w = SAMPLE_TK - SAMPLE_NEAR
        logits = []
        for h in range(N_HEADS):
            kt = kc_ref[0, h].astype(BF16)
            s = jnp.dot(_stack_q(q_ref[h]), kt, preferred_element_type=F32)
            logits.append(jnp.concatenate([s[:, :far_w], s[:, far_w:] + is_last * bpast_ref[h]], axis=1))
        update(logits, [lambda h=h: vc_ref[0, pl.ds(h, SAMPLE_TK, stride=N_HEADS), :].astype(BF16)
                        for h in range(N_HEADS)])

    @pl.when(t == n_past_tiles)
    def _():
        logits = []
        for h in range(N_HEADS):
            s = lax.dot_general(_stack_q(q_ref[h]), kn_ref[h], (((1,), (1,)), ((), ())),
                                preferred_element_type=F32)
            logits.append(s + bnew_ref[h])
        update(logits, [lambda h=h: vn_ref[h] for h in range(N_HEADS)])
        lam = _lam_from_ref(lam_ref)
        for h in range(N_HEADS):
            o = _diff_finish(acc_scr.at[h], lam, gain_ref[...], n_q)
            o_ref[:, h * HEAD_V:(h + 1) * HEAD_V] = o.astype(BF16)


def _attn_sample(z3, kc, vc, bias_past, bias_new, lam_rows, gain, *, batch, n_q, past):
    npt = past // SAMPLE_TK
    return pl.pallas_call(
        functools.partial(_attn_sample_kernel, n_past_tiles=npt, n_q=n_q),
        out_shape=jax.ShapeDtypeStruct((batch * n_q, 1024), BF16),
        grid=(batch, npt + 1),
        in_specs=[
            pl.BlockSpec((8, n_q, LANES), lambda b, t: (2, b, 0)),
            pl.BlockSpec((1, N_HEADS, LANES, SAMPLE_TK),
                         lambda b, t: (jnp.minimum(b + t // npt, batch - 1), 0, 0, t % npt)),
            pl.BlockSpec((1, SAMPLE_TK * N_HEADS, LANES),
                         lambda b, t: (jnp.minimum(b + t // npt, batch - 1), t % npt, 0)),
            pl.BlockSpec((8, n_q, LANES), lambda b, t: (3, b, 0)),
            pl.BlockSpec((8, n_q, LANES), lambda b, t: (4, b, 0)),
            pl.BlockSpec((N_HEADS, 2 * n_q, SAMPLE_NEAR), lambda b, t: (0, 0, 0)),
            pl.BlockSpec((N_HEADS, 2 * n_q, n_q), lambda b, t: (0, 0, 0)),
            pl.BlockSpec((8, LANES), lambda b, t: (0, 0)),
            pl.BlockSpec((1, LANES), lambda b, t: (0, 0)),
        ],
        out_specs=pl.BlockSpec((n_q, 1024), lambda b, t: (b, 0)),
        scratch_shapes=[pltpu.VMEM((N_HEADS, 2 * n_q, LANES), F32),
                        pltpu.VMEM((N_HEADS, 2 * n_q, 2 * LANES), F32)],
        compiler_params=_cparams(("arbitrary", "arbitrary")),
        name="attn_sample",
    )(z3, kc, vc, z3, z3, bias_past, bias_new, lam_rows, gain)


POST_TM = 256


def _post_a_kernel(or_ref, g_ref, od_ref, gr_ref, gd_ref, wr_ref, wd_ref, out_ref):
    a = jnp.dot(or_ref[...] * g_ref[...], wr_ref[...], preferred_element_type=F32)
    b = jnp.dot(od_ref[...], wd_ref[...], preferred_element_type=F32)
    out_ref[...] = (gr_ref[...].astype(F32) * a + gd_ref[...].astype(F32) * b).astype(BF16)


def _post_a(o_r, g, o_d, gates, wr_bf, wd_bf):
    T = o_r.shape[0]
    tm = POST_TM
    return pl.pallas_call(
        _post_a_kernel,
        out_shape=jax.ShapeDtypeStruct((T, D_MODEL), BF16),
        grid=(T // tm,),
        in_specs=[
            pl.BlockSpec((tm, 1024), lambda i: (i, 0)),
            pl.BlockSpec((tm, 1024), lambda i: (i, 0)),
            pl.BlockSpec((tm, 1024), lambda i: (i, 0)),
            pl.BlockSpec((tm, D_MODEL), lambda i: (i, 0)),
            pl.BlockSpec((tm, D_MODEL), lambda i: (i, 1)),
            pl.BlockSpec((1024, D_MODEL), lambda i: (0, 0)),
            pl.BlockSpec((1024, D_MODEL), lambda i: (0, 0)),
        ],
        out_specs=pl.BlockSpec((tm, D_MODEL), lambda i: (i, 0)),
        compiler_params=_cparams(("arbitrary",)),
        name="post_a",
    )(o_r, g, o_d, gates, gates, wr_bf, wd_bf)


def _pack_bf16_pairs(x):
    w = x.shape[1] // 2
    xb = x.astype(BF16).astype(F32)
    lo = pltpu.bitcast(xb[:, :w], jnp.uint32) >> 16
    hi = pltpu.bitcast(xb[:, w:], jnp.uint32) & jnp.uint32(0xFFFF0000)
    return lo | hi


def _unpack_bf16_pairs(words):
    lo = pltpu.bitcast(words << 16, F32)
    hi = pltpu.bitcast(words & jnp.uint32(0xFFFF0000), F32)
    return jnp.concatenate([lo, hi], axis=1)


def _post_b_kernel(xp_ref, mp_ref, xs_ref, ms_ref, wo_ref, gain_ref, wrt_ref, x1_ref, lg_ref, hp_ref, *,
                   n_prompt_blocks):
    i = pl.program_id(0)
    tm = POST_TM

    def body(x_ref, mg_ref):
        x1 = x_ref[...] + jnp.dot(mg_ref[...], wo_ref[...], preferred_element_type=F32)
        x1_ref[...] = x1
        h2 = x1 * lax.rsqrt(jnp.mean(x1 * x1, axis=-1, keepdims=True) + NORM_EPS) * gain_ref[...]
        hp_ref[...] = _pack_bf16_pairs(h2)
        h_hi = h2.astype(BF16)
        h_lo = (h2 - h_hi.astype(F32)).astype(BF16)
        r = jnp.dot(jnp.concatenate([h_hi, h_lo], axis=0), wrt_ref[...], preferred_element_type=F32)
        lg_ref[...] = (r[:tm, :LANES] + r[:tm, LANES:]) + (r[tm:, :LANES] + r[tm:, LANES:])

    @pl.when(i < n_prompt_blocks)
    def _():
        body(xp_ref, mp_ref)

    @pl.when(i >= n_prompt_blocks)
    def _():
        body(xs_ref, ms_ref)


def _post_b(xp2, mgp, xs2, mgs, wo_bf, gain, w_router2):
    tm = POST_TM
    npb, nsb = xp2.shape[0] // tm, xs2.shape[0] // tm
    T = xp2.shape[0] + xs2.shape[0]
    pmap = lambda i: (jnp.minimum(i, npb - 1), 0)
    smap = lambda i: (jnp.maximum(i - npb, 0), 0)
    return pl.pallas_call(
        functools.partial(_post_b_kernel, n_prompt_blocks=npb),
        out_shape=[jax.ShapeDtypeStruct((T, D_MODEL), F32),
                   jax.ShapeDtypeStruct((T, LANES), F32),
                   jax.ShapeDtypeStruct((T, D_MODEL // 2), jnp.uint32)],
        grid=(npb + nsb,),
        in_specs=[
            pl.BlockSpec((tm, D_MODEL), pmap),
            pl.BlockSpec((tm, D_MODEL), pmap),
            pl.BlockSpec((tm, D_MODEL), smap),
            pl.BlockSpec((tm, D_MODEL), smap),
            pl.BlockSpec((D_MODEL, D_MODEL), lambda i: (0, 0)),
            pl.BlockSpec((1, D_MODEL), lambda i: (0, 0)),
            pl.BlockSpec((D_MODEL, 2 * LANES), lambda i: (0, 0)),
        ],
        out_specs=[pl.BlockSpec((tm, D_MODEL), lambda i: (i, 0)),
                   pl.BlockSpec((tm, LANES), lambda i: (i, 0)),
                   pl.BlockSpec((tm, D_MODEL // 2), lambda i: (i, 0))],
        compiler_params=_cparams(("arbitrary",)),
        name="post_b",
    )(xp2, mgp, xs2, mgs, wo_bf, gain, w_router2)


ROUTE_TM = 512


def _route_kernel(lg_ref, bias_ref, info_ref, cnt_ref, tri_scr, carry_scr):
    i = pl.program_id(0)
    tm = ROUTE_TM

    @pl.when(i == 0)
    def _():
        r = lax.broadcasted_iota(jnp.int32, (tm, tm), 0)
        c = lax.broadcasted_iota(jnp.int32, (tm, tm), 1)
        tri_scr[...] = jnp.where(c < r, 1.0, 0.0).astype(BF16)
        carry_scr[...] = jnp.zeros_like(carry_scr)

    lg = lg_ref[...] + bias_ref[...]
    lane = lax.broadcasted_iota(jnp.int32, (tm, LANES), 1)
    lane_f = lane.astype(F32)
    neg = jnp.float32(-jnp.inf)

    def first_argmax(vals):
        top = jnp.max(vals, axis=1, keepdims=True)
        idx = jnp.min(jnp.where(vals == top, lane_f, float(LANES)), axis=1, keepdims=True)
        return top, idx

    is_group = lane < N_GROUPS
    gl = jnp.where(is_group, lg, neg)
    g_top, g_idx = first_argmax(gl)
    g_weight = 1.0 / jnp.sum(jnp.exp(gl - g_top), axis=1, keepdims=True)
    lane_group = ((lane - N_GROUPS) >> 3).astype(F32)
    in_group = (lane >= N_GROUPS) & (lane < N_GROUPS + N_EXPERTS) & (lane_group == g_idx)
    el = jnp.where(in_group, lg, neg)
    t1, i1 = first_argmax(el)
    el2 = jnp.where(lane_f == i1, neg, el)
    t2, i2 = first_argmax(el2)
    e2w = jnp.exp(t2 - t1)
    p1 = 1.0 / (1.0 + e2w)
    gate1 = g_weight * p1
    gate2 = g_weight * (e2w * p1)
    e1 = i1 - float(N_GROUPS)
    e2 = i2 - float(N_GROUPS)

    hot1 = lane_f == e1
    hot2 = lane_f == e2
    both = jnp.where(hot1 | hot2, 1.0, 0.0)
    prefix = jnp.dot(tri_scr[...], both.astype(BF16), preferred_element_type=F32) + carry_scr[...]
    rank1 = jnp.sum(jnp.where(hot1, prefix, 0.0), axis=1, keepdims=True)
    rank2 = jnp.sum(jnp.where(hot2, prefix, 0.0), axis=1, keepdims=True)
    carry_scr[...] = carry_scr[...] + jnp.sum(both, axis=0, keepdims=True)

    info = jnp.where(lane == 0, e1, jnp.where(lane == 1, e2, jnp.where(lane == 2, gate1, jnp.where(
        lane == 3, gate2, jnp.where(lane == 4, rank1, jnp.where(lane == 5, rank2, 0.0))))))
    info_ref[...] = info
    cnt_ref[...] = carry_scr[...]


def _route(lg_all, bias_row):
    T = lg_all.shape[0]
    tm = ROUTE_TM
    return pl.pallas_call(
        _route_kernel,
        out_shape=[jax.ShapeDtypeStruct((T, LANES), F32), jax.ShapeDtypeStruct((1, LANES), F32)],
        grid=(T // tm,),
        in_specs=[pl.BlockSpec((tm, LANES), lambda i: (i, 0)), pl.BlockSpec((1, LANES), lambda i: (0, 0))],
        out_specs=[pl.BlockSpec((tm, LANES), lambda i: (i, 0)), pl.BlockSpec((1, LANES), lambda i: (0, 0))],
        scratch_shapes=[pltpu.VMEM((tm, tm), BF16), pltpu.VMEM((1, LANES), F32)],
        compiler_params=_cparams(("arbitrary",)),
        name="route",
    )(lg_all, bias_row)


WEIGHT_DMA_PRIORITY = 1


def _moe_kernel(dest_ref, bexp_ref, nxt_ref, stage_ref, pad_ref, nused_ref, hp_hbm, wg_hbm, wu_hbm, wd_hbm,
                out_ref, xbuf, gsem, wgbuf, wubuf, wdbuf, wsem, wg_bf, wu_bf, wd_bf, wslot, tok_ref, *,
                n_blocks):
    b = pl.program_id(0)
    n_used = nused_ref[0]
    slot = b % 2

    def gather_copy(row, s, r):
        return pltpu.make_async_copy(hp_hbm.at[pl.ds(row, 1)], xbuf.at[s, pl.ds(r, 1)], gsem.at[s])

    def weight_copy(m, e, s):
        hbm, buf = ((wg_hbm, wgbuf), (wu_hbm, wubuf), (wd_hbm, wdbuf))[m]
        return pltpu.make_async_copy(hbm.at[e], buf.at[s], wsem.at[s])

    def weight_copies(e, s):
        return tuple(weight_copy(m, e, s) for m in range(3))

    def issue(blk, s):
        base = blk * EXPERT_BLOCK

        def body(r, carry):
            gather_copy(tok_ref[base + r], s, r).start()
            return carry

        lax.fori_loop(0, EXPERT_BLOCK, body, 0, unroll=8)

    @pl.when(b == 0)
    def _():
        for cp in weight_copies(bexp_ref[0], 0):
            cp.start(priority=WEIGHT_DMA_PRIORITY)
        wslot[0] = 0

        def zero_rows(lo, hi):
            def zero_body(r, carry):
                tok_ref[r] = 0
                return carry

            lax.fori_loop(lo, hi, zero_body, 0)

        def expert_body(e, carry):
            zero_rows(pad_ref[e], pad_ref[N_EXPERTS + e])
            return carry

        lax.fori_loop(0, N_EXPERTS, expert_body, 0)
        tail = n_used * EXPERT_BLOCK
        zero_rows(tail, jnp.minimum(tail + EXPERT_BLOCK, n_blocks * EXPERT_BLOCK))

        def fill_body(a, carry):
            tok_ref[dest_ref[a]] = a >> 1
            return carry

        lax.fori_loop(0, dest_ref.shape[0], fill_body, 0, unroll=8)
        issue(0, 0)

    @pl.when(b < n_used)
    def _():
        e = bexp_ref[b]
        new_expert = (b == 0) | (e != bexp_ref[jnp.maximum(b - 1, 0)])

        @pl.when(new_expert)
        def _():
            @pl.when(b > 0)
            def _():
                wslot[0] = 1 - wslot[0]

            s = wslot[0]
            for cp in weight_copies(e, s):
                cp.wait()
            wg_bf[...] = wgbuf[s].astype(BF16)
            wu_bf[...] = wubuf[s].astype(BF16)
            wd_bf[...] = wdbuf[s].astype(BF16)

        stage = stage_ref[b]
        nxt = nxt_ref[b]
        other = 1 - wslot[0]
        for m in range(3):
            @pl.when(((stage >> m) & 1) == 1)
            def _(m=m):
                weight_copy(m, nxt, other).start(priority=WEIGHT_DMA_PRIORITY)

    def expert_block(cur):
        pltpu.make_async_copy(hp_hbm.at[pl.ds(0, EXPERT_BLOCK)], xbuf.at[cur], gsem.at[cur]).wait()
        base = jnp.minimum(b + 1, n_blocks - 1) * EXPERT_BLOCK
        for r in range(EXPERT_BLOCK):
            gather_copy(tok_ref[base + r], 1 - cur, r).start()
        h = _unpack_bf16_pairs(xbuf[cur]).astype(BF16)
        g = jnp.dot(h, wg_bf[...], preferred_element_type=F32)
        u = jnp.dot(h, wu_bf[...], preferred_element_type=F32)
        a = (g * jax.nn.sigmoid(g) * u).astype(BF16)
        out_ref[...] = _pack_bf16_pairs(jnp.dot(a, wd_bf[...], preferred_element_type=F32))

    for cur in range(2):
        pl.when((b < n_used) & (slot == cur))(functools.partial(expert_block, cur))

    @pl.when(b == n_used)
    def _():
        pltpu.make_async_copy(hp_hbm.at[pl.ds(0, EXPERT_BLOCK)], xbuf.at[slot], gsem.at[slot]).wait()

    @pl.when((b >= n_used) & (b < n_blocks))
    def _():
        out_ref[...] = jnp.zeros_like(out_ref)


def _moe(dest, block_expert, next_expert, prefetch_stage, pad_bounds, n_used, h_packed, wg, wu, wd):
    n_blocks = block_expert.shape[0]
    n_rows = n_blocks * EXPERT_BLOCK
    half = D_MODEL // 2
    grid_spec = pltpu.PrefetchScalarGridSpec(
        num_scalar_prefetch=6,
        grid=(n_blocks + 1,),
        in_specs=[
            pl.BlockSpec(memory_space=pl.ANY),
            pl.BlockSpec(memory_space=pl.ANY),
            pl.BlockSpec(memory_space=pl.ANY),
            pl.BlockSpec(memory_space=pl.ANY),
        ],
        out_specs=pl.BlockSpec((EXPERT_BLOCK, half), lambda b, *_: (jnp.minimum(b, n_blocks - 1), 0)),
        scratch_shapes=[
            pltpu.VMEM((2, EXPERT_BLOCK, half), jnp.uint32),
            pltpu.SemaphoreType.DMA((2,)),
            pltpu.VMEM((2, D_MODEL, EXPERT_HIDDEN), F32),
            pltpu.VMEM((2, D_MODEL, EXPERT_HIDDEN), F32),
            pltpu.VMEM((2, EXPERT_HIDDEN, D_MODEL), F32),
            pltpu.SemaphoreType.DMA((2,)),
            pltpu.VMEM((D_MODEL, EXPERT_HIDDEN), BF16),
            pltpu.VMEM((D_MODEL, EXPERT_HIDDEN), BF16),
            pltpu.VMEM((EXPERT_HIDDEN, D_MODEL), BF16),
            pltpu.SMEM((1,), jnp.int32),
            pltpu.SMEM((n_rows,), jnp.int32),
        ],
    )
    return pl.pallas_call(
        functools.partial(_moe_kernel, n_blocks=n_blocks),
        out_shape=jax.ShapeDtypeStruct((n_rows, half), jnp.uint32),
        grid_spec=grid_spec,
        compiler_params=_cparams(("arbitrary",)),
        name="moe",
    )(dest, block_expert, next_expert, prefetch_stage, pad_bounds, n_used, h_packed, wg, wu, wd)


FINAL_TM = 256


def _final_kernel(dest_ref, x1_ref, info_ref, gain_ref, rows_hbm, y_ref, rbuf, sem, *, tok_offset):
    i = pl.program_id(0)
    n = pl.num_programs(0)
    tm = FINAL_TM
    slot = i % 2

    def issue(blk, s):
        base = 2 * (tok_offset + blk * tm)
        for r in range(tm):
            for kk in range(2):
                pltpu.make_async_copy(rows_hbm.at[pl.ds(dest_ref[base + 2 * r + kk], 1)],
                                      rbuf.at[s, kk, pl.ds(r, 1)], sem.at[s]).start(priority=kk)

    @pl.when(i == 0)
    def _():
        issue(0, 0)

    @pl.when(i + 1 < n)
    def _():
        issue(i + 1, 1 - slot)

    for kk in range(2):
        pltpu.make_async_copy(rows_hbm.at[pl.ds(0, tm)], rbuf.at[slot, kk], sem.at[slot]).wait()
    info = info_ref[...]
    x = (x1_ref[...] + info[:, 2:3] * _unpack_bf16_pairs(rbuf[slot, 0])
         + info[:, 3:4] * _unpack_bf16_pairs(rbuf[slot, 1]))
    y_ref[...] = x * lax.rsqrt(jnp.mean(x * x, axis=-1, keepdims=True) + NORM_EPS) * gain_ref[...]


def _final(dest, x1_all, info, gain, out_rows, *, tok_offset, n_tok):
    tm = FINAL_TM
    off = tok_offset // tm
    grid_spec = pltpu.PrefetchScalarGridSpec(
        num_scalar_prefetch=1,
        grid=(n_tok // tm,),
        in_specs=[
            pl.BlockSpec((tm, D_MODEL), lambda i, d: (i + off, 0)),
            pl.BlockSpec((tm, LANES), lambda i, d: (i + off, 0)),
            pl.BlockSpec((1, D_MODEL), lambda i, d: (0, 0)),
            pl.BlockSpec(memory_space=pl.ANY),
        ],
        out_specs=pl.BlockSpec((tm, D_MODEL), lambda i, d: (i, 0)),
        scratch_shapes=[pltpu.VMEM((2, 2, tm, D_MODEL // 2), jnp.uint32), pltpu.SemaphoreType.DMA((2,))],
    )
    return pl.pallas_call(
        functools.partial(_final_kernel, tok_offset=tok_offset),
        out_shape=jax.ShapeDtypeStruct((n_tok, D_MODEL), F32),
        grid_spec=grid_spec,
        compiler_params=_cparams(("arbitrary",)),
        name="final",
    )(dest, x1_all, info, gain, out_rows)


def _rot_table(pos, n_rows):
    half = RET_KEY_DIM // 2
    theta = 10000.0 ** (-jnp.linspace(0.0, 1.0, half, dtype=F32))
    ang = pos.astype(F32)[:, None] * theta[None, :]
    cos, sin = jnp.cos(ang), jnp.sin(ang)
    cos128 = jnp.tile(cos, (1, 4))
    sin128 = jnp.tile(jnp.concatenate([-sin, sin], axis=1), (1, 2))
    k_scale = RET_KEY_DIM ** -0.5
    tab = jnp.concatenate([cos128, sin128, cos128 * k_scale, sin128 * k_scale], axis=1)
    return jnp.tile(tab, (n_rows // tab.shape[0], 1))


def kernel(x_prompt, x_sample, cache_diff_k, cache_diff_v, state_retention, norm_mix_gain, w_in, lambda_q1,
           lambda_k1, lambda_q2, lambda_k2, diff_subln_gain, w_ret_out, w_diff_out, w_out, rel_bias_table,
           norm_ffn_gain, w_group_router, b_group_router, w_expert_router, b_expert_router, w_expert_gate,
           w_expert_up, w_expert_down, norm_final_gain):
    B, S, D = x_prompt.shape
    BS, L, _ = x_sample.shape
    past = cache_diff_k.shape[2]
    TP, TS = B * S, BS * L
    T = TP + TS

    w_in_bf = w_in[0].astype(BF16)
    wr_bf = w_ret_out[0].astype(BF16)
    wd_bf = w_diff_out[0].astype(BF16)
    wo_bf = w_out[0].astype(BF16)
    gain_mix = norm_mix_gain[0][None, :]
    gain_ffn = norm_ffn_gain[0][None, :]
    gain_fin = norm_final_gain[None, :]
    gain_sub = diff_subln_gain[0][None, :]
    lam_rows = jnp.zeros((8, LANES), F32).at[0:4, 0:DIFF_HEAD_DIM].set(
        jnp.stack([lambda_q1[0], lambda_k1[0], lambda_q2[0], lambda_k2[0]]))
    rot_p = _rot_table(jnp.arange(S), S)
    rot_s = _rot_table(past + jnp.arange(L), PROJ_TM)

    far = _far_bias(rel_bias_table)[:, None, None]
    qp = ATT_TQ + jnp.arange(ATT_TQ)
    def tile_bias(k_pos):
        bias = (_relative_bias(qp, k_pos, rel_bias_table) - far) * LOG2E
        visible = (k_pos[None, :] // CHUNK) <= (qp[:, None] // CHUNK)
        bias = jnp.swapaxes(jnp.where(visible[None], bias, NEG_INF), 1, 2)
        return jnp.concatenate([bias, bias], axis=2)
    bias_prev = tile_bias(jnp.arange(ATT_TK))
    bias_diag = tile_bias(ATT_TQ + jnp.arange(ATT_TK))
    q_pos_s = past + jnp.arange(L)
    assert SAMPLE_NEAR - L >= MAX_DISTANCE
    b_last = (_relative_bias(q_pos_s, past - SAMPLE_NEAR + jnp.arange(SAMPLE_NEAR), rel_bias_table) - far) * LOG2E
    bias_past = jnp.concatenate([b_last, b_last], axis=1)
    b_new = (_relative_bias(q_pos_s, q_pos_s, rel_bias_table) - far) * LOG2E
    bias_new = jnp.concatenate([b_new, b_new], axis=1)

    xp2 = x_prompt.reshape(TP, D)
    xs2 = x_sample.reshape(TS, D)
    z3p, gp, gatesp, v32p, qt, vt5, kt32 = _proj(xp2, gain_mix, w_in_bf, rot_p, seq=S, transposed_k=True)
    z3s, gs, gatess, v32s, k32s = _proj(xs2, gain_mix, w_in_bf, rot_s, seq=L, transposed_k=False)

    zero_state = jnp.zeros((B, N_HEADS, RET_KEY_DIM, HEAD_V), F32)
    orp, ret_p = _retention(z3p, zero_state, batch=B, seq=S, chunk=256)
    ors, ret_s = _retention(z3s, state_retention[0], batch=BS, seq=L, chunk=L)

    gain_col = jnp.broadcast_to(diff_subln_gain[0][:, None], (HEAD_V, ATT_TQ))
    odp = _attn_prompt(z3p, qt, vt5, bias_prev, bias_diag, lam_rows, gain_col, batch=B, seq=S)
    kc = jnp.transpose(cache_diff_k[0], (0, 2, 3, 4, 1)).reshape(BS, N_HEADS, LANES, past)
    vc = cache_diff_v[0].reshape(BS, past * N_HEADS, LANES)
    ods = _attn_sample(z3s, kc, vc, bias_past, bias_new, lam_rows, gain_sub, batch=BS, n_q=L, past=past)

    mgp = _post_a(orp, gp, odp, gatesp, wr_bf, wd_bf)
    mgs = _post_a(ors, gs, ods, gatess, wr_bf, wd_bf)

    w_rt = jnp.zeros((D, LANES), F32)
    w_rt = w_rt.at[:, 0:N_GROUPS].set(w_group_router[0])
    w_rt = w_rt.at[:, N_GROUPS:N_GROUPS + N_EXPERTS].set(
        jnp.transpose(w_expert_router[0], (1, 0, 2)).reshape(D, N_EXPERTS))
    w_rt_hi = w_rt.astype(BF16)
    w_rt_lo = (w_rt - w_rt_hi.astype(F32)).astype(BF16)
    w_router2 = jnp.concatenate([w_rt_hi, w_rt_lo], axis=1)
    bias_row = jnp.zeros((1, LANES), F32)
    bias_row = bias_row.at[0, 0:N_GROUPS].set(b_group_router[0])
    bias_row = bias_row.at[0, N_GROUPS:N_GROUPS + N_EXPERTS].set(b_expert_router[0].reshape(-1))

    x1_all, lg_all, h_packed = _post_b(xp2, mgp, xs2, mgs, wo_bf, gain_ffn, w_router2)

    info, counts = _route(lg_all, bias_row)
    counts = counts[0, :N_EXPERTS].astype(jnp.int32)
    padded = (counts + EXPERT_BLOCK - 1) // EXPERT_BLOCK * EXPERT_BLOCK
    pad_end = jnp.cumsum(padded)
    offs = pad_end - padded
    e12 = info[:, 0:2].astype(jnp.int32)
    rank12 = info[:, 4:6].astype(jnp.int32)
    expert_ids = jnp.arange(N_EXPERTS, dtype=jnp.int32)
    offs_e = jnp.sum(jnp.where(e12[:, :, None] == expert_ids, offs, 0), axis=-1)
    dest = (offs_e + rank12).reshape(-1)
    n_assign = 2 * T
    n_rows = -(-n_assign // EXPERT_BLOCK) * EXPERT_BLOCK + N_EXPERTS * EXPERT_BLOCK
    n_blocks = n_rows // EXPERT_BLOCK
    block_start = jnp.arange(n_blocks, dtype=jnp.int32) * EXPERT_BLOCK
    block_expert = jnp.minimum(jnp.sum((pad_end[None, :] <= block_start[:, None]).astype(jnp.int32), axis=1),
                               N_EXPERTS - 1)
    n_used = (pad_end[-1:] // EXPERT_BLOCK).astype(jnp.int32)
    later_used = (expert_ids[None, :] > expert_ids[:, None]) & (counts[None, :] > 0)
    next_of_expert = jnp.min(jnp.where(later_used, expert_ids[None, :], N_EXPERTS), axis=1)
    next_of_expert = jnp.where(next_of_expert == N_EXPERTS, -1, next_of_expert)
    next_expert = jnp.sum(jnp.where(block_expert[:, None] == expert_ids[None, :], next_of_expert[None, :], 0),
                          axis=1).astype(jnp.int32)

    pad_bounds = jnp.concatenate([offs + counts, pad_end]).astype(jnp.int32)
    onehot_be = block_expert[:, None] == expert_ids[None, :]
    first_block = jnp.sum(jnp.where(onehot_be, (offs // EXPERT_BLOCK)[None, :], 0), axis=1)
    blocks_of = jnp.sum(jnp.where(onehot_be, (padded // EXPERT_BLOCK)[None, :], 0), axis=1)
    k_in_expert = jnp.arange(n_blocks, dtype=jnp.int32) - first_block
    prefetch_stage = sum(
        jnp.where((jnp.minimum(m, blocks_of - 1) == k_in_expert) & (next_expert >= 0), 1 << m, 0)
        for m in range(3)).astype(jnp.int32)
    out_rows = _moe(dest, block_expert, next_expert, prefetch_stage, pad_bounds, n_used, h_packed,
                    w_expert_gate[0], w_expert_up[0], w_expert_down[0])

    y_p = _final(dest, x1_all, info, gain_fin, out_rows, tok_offset=0, n_tok=TP)
    y_s = _final(dest, x1_all, info, gain_fin, out_rows, tok_offset=TP, n_tok=TS)

    new_k_p = jnp.transpose(kt32, (0, 1, 5, 2, 3, 4))
    new_v_p = v32p.reshape(1, B, S, N_HEADS, HEAD_V)
    new_k_s = k32s.reshape(1, BS, L, N_HEADS, 2, DIFF_HEAD_DIM)
    new_v_s = v32s.reshape(1, BS, L, N_HEADS, HEAD_V)
    return (y_p.reshape(B, S, D), y_s.reshape(BS, L, D), new_k_p, new_v_p, ret_p[None],
            new_k_s, new_v_s, ret_s[None])
```

```python
import functools
import math

import jax
import jax.numpy as jnp
from jax import lax
from jax.experimental import pallas as pl
from jax.experimental.pallas import tpu as pltpu

F32 = jnp.float32
BF16 = jnp.bfloat16

D_MODEL = 2048
CHUNK = 64
N_HEADS = 8
RET_KEY_DIM = 64
HEAD_V = 128
DIFF_HEAD_DIM = 64
IN_WIDTH = 10240
NUM_BUCKETS = 32
MAX_DISTANCE = 128
N_GROUPS = 8
EXPERTS_PER_GROUP = 8
N_EXPERTS = 64
EXPERT_HIDDEN = 512
EXPERT_BLOCK = 128
NORM_EPS = 1e-6
NEG_INF = -1e30
LANES = 128
LOG2E = 1.4426950408889634
VT_ROWS = HEAD_V + 16

VMEM_LIMIT = 56 * 1024 * 1024


def _cparams(sem):
    return pltpu.CompilerParams(dimension_semantics=sem, vmem_limit_bytes=VMEM_LIMIT)


PROJ_TM = 512
PROJ_TN = 1024
PROJ_MC = 128
ATT_TK = 256


def _rotate_pairs(acc, cos, sin):
    outs = []
    lane = lax.broadcasted_iota(jnp.int32, (acc.shape[0], LANES), 1)
    first_half = (lane % 64) < 32
    for c in range(acc.shape[1] // LANES):
        xs = acc[:, c * LANES:(c + 1) * LANES]
        swapped = jnp.where(first_half, pltpu.roll(xs, 96, axis=1), pltpu.roll(xs, 32, axis=1))
        outs.append(xs * cos + swapped * sin)
    return outs


def _proj_kernel(x_ref, gain_ref, w_ref, rot_ref, *out_refs, transposed_k):
    if transposed_k:
        z_ref, g_ref, gates_ref, v32_ref, qt_ref, vt_ref, kt32_ref, h_scr = out_refs
    else:
        z_ref, g_ref, gates_ref, v32_ref, k32_ref, h_scr = out_refs
    j = pl.program_id(1)

    @pl.when(j == 0)
    def _():
        x = x_ref[...]
        ms = jnp.mean(x * x, axis=-1, keepdims=True)
        h_scr[...] = (x * lax.rsqrt(ms + NORM_EPS) * gain_ref[...]).astype(BF16)

    n_slab = PROJ_TN // LANES

    def for_row_chunks(epilogue):
        for c in range(PROJ_TM // PROJ_MC):
            rows = slice(c * PROJ_MC, (c + 1) * PROJ_MC)
            epilogue(c, rows, jnp.dot(h_scr[rows, :], w_ref[...], preferred_element_type=F32))

    def store_slabs(rows, vals):
        for s in range(n_slab):
            z_ref[s, rows, :] = vals[s].astype(BF16)

    def split(a):
        return [a[:, s * LANES:(s + 1) * LANES] for s in range(n_slab)]

    @pl.when(j == 0)
    def _():
        def epilogue(c, rows, acc):
            half = PROJ_TN // 2
            q = _rotate_pairs(acc[:, :half], rot_ref[rows, 0:128], rot_ref[rows, 128:256])
            k = _rotate_pairs(acc[:, half:], rot_ref[rows, 256:384], rot_ref[rows, 384:512])
            store_slabs(rows, q + k)
        for_row_chunks(epilogue)

    @pl.when(j == 1)
    def _():
        for_row_chunks(lambda c, rows, acc: store_slabs(rows, split(acc)))

    @pl.when(j == 2)
    def _():
        def epilogue(c, rows, acc):
            g_ref[rows, :] = (acc * jax.nn.sigmoid(acc)).astype(BF16)
        for_row_chunks(epilogue)

    @pl.when(j == 3)
    def _():
        def epilogue(c, rows, acc):
            qs = acc * (DIFF_HEAD_DIM ** -0.5 * LOG2E)
            store_slabs(rows, split(qs))
            if transposed_k:
                qs_t = qs.T.astype(BF16)
                for hh in range(n_slab):
                    qt_ref[0, hh, :, rows] = qs_t[hh * LANES:(hh + 1) * LANES, :]
        for_row_chunks(epilogue)

    @pl.when(j == 4)
    def _():
        def epilogue(c, rows, acc):
            store_slabs(rows, split(acc))
            if transposed_k:
                kt32_ref[0, 0, :, :, :, rows] = acc.T.reshape(n_slab, 2, DIFF_HEAD_DIM, PROJ_MC)
            else:
                k32_ref[rows, :] = acc
        for_row_chunks(epilogue)

    @pl.when(j == 5)
    def _():
        def epilogue(c, rows, acc):
            store_slabs(rows, split(acc))
            v32_ref[rows, :] = acc
            if transposed_k:
                acc_tb = acc.T.astype(BF16)
                t, cols = (c * PROJ_MC) // ATT_TK, slice((c * PROJ_MC) % ATT_TK, (c * PROJ_MC) % ATT_TK + PROJ_MC)
                ones = jnp.ones((VT_ROWS - HEAD_V, PROJ_MC), BF16)
                for hh in range(n_slab):
                    vt_ref[0, hh, t, 0:HEAD_V, cols] = acc_tb[hh * LANES:(hh + 1) * LANES, :]
                    vt_ref[0, hh, t, HEAD_V:VT_ROWS, cols] = ones
        for_row_chunks(epilogue)

    @pl.when(j >= 6)
    def _():
        def epilogue(c, rows, acc):
            gates_ref[rows, :] = jax.nn.sigmoid(acc).astype(BF16)
        for_row_chunks(epilogue)


def _proj(x2d, gain, w_bf, rot, *, seq, transposed_k):
    T = x2d.shape[0]
    tm, tn = PROJ_TM, PROJ_TN
    ni, nj = T // tm, IN_WIDTH // tn
    rot_blocks = rot.shape[0] // tm

    def zmap(i, j):
        zj = jnp.where(j < 2, j, jnp.where(j < 3, 1, jnp.where(j < 6, j - 1, 4)))
        return (zj, i, 0)

    out_shape = [
        jax.ShapeDtypeStruct((40, T, LANES), BF16),
        jax.ShapeDtypeStruct((T, 1024), BF16),
        jax.ShapeDtypeStruct((T, 4096), BF16),
        jax.ShapeDtypeStruct((T, 1024), F32),
    ]
    out_specs = [
        pl.BlockSpec((8, tm, LANES), zmap),
        pl.BlockSpec((tm, tn), lambda i, j: (i, 0)),
        pl.BlockSpec((tm, tn), lambda i, j: (i, jnp.clip(j - 6, 0, 3))),
        pl.BlockSpec((tm, tn), lambda i, j: (i, 0)),
    ]
    if transposed_k:
        B = T // seq
        spb = seq // tm
        out_shape += [
            jax.ShapeDtypeStruct((B, N_HEADS, LANES, seq), BF16),
            jax.ShapeDtypeStruct((B, N_HEADS, seq // ATT_TK, VT_ROWS, ATT_TK), BF16),
            jax.ShapeDtypeStruct((1, B, N_HEADS, 2, DIFF_HEAD_DIM, seq), F32),
        ]
        out_specs += [
            pl.BlockSpec((1, N_HEADS, LANES, tm), lambda i, j: (i // spb, 0, 0, i % spb)),
            pl.BlockSpec((1, N_HEADS, tm // ATT_TK, VT_ROWS, ATT_TK), lambda i, j: (i // spb, 0, i % spb, 0, 0)),
            pl.BlockSpec((1, 1, N_HEADS, 2, DIFF_HEAD_DIM, tm), lambda i, j: (0, i // spb, 0, 0, 0, i % spb)),
        ]
    else:
        out_shape += [jax.ShapeDtypeStruct((T, 1024), F32)]
        out_specs += [pl.BlockSpec((tm, tn), lambda i, j: (i, 0))]

    return pl.pallas_call(
        functools.partial(_proj_kernel, transposed_k=transposed_k),
        out_shape=out_shape,
        grid=(ni, nj),
        in_specs=[
            pl.BlockSpec((tm, D_MODEL), lambda i, j: (i, 0)),
            pl.BlockSpec((1, D_MODEL), lambda i, j: (0, 0)),
            pl.BlockSpec((D_MODEL, tn), lambda i, j: (0, j)),
            pl.BlockSpec((tm, 512), lambda i, j: (i % rot_blocks, 0)),
        ],
        out_specs=out_specs,
        scratch_shapes=[pltpu.VMEM((tm, D_MODEL), BF16)],
        compiler_params=_cparams(("arbitrary", "arbitrary")),
        name="proj_t" if transposed_k else "proj",
    )(x2d, gain, w_bf, rot)


def _retention_kernel(q_ref, k_ref, v_ref, decay_ref, inner_ref, outer_ref, st0_ref,
                      o_ref, st_out_ref, st_scr, *, chunk):
    c = pl.program_id(1)
    nc = pl.num_programs(1)

    @pl.when(c == 0)
    def _():
        st_scr[...] = jnp.zeros_like(st_scr)
        for h in range(N_HEADS):
            a = h % 2
            st_scr[h, a * RET_KEY_DIM:(a + 1) * RET_KEY_DIM, :] = st0_ref[0, h]

    lane_lo = lax.broadcasted_iota(jnp.int32, (chunk, LANES), 1) < RET_KEY_DIM
    row_lo = lax.broadcasted_iota(jnp.int32, (LANES, LANES), 0) < RET_KEY_DIM
    for h in range(N_HEADS):
        p, a = h // 2, h % 2
        q = q_ref[p]
        k = k_ref[p]
        v = v_ref[h]
        qa = jnp.where(lane_lo == (a == 0), q, jnp.zeros_like(q))
        s = lax.dot_general(qa, k, (((1,), (1,)), ((), ())), preferred_element_type=F32)
        s = s * decay_ref[h]
        st = st_scr[h]
        inner = inner_ref[h]
        o = (jnp.dot(s.astype(BF16), v, preferred_element_type=F32)
             + jnp.dot(qa, st.astype(BF16), preferred_element_type=F32) * inner)
        ko = (k.astype(F32) * outer_ref[p]).astype(BF16)
        upd = lax.dot_general(ko, v, (((0,), (0,)), ((), ())), preferred_element_type=F32)
        upd = jnp.where(row_lo == (a == 0), upd, 0.0)
        chunk_decay = inner[chunk - 1:chunk, :]
        st_scr[h] = chunk_decay * st + upd
        o_n = o * lax.rsqrt(jnp.mean(o * o, axis=-1, keepdims=True) + NORM_EPS)
        o_ref[:, h * HEAD_V:(h + 1) * HEAD_V] = o_n.astype(BF16)

    @pl.when(c == nc - 1)
    def _():
        for h in range(N_HEADS):
            a = h % 2
            st_out_ref[0, h] = st_scr[h, a * RET_KEY_DIM:(a + 1) * RET_KEY_DIM, :]


def _retention_consts(chunk):
    log_g = jnp.log(1.0 - 2.0 ** (-5.0 - jnp.arange(N_HEADS, dtype=F32)))
    j = jnp.arange(chunk, dtype=F32)
    diff = j[:, None] - j[None, :]
    decay = jnp.where(diff >= 0, jnp.exp(log_g[:, None, None] * jnp.maximum(diff, 0.0)), 0.0)
    inner = jnp.exp(log_g[:, None] * (j + 1.0))
    outer = jnp.exp(log_g[:, None] * (chunk - 1.0 - j))
    inner_b = jnp.broadcast_to(inner[:, :, None], (N_HEADS, chunk, LANES))
    outer_pair = jnp.repeat(outer.reshape(N_HEADS // 2, 2, chunk).transpose(0, 2, 1), RET_KEY_DIM, axis=2)
    return decay, inner_b, outer_pair


def _retention(z3, state0, *, batch, seq, chunk):
    T = batch * seq
    nc = seq // chunk
    decay, inner_b, outer_pair = _retention_consts(chunk)
    return pl.pallas_call(
        functools.partial(_retention_kernel, chunk=chunk),
        out_shape=[jax.ShapeDtypeStruct((T, 1024), BF16),
                   jax.ShapeDtypeStruct((batch, N_HEADS, RET_KEY_DIM, HEAD_V), F32)],
        grid=(batch, nc),
        in_specs=[
            pl.BlockSpec((4, chunk, LANES), lambda b, c: (0, b * nc + c, 0)),
            pl.BlockSpec((4, chunk, LANES), lambda b, c: (1, b * nc + c, 0)),
            pl.BlockSpec((8, chunk, LANES), lambda b, c: (1, b * nc + c, 0)),
            pl.BlockSpec((N_HEADS, chunk, chunk), lambda b, c: (0, 0, 0)),
            pl.BlockSpec((N_HEADS, chunk, LANES), lambda b, c: (0, 0, 0)),
            pl.BlockSpec((N_HEADS // 2, chunk, LANES), lambda b, c: (0, 0, 0)),
            pl.BlockSpec((1, N_HEADS, RET_KEY_DIM, HEAD_V), lambda b, c: (b, 0, 0, 0)),
        ],
        out_specs=[
            pl.BlockSpec((chunk, 1024), lambda b, c: (b * nc + c, 0)),
            pl.BlockSpec((1, N_HEADS, RET_KEY_DIM, HEAD_V), lambda b, c: (b, 0, 0, 0)),
        ],
        scratch_shapes=[pltpu.VMEM((N_HEADS, LANES, LANES), F32)],
        compiler_params=_cparams(("arbitrary", "arbitrary")),
        name=f"retention_c{chunk}",
    )(z3, z3, z3, decay, inner_b, outer_pair, state0)


def _lam_init(layer=0):
    return 0.8 - 0.6 * math.exp(-0.3 * layer)


def _lam_from_ref(lam_ref):
    lp = lam_ref[...]
    s1 = jnp.sum(lp[0:1] * lp[1:2], axis=-1, keepdims=True)
    s2 = jnp.sum(lp[2:3] * lp[3:4], axis=-1, keepdims=True)
    return jnp.exp(s1) - jnp.exp(s2) + _lam_init()


def _stack_q(q):
    lane_lo = lax.broadcasted_iota(jnp.int32, q.shape, 1) < DIFF_HEAD_DIM
    zero = jnp.zeros_like(q)
    return jnp.concatenate([jnp.where(lane_lo, q, zero), jnp.where(lane_lo, zero, q)], axis=0)


def _with_ones(v):
    return jnp.concatenate([v, jnp.ones_like(v)], axis=1)


def _diff_finish(acc_ref, lam, gain, n):
    o = acc_ref[:, 0:HEAD_V] / acc_ref[:, HEAD_V:2 * HEAD_V]
    o = o[:n] - lam * o[n:]
    o = o * lax.rsqrt(jnp.mean(o * o, axis=-1, keepdims=True) + NORM_EPS) * gain
    return o * (1.0 - _lam_init())


def _relative_bias(q_pos, k_pos, table):
    rel = k_pos[None, :] - q_pos[:, None]
    half = NUM_BUCKETS // 2
    max_exact = half // 2
    n = jnp.abs(rel)
    log_ratio = jnp.log(jnp.maximum(n, 1).astype(F32) / max_exact) / math.log(MAX_DISTANCE / max_exact)
    large = jnp.minimum(max_exact + (log_ratio * (half - max_exact)).astype(jnp.int32), half - 1)
    bucket = (rel > 0).astype(jnp.int32) * half + jnp.where(n < max_exact, n, large)
    out = jnp.zeros((table.shape[1],) + bucket.shape, F32)
    for bkt in range(NUM_BUCKETS):
        out = jnp.where(bucket[None] == bkt, table[bkt].astype(F32)[:, None, None], out)
    return out


def _far_bias(table):
    return table[NUM_BUCKETS // 2 - 1].astype(F32)


ATT_TQ = 256
ATT_HP = 8


def _attn_prompt_kernel(qt_ref, k_ref, vt_ref, bprev_ref, bdiag_ref, lam_ref, gain_ref, o_ref, acc_scr):
    i = pl.program_id(2)
    row_lo = lax.broadcasted_iota(jnp.int32, (LANES, ATT_TQ), 0) < DIFF_HEAD_DIM
    qs = []
    for hh in range(ATT_HP):
        qt = qt_ref[0, hh]
        zero = jnp.zeros_like(qt)
        qs.append(jnp.concatenate([jnp.where(row_lo, qt, zero), jnp.where(row_lo, zero, qt)], axis=1))
    acc_scr[...] = jnp.zeros_like(acc_scr)

    def tile(j, ms, biases):
        ss = []
        for hh in range(ATT_HP):
            kt = k_ref[hh, pl.ds(pl.multiple_of(j * ATT_TK, ATT_TK), ATT_TK), :]
            s = jnp.dot(kt, qs[hh], preferred_element_type=F32)
            ss.append(s if biases is None else s + biases[hh])
        ps, alphas, m_out = [], [], []
        for hh in range(ATT_HP):
            m_new = jnp.maximum(ms[hh], jnp.max(ss[hh], axis=0, keepdims=True))
            alphas.append(jnp.exp2(ms[hh] - m_new))
            ps.append(jnp.exp2(ss[hh] - m_new).astype(BF16))
            m_out.append(m_new)
        for hh in range(ATT_HP):
            acc_scr[hh] = alphas[hh] * acc_scr[hh] + jnp.dot(vt_ref[0, hh, j], ps[hh],
                                                             preferred_element_type=F32)
        return tuple(m_out)

    ms = tuple(jnp.full((1, 2 * ATT_TQ), NEG_INF, F32) for _ in range(ATT_HP))
    ms = lax.fori_loop(0, i - 1, lambda j, c: tile(j, c, None), ms)
    jp = jnp.maximum(i - 1, 0)
    ms = lax.cond(i >= 1,
                  lambda c: tile(jp, c, [bprev_ref[hh] for hh in range(ATT_HP)]),
                  lambda c: c, ms)
    tile(i, ms, [bdiag_ref[hh] for hh in range(ATT_HP)])

    lam = _lam_from_ref(lam_ref)
    for hh in range(ATT_HP):
        o = acc_scr[hh, 0:HEAD_V, :] / acc_scr[hh, HEAD_V:HEAD_V + 1, :]
        o = o[:, :ATT_TQ] - lam * o[:, ATT_TQ:]
        o = o * lax.rsqrt(jnp.mean(o * o, axis=0, keepdims=True) + NORM_EPS) * gain_ref[...]
        o_ref[:, hh * HEAD_V:(hh + 1) * HEAD_V] = (o * (1.0 - _lam_init())).T.astype(BF16)


def _attn_prompt(z3, qt, vt5, bias_prev, bias_diag, lam_rows, gain_col, *, batch, seq):
    T = batch * seq
    nq = seq // ATT_TQ
    hp = ATT_HP
    return pl.pallas_call(
        _attn_prompt_kernel,
        out_shape=jax.ShapeDtypeStruct((T, 1024), BF16),
        grid=(batch, N_HEADS // hp, nq),
        in_specs=[
            pl.BlockSpec((1, hp, LANES, ATT_TQ), lambda b, g, i: (b, g, 0, i)),
            pl.BlockSpec((hp, seq, LANES), lambda b, g, i: (24 // hp + g, b, 0), pipeline_mode=pl.Buffered(1)),
            pl.BlockSpec((1, hp, seq // ATT_TK, VT_ROWS, ATT_TK), lambda b, g, i: (b, g, 0, 0, 0),
                         pipeline_mode=pl.Buffered(1)),
            pl.BlockSpec((hp, ATT_TK, 2 * ATT_TQ), lambda b, g, i: (g, 0, 0), pipeline_mode=pl.Buffered(1)),
            pl.BlockSpec((hp, ATT_TK, 2 * ATT_TQ), lambda b, g, i: (g, 0, 0), pipeline_mode=pl.Buffered(1)),
            pl.BlockSpec((8, LANES), lambda b, g, i: (0, 0)),
            pl.BlockSpec((LANES, ATT_TQ), lambda b, g, i: (0, 0)),
        ],
        out_specs=pl.BlockSpec((ATT_TQ, hp * HEAD_V), lambda b, g, i: (b * nq + i, g)),
        scratch_shapes=[pltpu.VMEM((hp, VT_ROWS, 2 * ATT_TQ), F32)],
        compiler_params=_cparams(("arbitrary", "arbitrary", "arbitrary")),
        name="attn_prompt",
    )(qt, z3, vt5, bias_prev, bias_diag, lam_rows, gain_col)


SAMPLE_TK = 2048
SAMPLE_NEAR = 256


def _attn_sample_kernel(q_ref, kc_ref, vc_ref, kn_ref, vn_ref, bpast_ref, bnew_ref, lam_ref, gain_ref,
                        o_ref, m_scr, acc_scr, *, n_past_tiles, n_q):
    t = pl.program_id(1)

    @pl.when(t == 0)
    def _():
        m_scr[...] = jnp.full_like(m_scr, NEG_INF)
        acc_scr[...] = jnp.zeros_like(acc_scr)

    def update(logits, values):
        ps, alphas = [], []
        for h in range(N_HEADS):
            m_prev = m_scr[h]
            m_new = jnp.maximum(m_prev, jnp.max(logits[h], axis=1, keepdims=True))
            alpha = jnp.exp2(m_prev - m_new)
            ps.append(jnp.exp2(logits[h] - m_new[:, 0:1]).astype(BF16))
            alphas.append(jnp.concatenate([alpha, alpha], axis=1))
            m_scr[h] = m_new
        for h in range(N_HEADS):
            acc_scr[h] = alphas[h] * acc_scr[h] + jnp.dot(ps[h], _with_ones(values[h]()),
                                                          preferred_element_type=F32)

    @pl.when(t < n_past_tiles)
    def _():
        is_last = jnp.where(t == n_past_tiles - 1, 1.0, 0.0)
        far_w = SAMPLE_TK - SAMPLE_NEAR
        logits = []
        for h in range(N_HEADS):
            kt = kc_ref[0, h].astype(BF16)
            s = jnp.dot(_stack_q(q_ref[h]), kt, preferred_element_type=F32)
            logits.append(jnp.concatenate([s[:, :far_w], s[:, far_w:] + is_last * bpast_ref[h]], axis=1))
        update(logits, [lambda h=h: vc_ref[0, pl.ds(h, SAMPLE_TK, stride=N_HEADS), :].astype(BF16)
                        for h in range(N_HEADS)])

    @pl.when(t == n_past_tiles)
    def _():
        logits = []
        for h in range(N_HEADS):
            s = lax.dot_general(_stack_q(q_ref[h]), kn_ref[h], (((1,), (1,)), ((), ())),
                                preferred_element_type=F32)
            logits.append(s + bnew_ref[h])
        update(logits, [lambda h=h: vn_ref[h] for h in range(N_HEADS)])
        lam = _lam_from_ref(lam_ref)
        for h in range(N_HEADS):
            o = _diff_finish(acc_scr.at[h], lam, gain_ref[...], n_q)
            o_ref[:, h * HEAD_V:(h + 1) * HEAD_V] = o.astype(BF16)


def _attn_sample(z3, kc, vc, bias_past, bias_new, lam_rows, gain, *, batch, n_q, past):
    npt = past // SAMPLE_TK
    return pl.pallas_call(
        functools.partial(_attn_sample_kernel, n_past_tiles=npt, n_q=n_q),
        out_shape=jax.ShapeDtypeStruct((batch * n_q, 1024), BF16),
        grid=(batch, npt + 1),
        in_specs=[
            pl.BlockSpec((8, n_q, LANES), lambda b, t: (2, b, 0)),
            pl.BlockSpec((1, N_HEADS, LANES, SAMPLE_TK),
                         lambda b, t: (jnp.minimum(b + t // npt, batch - 1), 0, 0, t % npt)),
            pl.BlockSpec((1, SAMPLE_TK * N_HEADS, LANES),
                         lambda b, t: (jnp.minimum(b + t // npt, batch - 1), t % npt, 0)),
            pl.BlockSpec((8, n_q, LANES), lambda b, t: (3, b, 0)),
            pl.BlockSpec((8, n_q, LANES), lambda b, t: (4, b, 0)),
            pl.BlockSpec((N_HEADS, 2 * n_q, SAMPLE_NEAR), lambda b, t: (0, 0, 0)),
            pl.BlockSpec((N_HEADS, 2 * n_q, n_q), lambda b, t: (0, 0, 0)),
            pl.BlockSpec((8, LANES), lambda b, t: (0, 0)),
            pl.BlockSpec((1, LANES), lambda b, t: (0, 0)),
        ],
        out_specs=pl.BlockSpec((n_q, 1024), lambda b, t: (b, 0)),
        scratch_shapes=[pltpu.VMEM((N_HEADS, 2 * n_q, LANES), F32),
                        pltpu.VMEM((N_HEADS, 2 * n_q, 2 * LANES), F32)],
        compiler_params=_cparams(("arbitrary", "arbitrary")),
        name="attn_sample",
    )(z3, kc, vc, z3, z3, bias_past, bias_new, lam_rows, gain)


POST_TM = 256


def _post_a_kernel(or_ref, g_ref, od_ref, gr_ref, gd_ref, wr_ref, wd_ref, out_ref):
    a = jnp.dot(or_ref[...] * g_ref[...], wr_ref[...], preferred_element_type=F32)
    b = jnp.dot(od_ref[...], wd_ref[...], preferred_element_type=F32)
    out_ref[...] = (gr_ref[...].astype(F32) * a + gd_ref[...].astype(F32) * b).astype(BF16)


def _post_a(o_r, g, o_d, gates, wr_bf, wd_bf):
    T = o_r.shape[0]
    tm = POST_TM
    return pl.pallas_call(
        _post_a_kernel,
        out_shape=jax.ShapeDtypeStruct((T, D_MODEL), BF16),
        grid=(T // tm,),
        in_specs=[
            pl.BlockSpec((tm, 1024), lambda i: (i, 0)),
            pl.BlockSpec((tm, 1024), lambda i: (i, 0)),
            pl.BlockSpec((tm, 1024), lambda i: (i, 0)),
            pl.BlockSpec((tm, D_MODEL), lambda i: (i, 0)),
            pl.BlockSpec((tm, D_MODEL), lambda i: (i, 1)),
            pl.BlockSpec((1024, D_MODEL), lambda i: (0, 0)),
            pl.BlockSpec((1024, D_MODEL), lambda i: (0, 0)),
        ],
        out_specs=pl.BlockSpec((tm, D_MODEL), lambda i: (i, 0)),
        compiler_params=_cparams(("arbitrary",)),
        name="post_a",
    )(o_r, g, o_d, gates, gates, wr_bf, wd_bf)


def _pack_bf16_pairs(x):
    w = x.shape[1] // 2
    xb = x.astype(BF16).astype(F32)
    lo = pltpu.bitcast(xb[:, :w], jnp.uint32) >> 16
    hi = pltpu.bitcast(xb[:, w:], jnp.uint32) & jnp.uint32(0xFFFF0000)
    return lo | hi


def _unpack_bf16_pairs(words):
    lo = pltpu.bitcast(words << 16, F32)
    hi = pltpu.bitcast(words & jnp.uint32(0xFFFF0000), F32)
    return jnp.concatenate([lo, hi], axis=1)


def _post_b_kernel(xp_ref, mp_ref, xs_ref, ms_ref, wo_ref, gain_ref, wrt_ref, x1_ref, lg_ref, hp_ref, *,
                   n_prompt_blocks):
    i = pl.program_id(0)
    tm = POST_TM

    def body(x_ref, mg_ref):
        x1 = x_ref[...] + jnp.dot(mg_ref[...], wo_ref[...], preferred_element_type=F32)
        x1_ref[...] = x1
        h2 = x1 * lax.rsqrt(jnp.mean(x1 * x1, axis=-1, keepdims=True) + NORM_EPS) * gain_ref[...]
        hp_ref[...] = _pack_bf16_pairs(h2)
        h_hi = h2.astype(BF16)
        h_lo = (h2 - h_hi.astype(F32)).astype(BF16)
        r = jnp.dot(jnp.concatenate([h_hi, h_lo], axis=0), wrt_ref[...], preferred_element_type=F32)
        lg_ref[...] = (r[:tm, :LANES] + r[:tm, LANES:]) + (r[tm:, :LANES] + r[tm:, LANES:])

    @pl.when(i < n_prompt_blocks)
    def _():
        body(xp_ref, mp_ref)

    @pl.when(i >= n_prompt_blocks)
    def _():
        body(xs_ref, ms_ref)


def _post_b(xp2, mgp, xs2, mgs, wo_bf, gain, w_router2):
    tm = POST_TM
    npb, nsb = xp2.shape[0] // tm, xs2.shape[0] // tm
    T = xp2.shape[0] + xs2.shape[0]
    pmap = lambda i: (jnp.minimum(i, npb - 1), 0)
    smap = lambda i: (jnp.maximum(i - npb, 0), 0)
    return pl.pallas_call(
        functools.partial(_post_b_kernel, n_prompt_blocks=npb),
        out_shape=[jax.ShapeDtypeStruct((T, D_MODEL), F32),
                   jax.ShapeDtypeStruct((T, LANES), F32),
                   jax.ShapeDtypeStruct((T, D_MODEL // 2), jnp.uint32)],
        grid=(npb + nsb,),
        in_specs=[
            pl.BlockSpec((tm, D_MODEL), pmap),
            pl.BlockSpec((tm, D_MODEL), pmap),
            pl.BlockSpec((tm, D_MODEL), smap),
            pl.BlockSpec((tm, D_MODEL), smap),
            pl.BlockSpec((D_MODEL, D_MODEL), lambda i: (0, 0)),
            pl.BlockSpec((1, D_MODEL), lambda i: (0, 0)),
            pl.BlockSpec((D_MODEL, 2 * LANES), lambda i: (0, 0)),
        ],
        out_specs=[pl.BlockSpec((tm, D_MODEL), lambda i: (i, 0)),
                   pl.BlockSpec((tm, LANES), lambda i: (i, 0)),
                   pl.BlockSpec((tm, D_MODEL // 2), lambda i: (i, 0))],
        compiler_params=_cparams(("arbitrary",)),
        name="post_b",
    )(xp2, mgp, xs2, mgs, wo_bf, gain, w_router2)


ROUTE_TM = 512


def _route_kernel(lg_ref, bias_ref, info_ref, cnt_ref, tri_scr, carry_scr):
    i = pl.program_id(0)
    tm = ROUTE_TM

    @pl.when(i == 0)
    def _():
        r = lax.broadcasted_iota(jnp.int32, (tm, tm), 0)
        c = lax.broadcasted_iota(jnp.int32, (tm, tm), 1)
        tri_scr[...] = jnp.where(c < r, 1.0, 0.0).astype(BF16)
        carry_scr[...] = jnp.zeros_like(carry_scr)

    lg = lg_ref[...] + bias_ref[...]
    lane = lax.broadcasted_iota(jnp.int32, (tm, LANES), 1)
    lane_f = lane.astype(F32)
    neg = jnp.float32(-jnp.inf)

    def first_argmax(vals):
        top = jnp.max(vals, axis=1, keepdims=True)
        idx = jnp.min(jnp.where(vals == top, lane_f, float(LANES)), axis=1, keepdims=True)
        return top, idx

    is_group = lane < N_GROUPS
    gl = jnp.where(is_group, lg, neg)
    g_top, g_idx = first_argmax(gl)
    g_weight = 1.0 / jnp.sum(jnp.exp(gl - g_top), axis=1, keepdims=True)
    lane_group = ((lane - N_GROUPS) >> 3).astype(F32)
    in_group = (lane >= N_GROUPS) & (lane < N_GROUPS + N_EXPERTS) & (lane_group == g_idx)
    el = jnp.where(in_group, lg, neg)
    t1, i1 = first_argmax(el)
    el2 = jnp.where(lane_f == i1, neg, el)
    t2, i2 = first_argmax(el2)
    e2w = jnp.exp(t2 - t1)
    p1 = 1.0 / (1.0 + e2w)
    gate1 = g_weight * p1
    gate2 = g_weight * (e2w * p1)
    e1 = i1 - float(N_GROUPS)
    e2 = i2 - float(N_GROUPS)

    hot1 = lane_f == e1
    hot2 = lane_f == e2
    both = jnp.where(hot1 | hot2, 1.0, 0.0)
    prefix = jnp.dot(tri_scr[...], both.astype(BF16), preferred_element_type=F32) + carry_scr[...]
    rank1 = jnp.sum(jnp.where(hot1, prefix, 0.0), axis=1, keepdims=True)
    rank2 = jnp.sum(jnp.where(hot2, prefix, 0.0), axis=1, keepdims=True)
    carry_scr[...] = carry_scr[...] + jnp.sum(both, axis=0, keepdims=True)

    info = jnp.where(lane == 0, e1, jnp.where(lane == 1, e2, jnp.where(lane == 2, gate1, jnp.where(
        lane == 3, gate2, jnp.where(lane == 4, rank1, jnp.where(lane == 5, rank2, 0.0))))))
    info_ref[...] = info
    cnt_ref[...] = carry_scr[...]


def _route(lg_all, bias_row):
    T = lg_all.shape[0]
    tm = ROUTE_TM
    return pl.pallas_call(
        _route_kernel,
        out_shape=[jax.ShapeDtypeStruct((T, LANES), F32), jax.ShapeDtypeStruct((1, LANES), F32)],
        grid=(T // tm,),
        in_specs=[pl.BlockSpec((tm, LANES), lambda i: (i, 0)), pl.BlockSpec((1, LANES), lambda i: (0, 0))],
        out_specs=[pl.BlockSpec((tm, LANES), lambda i: (i, 0)), pl.BlockSpec((1, LANES), lambda i: (0, 0))],
        scratch_shapes=[pltpu.VMEM((tm, tm), BF16), pltpu.VMEM((1, LANES), F32)],
        compiler_params=_cparams(("arbitrary",)),
        name="route",
    )(lg_all, bias_row)


WEIGHT_DMA_PRIORITY = 1


def _moe_kernel(dest_ref, bexp_ref, nxt_ref, stage_ref, pad_ref, nused_ref, hp_hbm, wg_hbm, wu_hbm, wd_hbm,
                out_ref, xbuf, gsem, wgbuf, wubuf, wdbuf, wsem, wg_bf, wu_bf, wd_bf, wslot, tok_ref, *,
                n_blocks):
    b = pl.program_id(0)
    n_used = nused_ref[0]
    slot = b % 2

    def gather_copy(row, s, r):
        return pltpu.make_async_copy(hp_hbm.at[pl.ds(row, 1)], xbuf.at[s, pl.ds(r, 1)], gsem.at[s])

    def weight_copy(m, e, s):
        hbm, buf = ((wg_hbm, wgbuf), (wu_hbm, wubuf), (wd_hbm, wdbuf))[m]
        return pltpu.make_async_copy(hbm.at[e], buf.at[s], wsem.at[s])

    def weight_copies(e, s):
        return tuple(weight_copy(m, e, s) for m in range(3))

    def issue(blk, s):
        base = blk * EXPERT_BLOCK

        def body(r, carry):
            gather_copy(tok_ref[base + r], s, r).start()
            return carry

        lax.fori_loop(0, EXPERT_BLOCK, body, 0, unroll=8)

    @pl.when(b == 0)
    def _():
        for cp in weight_copies(bexp_ref[0], 0):
            cp.start(priority=WEIGHT_DMA_PRIORITY)
        wslot[0] = 0

        def zero_rows(lo, hi):
            def zero_body(r, carry):
                tok_ref[r] = 0
                return carry

            lax.fori_loop(lo, hi, zero_body, 0)

        def expert_body(e, carry):
            zero_rows(pad_ref[e], pad_ref[N_EXPERTS + e])
            return carry

        lax.fori_loop(0, N_EXPERTS, expert_body, 0)
        tail = n_used * EXPERT_BLOCK
        zero_rows(tail, jnp.minimum(tail + EXPERT_BLOCK, n_blocks * EXPERT_BLOCK))

        def fill_body(t, carry):
            tok_ref[dest_ref[2 * t]] = t
            tok_ref[dest_ref[2 * t + 1]] = t
            return carry

        lax.fori_loop(0, dest_ref.shape[0] // 2, fill_body, 0, unroll=8)
        issue(0, 0)

    @pl.when(b < n_used)
    def _():
        e = bexp_ref[b]
        new_expert = (b == 0) | (e != bexp_ref[jnp.maximum(b - 1, 0)])

        @pl.when(new_expert)
        def _():
            @pl.when(b > 0)
            def _():
                wslot[0] = 1 - wslot[0]

            s = wslot[0]
            for cp in weight_copies(e, s):
                cp.wait()
            wg_bf[...] = wgbuf[s].astype(BF16)
            wu_bf[...] = wubuf[s].astype(BF16)
            wd_bf[...] = wdbuf[s].astype(BF16)

        stage = stage_ref[b]
        nxt = nxt_ref[b]
        other = 1 - wslot[0]
        for m in range(3):
            @pl.when(((stage >> m) & 1) == 1)
            def _(m=m):
                weight_copy(m, nxt, other).start(priority=WEIGHT_DMA_PRIORITY)

    def expert_block(cur):
        pltpu.make_async_copy(hp_hbm.at[pl.ds(0, EXPERT_BLOCK)], xbuf.at[cur], gsem.at[cur]).wait()
        base = jnp.minimum(b + 1, n_blocks - 1) * EXPERT_BLOCK
        for r in range(EXPERT_BLOCK):
            gather_copy(tok_ref[base + r], 1 - cur, r).start()
        h = _unpack_bf16_pairs(xbuf[cur]).astype(BF16)
        g = jnp.dot(h, wg_bf[...], preferred_element_type=F32)
        u = jnp.dot(h, wu_bf[...], preferred_element_type=F32)
        a = (g * jax.nn.sigmoid(g) * u).astype(BF16)
        out_ref[...] = _pack_bf16_pairs(jnp.dot(a, wd_bf[...], preferred_element_type=F32))

    for cur in range(2):
        pl.when((b < n_used) & (slot == cur))(functools.partial(expert_block, cur))

    @pl.when(b == n_used)
    def _():
        pltpu.make_async_copy(hp_hbm.at[pl.ds(0, EXPERT_BLOCK)], xbuf.at[slot], gsem.at[slot]).wait()

    @pl.when((b >= n_used) & (b < n_blocks))
    def _():
        out_ref[...] = jnp.zeros_like(out_ref)


def _moe(dest, block_expert, next_expert, prefetch_stage, pad_bounds, n_used, h_packed, wg, wu, wd):
    n_blocks = block_expert.shape[0]
    n_rows = n_blocks * EXPERT_BLOCK
    half = D_MODEL // 2
    grid_spec = pltpu.PrefetchScalarGridSpec(
        num_scalar_prefetch=6,
        grid=(n_blocks + 1,),
        in_specs=[
            pl.BlockSpec(memory_space=pl.ANY),
            pl.BlockSpec(memory_space=pl.ANY),
            pl.BlockSpec(memory_space=pl.ANY),
            pl.BlockSpec(memory_space=pl.ANY),
        ],
        out_specs=pl.BlockSpec((EXPERT_BLOCK, half), lambda b, *_: (jnp.minimum(b, n_blocks - 1), 0)),
        scratch_shapes=[
            pltpu.VMEM((2, EXPERT_BLOCK, half), jnp.uint32),
            pltpu.SemaphoreType.DMA((2,)),
            pltpu.VMEM((2, D_MODEL, EXPERT_HIDDEN), F32),
            pltpu.VMEM((2, D_MODEL, EXPERT_HIDDEN), F32),
            pltpu.VMEM((2, EXPERT_HIDDEN, D_MODEL), F32),
            pltpu.SemaphoreType.DMA((2,)),
            pltpu.VMEM((D_MODEL, EXPERT_HIDDEN), BF16),
            pltpu.VMEM((D_MODEL, EXPERT_HIDDEN), BF16),
            pltpu.VMEM((EXPERT_HIDDEN, D_MODEL), BF16),
            pltpu.SMEM((1,), jnp.int32),
            pltpu.SMEM((n_rows,), jnp.int32),
        ],
    )
    return pl.pallas_call(
        functools.partial(_moe_kernel, n_blocks=n_blocks),
        out_shape=jax.ShapeDtypeStruct((n_rows, half), jnp.uint32),
        grid_spec=grid_spec,
        compiler_params=_cparams(("arbitrary",)),
        name="moe",
    )(dest, block_expert, next_expert, prefetch_stage, pad_bounds, n_used, h_packed, wg, wu, wd)


FINAL_TM = 256


def _final_kernel(dest_ref, x1_ref, info_ref, gain_ref, rows_hbm, y_ref, rbuf, sem, *, tok_offset):
    i = pl.program_id(0)
    n = pl.num_programs(0)
    tm = FINAL_TM
    slot = i % 2

    def issue(blk, s):
        base = 2 * (tok_offset + blk * tm)
        for r in range(tm):
            for kk in range(2):
                pltpu.make_async_copy(rows_hbm.at[pl.ds(dest_ref[base + 2 * r + kk], 1)],
                                      rbuf.at[s, kk, pl.ds(r, 1)], sem.at[s]).start(priority=kk)

    @pl.when(i == 0)
    def _():
        issue(0, 0)

    @pl.when(i + 1 < n)
    def _():
        issue(i + 1, 1 - slot)

    for kk in range(2):
        pltpu.make_async_copy(rows_hbm.at[pl.ds(0, tm)], rbuf.at[slot, kk], sem.at[slot]).wait()
    info = info_ref[...]
    x = (x1_ref[...] + info[:, 2:3] * _unpack_bf16_pairs(rbuf[slot, 0])
         + info[:, 3:4] * _unpack_bf16_pairs(rbuf[slot, 1]))
    y_ref[...] = x * lax.rsqrt(jnp.mean(x * x, axis=-1, keepdims=True) + NORM_EPS) * gain_ref[...]


def _final(dest, x1_all, info, gain, out_rows, *, tok_offset, n_tok):
    tm = FINAL_TM
    off = tok_offset // tm
    grid_spec = pltpu.PrefetchScalarGridSpec(
        num_scalar_prefetch=1,
        grid=(n_tok // tm,),
        in_specs=[
            pl.BlockSpec((tm, D_MODEL), lambda i, d: (i + off, 0)),
            pl.BlockSpec((tm, LANES), lambda i, d: (i + off, 0)),
            pl.BlockSpec((1, D_MODEL), lambda i, d: (0, 0)),
            pl.BlockSpec(memory_space=pl.ANY),
        ],
        out_specs=pl.BlockSpec((tm, D_MODEL), lambda i, d: (i, 0)),
        scratch_shapes=[pltpu.VMEM((2, 2, tm, D_MODEL // 2), jnp.uint32), pltpu.SemaphoreType.DMA((2,))],
    )
    return pl.pallas_call(
        functools.partial(_final_kernel, tok_offset=tok_offset),
        out_shape=jax.ShapeDtypeStruct((n_tok, D_MODEL), F32),
        grid_spec=grid_spec,
        compiler_params=_cparams(("arbitrary",)),
        name="final",
    )(dest, x1_all, info, gain, out_rows)


def _rot_table(pos, n_rows):
    half = RET_KEY_DIM // 2
    theta = 10000.0 ** (-jnp.linspace(0.0, 1.0, half, dtype=F32))
    ang = pos.astype(F32)[:, None] * theta[None, :]
    cos, sin = jnp.cos(ang), jnp.sin(ang)
    cos128 = jnp.tile(cos, (1, 4))
    sin128 = jnp.tile(jnp.concatenate([-sin, sin], axis=1), (1, 2))
    k_scale = RET_KEY_DIM ** -0.5
    tab = jnp.concatenate([cos128, sin128, cos128 * k_scale, sin128 * k_scale], axis=1)
    return jnp.tile(tab, (n_rows // tab.shape[0], 1))


def kernel(x_prompt, x_sample, cache_diff_k, cache_diff_v, state_retention, norm_mix_gain, w_in, lambda_q1,
           lambda_k1, lambda_q2, lambda_k2, diff_subln_gain, w_ret_out, w_diff_out, w_out, rel_bias_table,
           norm_ffn_gain, w_group_router, b_group_router, w_expert_router, b_expert_router, w_expert_gate,
           w_expert_up, w_expert_down, norm_final_gain):
    B, S, D = x_prompt.shape
    BS, L, _ = x_sample.shape
    past = cache_diff_k.shape[2]
    TP, TS = B * S, BS * L
    T = TP + TS

    w_in_bf = w_in[0].astype(BF16)
    wr_bf = w_ret_out[0].astype(BF16)
    wd_bf = w_diff_out[0].astype(BF16)
    wo_bf = w_out[0].astype(BF16)
    gain_mix = norm_mix_gain[0][None, :]
    gain_ffn = norm_ffn_gain[0][None, :]
    gain_fin = norm_final_gain[None, :]
    gain_sub = diff_subln_gain[0][None, :]
    lam_rows = jnp.zeros((8, LANES), F32).at[0:4, 0:DIFF_HEAD_DIM].set(
        jnp.stack([lambda_q1[0], lambda_k1[0], lambda_q2[0], lambda_k2[0]]))
    rot_p = _rot_table(jnp.arange(S), S)
    rot_s = _rot_table(past + jnp.arange(L), PROJ_TM)

    far = _far_bias(rel_bias_table)[:, None, None]
    qp = ATT_TQ + jnp.arange(ATT_TQ)
    def tile_bias(k_pos):
        bias = (_relative_bias(qp, k_pos, rel_bias_table) - far) * LOG2E
        visible = (k_pos[None, :] // CHUNK) <= (qp[:, None] // CHUNK)
        bias = jnp.swapaxes(jnp.where(visible[None], bias, NEG_INF), 1, 2)
        return jnp.concatenate([bias, bias], axis=2)
    bias_prev = tile_bias(jnp.arange(ATT_TK))
    bias_diag = tile_bias(ATT_TQ + jnp.arange(ATT_TK))
    q_pos_s = past + jnp.arange(L)
    assert SAMPLE_NEAR - L >= MAX_DISTANCE
    b_last = (_relative_bias(q_pos_s, past - SAMPLE_NEAR + jnp.arange(SAMPLE_NEAR), rel_bias_table) - far) * LOG2E
    bias_past = jnp.concatenate([b_last, b_last], axis=1)
    b_new = (_relative_bias(q_pos_s, q_pos_s, rel_bias_table) - far) * LOG2E
    bias_new = jnp.concatenate([b_new, b_new], axis=1)

    xp2 = x_prompt.reshape(TP, D)
    xs2 = x_sample.reshape(TS, D)
    z3p, gp, gatesp, v32p, qt, vt5, kt32 = _proj(xp2, gain_mix, w_in_bf, rot_p, seq=S, transposed_k=True)
    z3s, gs, gatess, v32s, k32s = _proj(xs2, gain_mix, w_in_bf, rot_s, seq=L, transposed_k=False)

    zero_state = jnp.zeros((B, N_HEADS, RET_KEY_DIM, HEAD_V), F32)
    orp, ret_p = _retention(z3p, zero_state, batch=B, seq=S, chunk=256)
    ors, ret_s = _retention(z3s, state_retention[0], batch=BS, seq=L, chunk=L)

    gain_col = jnp.broadcast_to(diff_subln_gain[0][:, None], (HEAD_V, ATT_TQ))
    odp = _attn_prompt(z3p, qt, vt5, bias_prev, bias_diag, lam_rows, gain_col, batch=B, seq=S)
    kc = jnp.transpose(cache_diff_k[0], (0, 2, 3, 4, 1)).reshape(BS, N_HEADS, LANES, past)
    vc = cache_diff_v[0].reshape(BS, past * N_HEADS, LANES)
    ods = _attn_sample(z3s, kc, vc, bias_past, bias_new, lam_rows, gain_sub, batch=BS, n_q=L, past=past)

    mgp = _post_a(orp, gp, odp, gatesp, wr_bf, wd_bf)
    mgs = _post_a(ors, gs, ods, gatess, wr_bf, wd_bf)

    w_rt = jnp.zeros((D, LANES), F32)
    w_rt = w_rt.at[:, 0:N_GROUPS].set(w_group_router[0])
    w_rt = w_rt.at[:, N_GROUPS:N_GROUPS + N_EXPERTS].set(
        jnp.transpose(w_expert_router[0], (1, 0, 2)).reshape(D, N_EXPERTS))
    w_rt_hi = w_rt.astype(BF16)
    w_rt_lo = (w_rt - w_rt_hi.astype(F32)).astype(BF16)
    w_router2 = jnp.concatenate([w_rt_hi, w_rt_lo], axis=1)
    bias_row = jnp.zeros((1, LANES), F32)
    bias_row = bias_row.at[0, 0:N_GROUPS].set(b_group_router[0])
    bias_row = bias_row.at[0, N_GROUPS:N_GROUPS + N_EXPERTS].set(b_expert_router[0].reshape(-1))

    x1_all, lg_all, h_packed = _post_b(xp2, mgp, xs2, mgs, wo_bf, gain_ffn, w_router2)

    info, counts = _route(lg_all, bias_row)
    counts = counts[0, :N_EXPERTS].astype(jnp.int32)
    padded = (counts + EXPERT_BLOCK - 1) // EXPERT_BLOCK * EXPERT_BLOCK
    pad_end = jnp.cumsum(padded)
    offs = pad_end - padded
    e12 = info[:, 0:2].astype(jnp.int32)
    rank12 = info[:, 4:6].astype(jnp.int32)
    expert_ids = jnp.arange(N_EXPERTS, dtype=jnp.int32)
    offs_e = jnp.sum(jnp.where(e12[:, :, None] == expert_ids, offs, 0), axis=-1)
    dest = (offs_e + rank12).reshape(-1)
    n_assign = 2 * T
    n_rows = -(-n_assign // EXPERT_BLOCK) * EXPERT_BLOCK + N_EXPERTS * EXPERT_BLOCK
    n_blocks = n_rows // EXPERT_BLOCK
    block_start = jnp.arange(n_blocks, dtype=jnp.int32) * EXPERT_BLOCK
    block_expert = jnp.minimum(jnp.sum((pad_end[None, :] <= block_start[:, None]).astype(jnp.int32), axis=1),
                               N_EXPERTS - 1)
    n_used = (pad_end[-1:] // EXPERT_BLOCK).astype(jnp.int32)
    later_used = (expert_ids[None, :] > expert_ids[:, None]) & (counts[None, :] > 0)
    next_of_expert = jnp.min(jnp.where(later_used, expert_ids[None, :], N_EXPERTS), axis=1)
    next_of_expert = jnp.where(next_of_expert == N_EXPERTS, -1, next_of_expert)
    next_expert = jnp.sum(jnp.where(block_expert[:, None] == expert_ids[None, :], next_of_expert[None, :], 0),
                          axis=1).astype(jnp.int32)

    pad_bounds = jnp.concatenate([offs + counts, pad_end]).astype(jnp.int32)
    onehot_be = block_expert[:, None] == expert_ids[None, :]
    first_block = jnp.sum(jnp.where(onehot_be, (offs // EXPERT_BLOCK)[None, :], 0), axis=1)
    blocks_of = jnp.sum(jnp.where(onehot_be, (padded // EXPERT_BLOCK)[None, :], 0), axis=1)
    k_in_expert = jnp.arange(n_blocks, dtype=jnp.int32) - first_block
    prefetch_stage = sum(
        jnp.where((jnp.minimum(m, blocks_of - 1) == k_in_expert) & (next_expert >= 0), 1 << m, 0)
        for m in range(3)).astype(jnp.int32)
    out_rows = _moe(dest, block_expert, next_expert, prefetch_stage, pad_bounds, n_used, h_packed,
                    w_expert_gate[0], w_expert_up[0], w_expert_down[0])

    y_p = _final(dest, x1_all, info, gain_fin, out_rows, tok_offset=0, n_tok=TP)
    y_s = _final(dest, x1_all, info, gain_fin, out_rows, tok_offset=TP, n_tok=TS)

    new_k_p = jnp.transpose(kt32, (0, 1, 5, 2, 3, 4))
    new_v_p = v32p.reshape(1, B, S, N_HEADS, HEAD_V)
    new_k_s = k32s.reshape(1, BS, L, N_HEADS, 2, DIFF_HEAD_DIM)
    new_v_s = v32s.reshape(1, BS, L, N_HEADS, HEAD_V)
    return (y_p.reshape(B, S, D), y_s.reshape(BS, L, D), new_k_p, new_v_p, ret_p[None],
            new_k_s, new_v_s, ret_s[None])
```

```python
import functools
import math

import jax
import jax.numpy as jnp
from jax import lax
from jax.experimental import pallas as pl
from jax.experimental.pallas import tpu as pltpu

F32 = jnp.float32
BF16 = jnp.bfloat16

D_MODEL = 2048
CHUNK = 64
N_HEADS = 8
RET_KEY_DIM = 64
HEAD_V = 128
DIFF_HEAD_DIM = 64
IN_WIDTH = 10240
NUM_BUCKETS = 32
MAX_DISTANCE = 128
N_GROUPS = 8
EXPERTS_PER_GROUP = 8
N_EXPERTS = 64
EXPERT_HIDDEN = 512
EXPERT_BLOCK = 128
NORM_EPS = 1e-6
NEG_INF = -1e30
LANES = 128
LOG2E = 1.4426950408889634
VT_ROWS = HEAD_V + 16

VMEM_LIMIT = 56 * 1024 * 1024


def _cparams(sem):
    return pltpu.CompilerParams(dimension_semantics=sem, vmem_limit_bytes=VMEM_LIMIT)


PROJ_TM = 512
PROJ_TN = 1024
PROJ_MC = 128
ATT_TK = 256


def _rotate_pairs(acc, cos, sin):
    outs = []
    lane = lax.broadcasted_iota(jnp.int32, (acc.shape[0], LANES), 1)
    first_half = (lane % 64) < 32
    for c in range(acc.shape[1] // LANES):
        xs = acc[:, c * LANES:(c + 1) * LANES]
        swapped = jnp.where(first_half, pltpu.roll(xs, 96, axis=1), pltpu.roll(xs, 32, axis=1))
        outs.append(xs * cos + swapped * sin)
    return outs


def _proj_kernel(x_ref, gain_ref, w_ref, rot_ref, *out_refs, transposed_k):
    if transposed_k:
        z_ref, g_ref, gates_ref, v32_ref, qt_ref, vt_ref, kt32_ref, h_scr = out_refs
    else:
        z_ref, g_ref, gates_ref, v32_ref, k32_ref, h_scr = out_refs
    j = pl.program_id(1)

    @pl.when(j == 0)
    def _():
        x = x_ref[...]
        ms = jnp.mean(x * x, axis=-1, keepdims=True)
        h_scr[...] = (x * lax.rsqrt(ms + NORM_EPS) * gain_ref[...]).astype(BF16)

    n_slab = PROJ_TN // LANES

    def for_row_chunks(epilogue):
        for c in range(PROJ_TM // PROJ_MC):
            rows = slice(c * PROJ_MC, (c + 1) * PROJ_MC)
            epilogue(c, rows, jnp.dot(h_scr[rows, :], w_ref[...], preferred_element_type=F32))

    def store_slabs(rows, vals):
        for s in range(n_slab):
            z_ref[s, rows, :] = vals[s].astype(BF16)

    def split(a):
        return [a[:, s * LANES:(s + 1) * LANES] for s in range(n_slab)]

    @pl.when(j == 0)
    def _():
        def epilogue(c, rows, acc):
            half = PROJ_TN // 2
            q = _rotate_pairs(acc[:, :half], rot_ref[rows, 0:128], rot_ref[rows, 128:256])
            k = _rotate_pairs(acc[:, half:], rot_ref[rows, 256:384], rot_ref[rows, 384:512])
            store_slabs(rows, q + k)
        for_row_chunks(epilogue)

    @pl.when(j == 1)
    def _():
        for_row_chunks(lambda c, rows, acc: store_slabs(rows, split(acc)))

    @pl.when(j == 2)
    def _():
        def epilogue(c, rows, acc):
            g_ref[rows, :] = (acc * jax.nn.sigmoid(acc)).astype(BF16)
        for_row_chunks(epilogue)

    @pl.when(j == 3)
    def _():
        def epilogue(c, rows, acc):
            qs = acc * (DIFF_HEAD_DIM ** -0.5 * LOG2E)
            store_slabs(rows, split(qs))
            if transposed_k:
                qs_t = qs.T.astype(BF16)
                for hh in range(n_slab):
                    qt_ref[0, hh, :, rows] = qs_t[hh * LANES:(hh + 1) * LANES, :]
        for_row_chunks(epilogue)

    @pl.when(j == 4)
    def _():
        def epilogue(c, rows, acc):
            store_slabs(rows, split(acc))
            if transposed_k:
                kt32_ref[0, 0, :, :, :, rows] = acc.T.reshape(n_slab, 2, DIFF_HEAD_DIM, PROJ_MC)
            else:
                k32_ref[rows, :] = acc
        for_row_chunks(epilogue)

    @pl.when(j == 5)
    def _():
        def epilogue(c, rows, acc):
            store_slabs(rows, split(acc))
            v32_ref[rows, :] = acc
            if transposed_k:
                acc_tb = acc.T.astype(BF16)
                t, cols = (c * PROJ_MC) // ATT_TK, slice((c * PROJ_MC) % ATT_TK, (c * PROJ_MC) % ATT_TK + PROJ_MC)
                ones = jnp.ones((VT_ROWS - HEAD_V, PROJ_MC), BF16)
                for hh in range(n_slab):
                    vt_ref[0, hh, t, 0:HEAD_V, cols] = acc_tb[hh * LANES:(hh + 1) * LANES, :]
                    vt_ref[0, hh, t, HEAD_V:VT_ROWS, cols] = ones
        for_row_chunks(epilogue)

    @pl.when(j >= 6)
    def _():
        def epilogue(c, rows, acc):
            gates_ref[rows, :] = jax.nn.sigmoid(acc).astype(BF16)
        for_row_chunks(epilogue)


def _proj(x2d, gain, w_bf, rot, *, seq, transposed_k):
    T = x2d.shape[0]
    tm, tn = PROJ_TM, PROJ_TN
    ni, nj = T // tm, IN_WIDTH // tn
    rot_blocks = rot.shape[0] // tm

    def zmap(i, j):
        zj = jnp.where(j < 2, j, jnp.where(j < 3, 1, jnp.where(j < 6, j - 1, 4)))
        return (zj, i, 0)

    out_shape = [
        jax.ShapeDtypeStruct((40, T, LANES), BF16),
        jax.ShapeDtypeStruct((T, 1024), BF16),
        jax.ShapeDtypeStruct((T, 4096), BF16),
        jax.ShapeDtypeStruct((T, 1024), F32),
    ]
    out_specs = [
        pl.BlockSpec((8, tm, LANES), zmap),
        pl.BlockSpec((tm, tn), lambda i, j: (i, 0)),
        pl.BlockSpec((tm, tn), lambda i, j: (i, jnp.clip(j - 6, 0, 3))),
        pl.BlockSpec((tm, tn), lambda i, j: (i, 0)),
    ]
    if transposed_k:
        B = T // seq
        spb = seq // tm
        out_shape += [
            jax.ShapeDtypeStruct((B, N_HEADS, LANES, seq), BF16),
            jax.ShapeDtypeStruct((B, N_HEADS, seq // ATT_TK, VT_ROWS, ATT_TK), BF16),
            jax.ShapeDtypeStruct((1, B, N_HEADS, 2, DIFF_HEAD_DIM, seq), F32),
        ]
        out_specs += [
            pl.BlockSpec((1, N_HEADS, LANES, tm), lambda i, j: (i // spb, 0, 0, i % spb)),
            pl.BlockSpec((1, N_HEADS, tm // ATT_TK, VT_ROWS, ATT_TK), lambda i, j: (i // spb, 0, i % spb, 0, 0)),
            pl.BlockSpec((1, 1, N_HEADS, 2, DIFF_HEAD_DIM, tm), lambda i, j: (0, i // spb, 0, 0, 0, i % spb)),
        ]
    else:
        out_shape += [jax.ShapeDtypeStruct((T, 1024), F32)]
        out_specs += [pl.BlockSpec((tm, tn), lambda i, j: (i, 0))]

    return pl.pallas_call(
        functools.partial(_proj_kernel, transposed_k=transposed_k),
        out_shape=out_shape,
        grid=(ni, nj),
        in_specs=[
            pl.BlockSpec((tm, D_MODEL), lambda i, j: (i, 0)),
            pl.BlockSpec((1, D_MODEL), lambda i, j: (0, 0)),
            pl.BlockSpec((D_MODEL, tn), lambda i, j: (0, j)),
            pl.BlockSpec((tm, 512), lambda i, j: (i % rot_blocks, 0)),
        ],
        out_specs=out_specs,
        scratch_shapes=[pltpu.VMEM((tm, D_MODEL), BF16)],
        compiler_params=_cparams(("arbitrary", "arbitrary")),
        name="proj_t" if transposed_k else "proj",
    )(x2d, gain, w_bf, rot)


def _retention_kernel(q_ref, k_ref, v_ref, decay_ref, inner_ref, outer_ref, st0_ref,
                      o_ref, st_out_ref, st_scr, *, chunk):
    c = pl.program_id(1)
    nc = pl.num_programs(1)

    @pl.when(c == 0)
    def _():
        st_scr[...] = jnp.zeros_like(st_scr)
        for h in range(N_HEADS):
            a = h % 2
            st_scr[h, a * RET_KEY_DIM:(a + 1) * RET_KEY_DIM, :] = st0_ref[0, h]

    lane_lo = lax.broadcasted_iota(jnp.int32, (chunk, LANES), 1) < RET_KEY_DIM
    row_lo = lax.broadcasted_iota(jnp.int32, (LANES, LANES), 0) < RET_KEY_DIM
    for h in range(N_HEADS):
        p, a = h // 2, h % 2
        q = q_ref[p]
        k = k_ref[p]
        v = v_ref[h]
        qa = jnp.where(lane_lo == (a == 0), q, jnp.zeros_like(q))
        s = lax.dot_general(qa, k, (((1,), (1,)), ((), ())), preferred_element_type=F32)
        s = s * decay_ref[h]
        st = st_scr[h]
        inner = inner_ref[h]
        o = (jnp.dot(s.astype(BF16), v, preferred_element_type=F32)
             + jnp.dot(qa, st.astype(BF16), preferred_element_type=F32) * inner)
        ko = (k.astype(F32) * outer_ref[p]).astype(BF16)
        upd = lax.dot_general(ko, v, (((0,), (0,)), ((), ())), preferred_element_type=F32)
        upd = jnp.where(row_lo == (a == 0), upd, 0.0)
        chunk_decay = inner[chunk - 1:chunk, :]
        st_scr[h] = chunk_decay * st + upd
        o_n = o * lax.rsqrt(jnp.mean(o * o, axis=-1, keepdims=True) + NORM_EPS)
        o_ref[:, h * HEAD_V:(h + 1) * HEAD_V] = o_n.astype(BF16)

    @pl.when(c == nc - 1)
    def _():
        for h in range(N_HEADS):
            a = h % 2
            st_out_ref[0, h] = st_scr[h, a * RET_KEY_DIM:(a + 1) * RET_KEY_DIM, :]


def _retention_consts(chunk):
    log_g = jnp.log(1.0 - 2.0 ** (-5.0 - jnp.arange(N_HEADS, dtype=F32)))
    j = jnp.arange(chunk, dtype=F32)
    diff = j[:, None] - j[None, :]
    decay = jnp.where(diff >= 0, jnp.exp(log_g[:, None, None] * jnp.maximum(diff, 0.0)), 0.0)
    inner = jnp.exp(log_g[:, None] * (j + 1.0))
    outer = jnp.exp(log_g[:, None] * (chunk - 1.0 - j))
    inner_b = jnp.broadcast_to(inner[:, :, None], (N_HEADS, chunk, LANES))
    outer_pair = jnp.repeat(outer.reshape(N_HEADS // 2, 2, chunk).transpose(0, 2, 1), RET_KEY_DIM, axis=2)
    return decay, inner_b, outer_pair


def _retention(z3, state0, *, batch, seq, chunk):
    T = batch * seq
    nc = seq // chunk
    decay, inner_b, outer_pair = _retention_consts(chunk)
    return pl.pallas_call(
        functools.partial(_retention_kernel, chunk=chunk),
        out_shape=[jax.ShapeDtypeStruct((T, 1024), BF16),
                   jax.ShapeDtypeStruct((batch, N_HEADS, RET_KEY_DIM, HEAD_V), F32)],
        grid=(batch, nc),
        in_specs=[
            pl.BlockSpec((4, chunk, LANES), lambda b, c: (0, b * nc + c, 0)),
            pl.BlockSpec((4, chunk, LANES), lambda b, c: (1, b * nc + c, 0)),
            pl.BlockSpec((8, chunk, LANES), lambda b, c: (1, b * nc + c, 0)),
            pl.BlockSpec((N_HEADS, chunk, chunk), lambda b, c: (0, 0, 0)),
            pl.BlockSpec((N_HEADS, chunk, LANES), lambda b, c: (0, 0, 0)),
            pl.BlockSpec((N_HEADS // 2, chunk, LANES), lambda b, c: (0, 0, 0)),
            pl.BlockSpec((1, N_HEADS, RET_KEY_DIM, HEAD_V), lambda b, c: (b, 0, 0, 0)),
        ],
        out_specs=[
            pl.BlockSpec((chunk, 1024), lambda b, c: (b * nc + c, 0)),
            pl.BlockSpec((1, N_HEADS, RET_KEY_DIM, HEAD_V), lambda b, c: (b, 0, 0, 0)),
        ],
        scratch_shapes=[pltpu.VMEM((N_HEADS, LANES, LANES), F32)],
        compiler_params=_cparams(("arbitrary", "arbitrary")),
        name=f"retention_c{chunk}",
    )(z3, z3, z3, decay, inner_b, outer_pair, state0)


def _lam_init(layer=0):
    return 0.8 - 0.6 * math.exp(-0.3 * layer)


def _lam_from_ref(lam_ref):
    lp = lam_ref[...]
    s1 = jnp.sum(lp[0:1] * lp[1:2], axis=-1, keepdims=True)
    s2 = jnp.sum(lp[2:3] * lp[3:4], axis=-1, keepdims=True)
    return jnp.exp(s1) - jnp.exp(s2) + _lam_init()


def _stack_q(q):
    lane_lo = lax.broadcasted_iota(jnp.int32, q.shape, 1) < DIFF_HEAD_DIM
    zero = jnp.zeros_like(q)
    return jnp.concatenate([jnp.where(lane_lo, q, zero), jnp.where(lane_lo, zero, q)], axis=0)


def _with_ones(v):
    return jnp.concatenate([v, jnp.ones_like(v)], axis=1)


def _diff_finish(acc_ref, lam, gain, n):
    o = acc_ref[:, 0:HEAD_V] / acc_ref[:, HEAD_V:2 * HEAD_V]
    o = o[:n] - lam * o[n:]
    o = o * lax.rsqrt(jnp.mean(o * o, axis=-1, keepdims=True) + NORM_EPS) * gain
    return o * (1.0 - _lam_init())


def _relative_bias(q_pos, k_pos, table):
    rel = k_pos[None, :] - q_pos[:, None]
    half = NUM_BUCKETS // 2
    max_exact = half // 2
    n = jnp.abs(rel)
    log_ratio = jnp.log(jnp.maximum(n, 1).astype(F32) / max_exact) / math.log(MAX_DISTANCE / max_exact)
    large = jnp.minimum(max_exact + (log_ratio * (half - max_exact)).astype(jnp.int32), half - 1)
    bucket = (rel > 0).astype(jnp.int32) * half + jnp.where(n < max_exact, n, large)
    out = jnp.zeros((table.shape[1],) + bucket.shape, F32)
    for bkt in range(NUM_BUCKETS):
        out = jnp.where(bucket[None] == bkt, table[bkt].astype(F32)[:, None, None], out)
    return out


def _far_bias(table):
    return table[NUM_BUCKETS // 2 - 1].astype(F32)


ATT_TQ = 256
ATT_HP = 8


def _attn_prompt_kernel(qt_ref, k_ref, vt_ref, bprev_ref, bdiag_ref, lam_ref, gain_ref, o_ref, acc_scr):
    i = pl.program_id(2)
    row_lo = lax.broadcasted_iota(jnp.int32, (LANES, ATT_TQ), 0) < DIFF_HEAD_DIM
    qs = []
    for hh in range(ATT_HP):
        qt = qt_ref[0, hh]
        zero = jnp.zeros_like(qt)
        qs.append(jnp.concatenate([jnp.where(row_lo, qt, zero), jnp.where(row_lo, zero, qt)], axis=1))
    acc_scr[...] = jnp.zeros_like(acc_scr)

    def tile(j, ms, biases):
        ss = []
        for hh in range(ATT_HP):
            kt = k_ref[hh, pl.ds(pl.multiple_of(j * ATT_TK, ATT_TK), ATT_TK), :]
            s = jnp.dot(kt, qs[hh], preferred_element_type=F32)
            ss.append(s if biases is None else s + biases[hh])
        ps, alphas, m_out = [], [], []
        for hh in range(ATT_HP):
            m_new = jnp.maximum(ms[hh], jnp.max(ss[hh], axis=0, keepdims=True))
            alphas.append(jnp.exp2(ms[hh] - m_new))
            ps.append(jnp.exp2(ss[hh] - m_new).astype(BF16))
            m_out.append(m_new)
        for hh in range(ATT_HP):
            acc_scr[hh] = alphas[hh] * acc_scr[hh] + jnp.dot(vt_ref[0, hh, j], ps[hh],
                                                             preferred_element_type=F32)
        return tuple(m_out)

    ms = tuple(jnp.full((1, 2 * ATT_TQ), NEG_INF, F32) for _ in range(ATT_HP))
    ms = lax.fori_loop(0, i - 1, lambda j, c: tile(j, c, None), ms)
    jp = jnp.maximum(i - 1, 0)
    ms = lax.cond(i >= 1,
                  lambda c: tile(jp, c, [bprev_ref[hh] for hh in range(ATT_HP)]),
                  lambda c: c, ms)
    tile(i, ms, [bdiag_ref[hh] for hh in range(ATT_HP)])

    lam = _lam_from_ref(lam_ref)
    for hh in range(ATT_HP):
        o = acc_scr[hh, 0:HEAD_V, :] / acc_scr[hh, HEAD_V:HEAD_V + 1, :]
        o = o[:, :ATT_TQ] - lam * o[:, ATT_TQ:]
        o = o * lax.rsqrt(jnp.mean(o * o, axis=0, keepdims=True) + NORM_EPS) * gain_ref[...]
        o_ref[:, hh * HEAD_V:(hh + 1) * HEAD_V] = (o * (1.0 - _lam_init())).T.astype(BF16)


def _attn_prompt(z3, qt, vt5, bias_prev, bias_diag, lam_rows, gain_col, *, batch, seq):
    T = batch * seq
    nq = seq // ATT_TQ
    hp = ATT_HP
    return pl.pallas_call(
        _attn_prompt_kernel,
        out_shape=jax.ShapeDtypeStruct((T, 1024), BF16),
        grid=(batch, N_HEADS // hp, nq),
        in_specs=[
            pl.BlockSpec((1, hp, LANES, ATT_TQ), lambda b, g, i: (b, g, 0, i)),
            pl.BlockSpec((hp, seq, LANES), lambda b, g, i: (24 // hp + g, b, 0), pipeline_mode=pl.Buffered(1)),
            pl.BlockSpec((1, hp, seq // ATT_TK, VT_ROWS, ATT_TK), lambda b, g, i: (b, g, 0, 0, 0),
                         pipeline_mode=pl.Buffered(1)),
            pl.BlockSpec((hp, ATT_TK, 2 * ATT_TQ), lambda b, g, i: (g, 0, 0), pipeline_mode=pl.Buffered(1)),
            pl.BlockSpec((hp, ATT_TK, 2 * ATT_TQ), lambda b, g, i: (g, 0, 0), pipeline_mode=pl.Buffered(1)),
            pl.BlockSpec((8, LANES), lambda b, g, i: (0, 0)),
            pl.BlockSpec((LANES, ATT_TQ), lambda b, g, i: (0, 0)),
        ],
        out_specs=pl.BlockSpec((ATT_TQ, hp * HEAD_V), lambda b, g, i: (b * nq + i, g)),
        scratch_shapes=[pltpu.VMEM((hp, VT_ROWS, 2 * ATT_TQ), F32)],
        compiler_params=_cparams(("arbitrary", "arbitrary", "arbitrary")),
        name="attn_prompt",
    )(qt, z3, vt5, bias_prev, bias_diag, lam_rows, gain_col)


SAMPLE_TK = 2048
SAMPLE_NEAR = 256


def _attn_sample_kernel(q_ref, kc_ref, vc_ref, kn_ref, vn_ref, bpast_ref, bnew_ref, lam_ref, gain_ref,
                        o_ref, m_scr, acc_scr, *, n_past_tiles, n_q):
    t = pl.program_id(1)

    @pl.when(t == 0)
    def _():
        m_scr[...] = jnp.full_like(m_scr, NEG_INF)
        acc_scr[...] = jnp.zeros_like(acc_scr)

    def update(logits, values):
        ps, alphas = [], []
        for h in range(N_HEADS):
            m_prev = m_scr[h]
            m_new = jnp.maximum(m_prev, jnp.max(logits[h], axis=1, keepdims=True))
            alpha = jnp.exp2(m_prev - m_new)
            ps.append(jnp.exp2(logits[h] - m_new[:, 0:1]).astype(BF16))
            alphas.append(jnp.concatenate([alpha, alpha], axis=1))
            m_scr[h] = m_new
        for h in range(N_HEADS):
            acc_scr[h] = alphas[h] * acc_scr[h] + jnp.dot(ps[h], _with_ones(values[h]()),
                                                          preferred_element_type=F32)

    @pl.when(t < n_past_tiles)
    def _():
        is_last = jnp.where(t == n_past_tiles - 1, 1.0, 0.0)
        far_w = SAMPLE_TK - SAMPLE_NEAR
        logits = []
        for h in range(N_HEADS):
            kt = kc_ref[0, h].astype(BF16)
            s = jnp.dot(_stack_q(q_ref[h]), kt, preferred_element_type=F32)
            logits.append(jnp.concatenate([s[:, :far_w], s[:, far_w:] + is_last * bpast_ref[h]], axis=1))
        update(logits, [lambda h=h: vc_ref[0, pl.ds(h, SAMPLE_TK, stride=N_HEADS), :].astype(BF16)
                        for h in range(N_HEADS)])

    @pl.when(t == n_past_tiles)
    def _():
        logits = []
        for h in range(N_HEADS):
            s = lax.dot_general(_stack_q(q_ref[h]), kn_ref[h], (((1,), (1,)), ((), ())),
                                preferred_element_type=F32)
            logits.append(s + bnew_ref[h])
        update(logits, [lambda h=h: vn_ref[h] for h in range(N_HEADS)])
        lam = _lam_from_ref(lam_ref)
        for h in range(N_HEADS):
            o = _diff_finish(acc_scr.at[h], lam, gain_ref[...], n_q)
            o_ref[:, h * HEAD_V:(h + 1) * HEAD_V] = o.astype(BF16)


def _attn_sample(z3, kc, vc, bias_past, bias_new, lam_rows, gain, *, batch, n_q, past):
    npt = past // SAMPLE_TK
    return pl.pallas_call(
        functools.partial(_attn_sample_kernel, n_past_tiles=npt, n_q=n_q),
        out_shape=jax.ShapeDtypeStruct((batch * n_q, 1024), BF16),
        grid=(batch, npt + 1),
        in_specs=[
            pl.BlockSpec((8, n_q, LANES), lambda b, t: (2, b, 0)),
            pl.BlockSpec((1, N_HEADS, LANES, SAMPLE_TK),
                         lambda b, t: (jnp.minimum(b + t // npt, batch - 1), 0, 0, t % npt)),
            pl.BlockSpec((1, SAMPLE_TK * N_HEADS, LANES),
                         lambda b, t: (jnp.minimum(b + t // npt, batch - 1), t % npt, 0)),
            pl.BlockSpec((8, n_q, LANES), lambda b, t: (3, b, 0)),
            pl.BlockSpec((8, n_q, LANES), lambda b, t: (4, b, 0)),
            pl.BlockSpec((N_HEADS, 2 * n_q, SAMPLE_NEAR), lambda b, t: (0, 0, 0)),
            pl.BlockSpec((N_HEADS, 2 * n_q, n_q), lambda b, t: (0, 0, 0)),
            pl.BlockSpec((8, LANES), lambda b, t: (0, 0)),
            pl.BlockSpec((1, LANES), lambda b, t: (0, 0)),
        ],
        out_specs=pl.BlockSpec((n_q, 1024), lambda b, t: (b, 0)),
        scratch_shapes=[pltpu.VMEM((N_HEADS, 2 * n_q, LANES), F32),
                        pltpu.VMEM((N_HEADS, 2 * n_q, 2 * LANES), F32)],
        compiler_params=_cparams(("arbitrary", "arbitrary")),
        name="attn_sample",
    )(z3, kc, vc, z3, z3, bias_past, bias_new, lam_rows, gain)


POST_TM = 256


def _post_a_kernel(or_ref, g_ref, od_ref, gr_ref, gd_ref, wr_ref, wd_ref, out_ref):
    a = jnp.dot(or_ref[...] * g_ref[...], wr_ref[...], preferred_element_type=F32)
    b = jnp.dot(od_ref[...], wd_ref[...], preferred_element_type=F32)
    out_ref[...] = (gr_ref[...].astype(F32) * a + gd_ref[...].astype(F32) * b).astype(BF16)


def _post_a(o_r, g, o_d, gates, wr_bf, wd_bf):
    T = o_r.shape[0]
    tm = POST_TM
    return pl.pallas_call(
        _post_a_kernel,
        out_shape=jax.ShapeDtypeStruct((T, D_MODEL), BF16),
        grid=(T // tm,),
        in_specs=[
            pl.BlockSpec((tm, 1024), lambda i: (i, 0)),
            pl.BlockSpec((tm, 1024), lambda i: (i, 0)),
            pl.BlockSpec((tm, 1024), lambda i: (i, 0)),
            pl.BlockSpec((tm, D_MODEL), lambda i: (i, 0)),
            pl.BlockSpec((tm, D_MODEL), lambda i: (i, 1)),
            pl.BlockSpec((1024, D_MODEL), lambda i: (0, 0)),
            pl.BlockSpec((1024, D_MODEL), lambda i: (0, 0)),
        ],
        out_specs=pl.BlockSpec((tm, D_MODEL), lambda i: (i, 0)),
        compiler_params=_cparams(("arbitrary",)),
        name="post_a",
    )(o_r, g, o_d, gates, gates, wr_bf, wd_bf)


def _pack_bf16_pairs(x):
    w = x.shape[1] // 2
    xb = x.astype(BF16).astype(F32)
    lo = pltpu.bitcast(xb[:, :w], jnp.uint32) >> 16
    hi = pltpu.bitcast(xb[:, w:], jnp.uint32) & jnp.uint32(0xFFFF0000)
    return lo | hi


def _unpack_bf16_pairs(words):
    lo = pltpu.bitcast(words << 16, F32)
    hi = pltpu.bitcast(words & jnp.uint32(0xFFFF0000), F32)
    return jnp.concatenate([lo, hi], axis=1)


def _post_b_kernel(xp_ref, mp_ref, xs_ref, ms_ref, wo_ref, gain_ref, wrt_ref, x1_ref, lg_ref, hp_ref, *,
                   n_prompt_blocks):
    i = pl.program_id(0)
    tm = POST_TM

    def body(x_ref, mg_ref):
        x1 = x_ref[...] + jnp.dot(mg_ref[...], wo_ref[...], preferred_element_type=F32)
        x1_ref[...] = x1
        h2 = x1 * lax.rsqrt(jnp.mean(x1 * x1, axis=-1, keepdims=True) + NORM_EPS) * gain_ref[...]
        hp_ref[...] = _pack_bf16_pairs(h2)
        h_hi = h2.astype(BF16)
        h_lo = (h2 - h_hi.astype(F32)).astype(BF16)
        r = jnp.dot(jnp.concatenate([h_hi, h_lo], axis=0), wrt_ref[...], preferred_element_type=F32)
        lg_ref[...] = (r[:tm, :LANES] + r[:tm, LANES:]) + (r[tm:, :LANES] + r[tm:, LANES:])

    @pl.when(i < n_prompt_blocks)
    def _():
        body(xp_ref, mp_ref)

    @pl.when(i >= n_prompt_blocks)
    def _():
        body(xs_ref, ms_ref)


def _post_b(xp2, mgp, xs2, mgs, wo_bf, gain, w_router2):
    tm = POST_TM
    npb, nsb = xp2.shape[0] // tm, xs2.shape[0] // tm
    T = xp2.shape[0] + xs2.shape[0]
    pmap = lambda i: (jnp.minimum(i, npb - 1), 0)
    smap = lambda i: (jnp.maximum(i - npb, 0), 0)
    return pl.pallas_call(
        functools.partial(_post_b_kernel, n_prompt_blocks=npb),
        out_shape=[jax.ShapeDtypeStruct((T, D_MODEL), F32),
                   jax.ShapeDtypeStruct((T, LANES), F32),
                   jax.ShapeDtypeStruct((T, D_MODEL // 2), jnp.uint32)],
        grid=(npb + nsb,),
        in_specs=[
            pl.BlockSpec((tm, D_MODEL), pmap),
            pl.BlockSpec((tm, D_MODEL), pmap),
            pl.BlockSpec((tm, D_MODEL), smap),
            pl.BlockSpec((tm, D_MODEL), smap),
            pl.BlockSpec((D_MODEL, D_MODEL), lambda i: (0, 0)),
            pl.BlockSpec((1, D_MODEL), lambda i: (0, 0)),
            pl.BlockSpec((D_MODEL, 2 * LANES), lambda i: (0, 0)),
        ],
        out_specs=[pl.BlockSpec((tm, D_MODEL), lambda i: (i, 0)),
                   pl.BlockSpec((tm, LANES), lambda i: (i, 0)),
                   pl.BlockSpec((tm, D_MODEL // 2), lambda i: (i, 0))],
        compiler_params=_cparams(("arbitrary",)),
        name="post_b",
    )(xp2, mgp, xs2, mgs, wo_bf, gain, w_router2)


ROUTE_TM = 512


def _route_kernel(lg_ref, bias_ref, info_ref, cnt_ref, tri_scr, carry_scr):
    i = pl.program_id(0)
    tm = ROUTE_TM

    @pl.when(i == 0)
    def _():
        r = lax.broadcasted_iota(jnp.int32, (tm, tm), 0)
        c = lax.broadcasted_iota(jnp.int32, (tm, tm), 1)
        tri_scr[...] = jnp.where(c < r, 1.0, 0.0).astype(BF16)
        carry_scr[...] = jnp.zeros_like(carry_scr)

    lg = lg_ref[...] + bias_ref[...]
    lane = lax.broadcasted_iota(jnp.int32, (tm, LANES), 1)
    lane_f = lane.astype(F32)
    neg = jnp.float32(-jnp.inf)

    def first_argmax(vals):
        top = jnp.max(vals, axis=1, keepdims=True)
        idx = jnp.min(jnp.where(vals == top, lane_f, float(LANES)), axis=1, keepdims=True)
        return top, idx

    is_group = lane < N_GROUPS
    gl = jnp.where(is_group, lg, neg)
    g_top, g_idx = first_argmax(gl)
    g_weight = 1.0 / jnp.sum(jnp.exp(gl - g_top), axis=1, keepdims=True)
    lane_group = ((lane - N_GROUPS) >> 3).astype(F32)
    in_group = (lane >= N_GROUPS) & (lane < N_GROUPS + N_EXPERTS) & (lane_group == g_idx)
    el = jnp.where(in_group, lg, neg)
    t1, i1 = first_argmax(el)
    el2 = jnp.where(lane_f == i1, neg, el)
    t2, i2 = first_argmax(el2)
    e2w = jnp.exp(t2 - t1)
    p1 = 1.0 / (1.0 + e2w)
    gate1 = g_weight * p1
    gate2 = g_weight * (e2w * p1)
    e1 = i1 - float(N_GROUPS)
    e2 = i2 - float(N_GROUPS)

    hot1 = lane_f == e1
    hot2 = lane_f == e2
    both = jnp.where(hot1 | hot2, 1.0, 0.0)
    prefix = jnp.dot(tri_scr[...], both.astype(BF16), preferred_element_type=F32) + carry_scr[...]
    rank1 = jnp.sum(jnp.where(hot1, prefix, 0.0), axis=1, keepdims=True)
    rank2 = jnp.sum(jnp.where(hot2, prefix, 0.0), axis=1, keepdims=True)
    carry_scr[...] = carry_scr[...] + jnp.sum(both, axis=0, keepdims=True)

    info = jnp.where(lane == 0, e1, jnp.where(lane == 1, e2, jnp.where(lane == 2, gate1, jnp.where(
        lane == 3, gate2, jnp.where(lane == 4, rank1, jnp.where(lane == 5, rank2, 0.0))))))
    info_ref[...] = info
    cnt_ref[...] = carry_scr[...]


def _route(lg_all, bias_row):
    T = lg_all.shape[0]
    tm = ROUTE_TM
    return pl.pallas_call(
        _route_kernel,
        out_shape=[jax.ShapeDtypeStruct((T, LANES), F32), jax.ShapeDtypeStruct((1, LANES), F32)],
        grid=(T // tm,),
        in_specs=[pl.BlockSpec((tm, LANES), lambda i: (i, 0)), pl.BlockSpec((1, LANES), lambda i: (0, 0))],
        out_specs=[pl.BlockSpec((tm, LANES), lambda i: (i, 0)), pl.BlockSpec((1, LANES), lambda i: (0, 0))],
        scratch_shapes=[pltpu.VMEM((tm, tm), BF16), pltpu.VMEM((1, LANES), F32)],
        compiler_params=_cparams(("arbitrary",)),
        name="route",
    )(lg_all, bias_row)


WEIGHT_DMA_PRIORITY = 1


def _moe_kernel(dest_ref, bexp_ref, nxt_ref, pad_ref, nused_ref, hp_hbm, wg_hbm, wu_hbm, wd_hbm,
                out_ref, xbuf, gsem, wgbuf, wubuf, wdbuf, wsem, wg_bf, wu_bf, wd_bf, wslot, tok_ref, *,
                n_blocks):
    b = pl.program_id(0)
    n_used = nused_ref[0]
    slot = b % 2

    def gather_copy(row, s, r):
        return pltpu.make_async_copy(hp_hbm.at[pl.ds(row, 1)], xbuf.at[s, pl.ds(r, 1)], gsem.at[s])

    def weight_copies(e, s):
        return (pltpu.make_async_copy(wg_hbm.at[e], wgbuf.at[s], wsem.at[s]),
                pltpu.make_async_copy(wu_hbm.at[e], wubuf.at[s], wsem.at[s]),
                pltpu.make_async_copy(wd_hbm.at[e], wdbuf.at[s], wsem.at[s]))

    def issue(blk, s):
        base = blk * EXPERT_BLOCK

        def body(r, carry):
            gather_copy(tok_ref[base + r], s, r).start()
            return carry

        lax.fori_loop(0, EXPERT_BLOCK, body, 0, unroll=8)

    @pl.when(b == 0)
    def _():
        for cp in weight_copies(bexp_ref[0], 0):
            cp.start(priority=WEIGHT_DMA_PRIORITY)
        wslot[0] = 0

        def zero_rows(lo, hi):
            def zero_body(r, carry):
                tok_ref[r] = 0
                return carry

            lax.fori_loop(lo, hi, zero_body, 0)

        def expert_body(e, carry):
            zero_rows(pad_ref[e], pad_ref[N_EXPERTS + e])
            return carry

        lax.fori_loop(0, N_EXPERTS, expert_body, 0)
        tail = n_used * EXPERT_BLOCK
        zero_rows(tail, jnp.minimum(tail + EXPERT_BLOCK, n_blocks * EXPERT_BLOCK))

        def fill_body(t, carry):
            tok_ref[dest_ref[2 * t]] = t
            tok_ref[dest_ref[2 * t + 1]] = t
            return carry

        lax.fori_loop(0, dest_ref.shape[0] // 2, fill_body, 0, unroll=8)
        issue(0, 0)

    @pl.when(b < n_used)
    def _():
        e = bexp_ref[b]
        new_expert = (b == 0) | (e != bexp_ref[jnp.maximum(b - 1, 0)])

        @pl.when(new_expert)
        def _():
            @pl.when(b > 0)
            def _():
                wslot[0] = 1 - wslot[0]

            s = wslot[0]
            for cp in weight_copies(e, s):
                cp.wait()
            nxt = nxt_ref[b]

            @pl.when(nxt >= 0)
            def _():
                for cp in weight_copies(nxt, 1 - s):
                    cp.start(priority=WEIGHT_DMA_PRIORITY)

            wg_bf[...] = wgbuf[s].astype(BF16)
            wu_bf[...] = wubuf[s].astype(BF16)
            wd_bf[...] = wdbuf[s].astype(BF16)

    def expert_block(cur):
        pltpu.make_async_copy(hp_hbm.at[pl.ds(0, EXPERT_BLOCK)], xbuf.at[cur], gsem.at[cur]).wait()
        base = jnp.minimum(b + 1, n_blocks - 1) * EXPERT_BLOCK
        for r in range(EXPERT_BLOCK):
            gather_copy(tok_ref[base + r], 1 - cur, r).start()
        h = _unpack_bf16_pairs(xbuf[cur]).astype(BF16)
        g = jnp.dot(h, wg_bf[...], preferred_element_type=F32)
        u = jnp.dot(h, wu_bf[...], preferred_element_type=F32)
        a = (g * jax.nn.sigmoid(g) * u).astype(BF16)
        out_ref[...] = _pack_bf16_pairs(jnp.dot(a, wd_bf[...], preferred_element_type=F32))

    for cur in range(2):
        pl.when((b < n_used) & (slot == cur))(functools.partial(expert_block, cur))

    @pl.when(b == n_used)
    def _():
        pltpu.make_async_copy(hp_hbm.at[pl.ds(0, EXPERT_BLOCK)], xbuf.at[slot], gsem.at[slot]).wait()

    @pl.when((b >= n_used) & (b < n_blocks))
    def _():
        out_ref[...] = jnp.zeros_like(out_ref)


def _moe(dest, block_expert, next_expert, pad_bounds, n_used, h_packed, wg, wu, wd):
    n_blocks = block_expert.shape[0]
    n_rows = n_blocks * EXPERT_BLOCK
    half = D_MODEL // 2
    grid_spec = pltpu.PrefetchScalarGridSpec(
        num_scalar_prefetch=5,
        grid=(n_blocks + 1,),
        in_specs=[
            pl.BlockSpec(memory_space=pl.ANY),
            pl.BlockSpec(memory_space=pl.ANY),
            pl.BlockSpec(memory_space=pl.ANY),
            pl.BlockSpec(memory_space=pl.ANY),
        ],
        out_specs=pl.BlockSpec((EXPERT_BLOCK, half), lambda b, *_: (jnp.minimum(b, n_blocks - 1), 0)),
        scratch_shapes=[
            pltpu.VMEM((2, EXPERT_BLOCK, half), jnp.uint32),
            pltpu.SemaphoreType.DMA((2,)),
            pltpu.VMEM((2, D_MODEL, EXPERT_HIDDEN), F32),
            pltpu.VMEM((2, D_MODEL, EXPERT_HIDDEN), F32),
            pltpu.VMEM((2, EXPERT_HIDDEN, D_MODEL), F32),
            pltpu.SemaphoreType.DMA((2,)),
            pltpu.VMEM((D_MODEL, EXPERT_HIDDEN), BF16),
            pltpu.VMEM((D_MODEL, EXPERT_HIDDEN), BF16),
            pltpu.VMEM((EXPERT_HIDDEN, D_MODEL), BF16),
            pltpu.SMEM((1,), jnp.int32),
            pltpu.SMEM((n_rows,), jnp.int32),
        ],
    )
    return pl.pallas_call(
        functools.partial(_moe_kernel, n_blocks=n_blocks),
        out_shape=jax.ShapeDtypeStruct((n_rows, half), jnp.uint32),
        grid_spec=grid_spec,
        compiler_params=_cparams(("arbitrary",)),
        name="moe",
    )(dest, block_expert, next_expert, pad_bounds, n_used, h_packed, wg, wu, wd)


FINAL_TM = 256


def _final_kernel(dest_ref, x1_ref, info_ref, gain_ref, rows_hbm, y_ref, rbuf, sem, *, tok_offset):
    i = pl.program_id(0)
    n = pl.num_programs(0)
    tm = FINAL_TM
    slot = i % 2

    def issue(blk, s):
        base = 2 * (tok_offset + blk * tm)
        for r in range(tm):
            for kk in range(2):
                pltpu.make_async_copy(rows_hbm.at[pl.ds(dest_ref[base + 2 * r + kk], 1)],
                                      rbuf.at[s, kk, pl.ds(r, 1)], sem.at[s]).start(priority=kk)

    @pl.when(i == 0)
    def _():
        issue(0, 0)

    for cur in range(2):
        pl.when((i + 1 < n) & (slot == cur))(functools.partial(issue, i + 1, 1 - cur))

    for kk in range(2):
        pltpu.make_async_copy(rows_hbm.at[pl.ds(0, tm)], rbuf.at[slot, kk], sem.at[slot]).wait()
    info = info_ref[...]
    x = (x1_ref[...] + info[:, 2:3] * _unpack_bf16_pairs(rbuf[slot, 0])
         + info[:, 3:4] * _unpack_bf16_pairs(rbuf[slot, 1]))
    y_ref[...] = x * lax.rsqrt(jnp.mean(x * x, axis=-1, keepdims=True) + NORM_EPS) * gain_ref[...]


def _final(dest, x1_all, info, gain, out_rows, *, tok_offset, n_tok):
    tm = FINAL_TM
    off = tok_offset // tm
    grid_spec = pltpu.PrefetchScalarGridSpec(
        num_scalar_prefetch=1,
        grid=(n_tok // tm,),
        in_specs=[
            pl.BlockSpec((tm, D_MODEL), lambda i, d: (i + off, 0)),
            pl.BlockSpec((tm, LANES), lambda i, d: (i + off, 0)),
            pl.BlockSpec((1, D_MODEL), lambda i, d: (0, 0)),
            pl.BlockSpec(memory_space=pl.ANY),
        ],
        out_specs=pl.BlockSpec((tm, D_MODEL), lambda i, d: (i, 0)),
        scratch_shapes=[pltpu.VMEM((2, 2, tm, D_MODEL // 2), jnp.uint32), pltpu.SemaphoreType.DMA((2,))],
    )
    return pl.pallas_call(
        functools.partial(_final_kernel, tok_offset=tok_offset),
        out_shape=jax.ShapeDtypeStruct((n_tok, D_MODEL), F32),
        grid_spec=grid_spec,
        compiler_params=_cparams(("arbitrary",)),
        name="final",
    )(dest, x1_all, info, gain, out_rows)


def _rot_table(pos, n_rows):
    half = RET_KEY_DIM // 2
    theta = 10000.0 ** (-jnp.linspace(0.0, 1.0, half, dtype=F32))
    ang = pos.astype(F32)[:, None] * theta[None, :]
    cos, sin = jnp.cos(ang), jnp.sin(ang)
    cos128 = jnp.tile(cos, (1, 4))
    sin128 = jnp.tile(jnp.concatenate([-sin, sin], axis=1), (1, 2))
    k_scale = RET_KEY_DIM ** -0.5
    tab = jnp.concatenate([cos128, sin128, cos128 * k_scale, sin128 * k_scale], axis=1)
    return jnp.tile(tab, (n_rows // tab.shape[0], 1))


def kernel(x_prompt, x_sample, cache_diff_k, cache_diff_v, state_retention, norm_mix_gain, w_in, lambda_q1,
           lambda_k1, lambda_q2, lambda_k2, diff_subln_gain, w_ret_out, w_diff_out, w_out, rel_bias_table,
           norm_ffn_gain, w_group_router, b_group_router, w_expert_router, b_expert_router, w_expert_gate,
           w_expert_up, w_expert_down, norm_final_gain):
    B, S, D = x_prompt.shape
    BS, L, _ = x_sample.shape
    past = cache_diff_k.shape[2]
    TP, TS = B * S, BS * L
    T = TP + TS

    w_in_bf = w_in[0].astype(BF16)
    wr_bf = w_ret_out[0].astype(BF16)
    wd_bf = w_diff_out[0].astype(BF16)
    wo_bf = w_out[0].astype(BF16)
    gain_mix = norm_mix_gain[0][None, :]
    gain_ffn = norm_ffn_gain[0][None, :]
    gain_fin = norm_final_gain[None, :]
    gain_sub = diff_subln_gain[0][None, :]
    lam_rows = jnp.zeros((8, LANES), F32).at[0:4, 0:DIFF_HEAD_DIM].set(
        jnp.stack([lambda_q1[0], lambda_k1[0], lambda_q2[0], lambda_k2[0]]))
    rot_p = _rot_table(jnp.arange(S), S)
    rot_s = _rot_table(past + jnp.arange(L), PROJ_TM)

    far = _far_bias(rel_bias_table)[:, None, None]
    qp = ATT_TQ + jnp.arange(ATT_TQ)
    def tile_bias(k_pos):
        bias = (_relative_bias(qp, k_pos, rel_bias_table) - far) * LOG2E
        visible = (k_pos[None, :] // CHUNK) <= (qp[:, None] // CHUNK)
        bias = jnp.swapaxes(jnp.where(visible[None], bias, NEG_INF), 1, 2)
        return jnp.concatenate([bias, bias], axis=2)
    bias_prev = tile_bias(jnp.arange(ATT_TK))
    bias_diag = tile_bias(ATT_TQ + jnp.arange(ATT_TK))
    q_pos_s = past + jnp.arange(L)
    assert SAMPLE_NEAR - L >= MAX_DISTANCE
    b_last = (_relative_bias(q_pos_s, past - SAMPLE_NEAR + jnp.arange(SAMPLE_NEAR), rel_bias_table) - far) * LOG2E
    bias_past = jnp.concatenate([b_last, b_last], axis=1)
    b_new = (_relative_bias(q_pos_s, q_pos_s, rel_bias_table) - far) * LOG2E
    bias_new = jnp.concatenate([b_new, b_new], axis=1)

    xp2 = x_prompt.reshape(TP, D)
    xs2 = x_sample.reshape(TS, D)
    z3p, gp, gatesp, v32p, qt, vt5, kt32 = _proj(xp2, gain_mix, w_in_bf, rot_p, seq=S, transposed_k=True)
    z3s, gs, gatess, v32s, k32s = _proj(xs2, gain_mix, w_in_bf, rot_s, seq=L, transposed_k=False)

    zero_state = jnp.zeros((B, N_HEADS, RET_KEY_DIM, HEAD_V), F32)
    orp, ret_p = _retention(z3p, zero_state, batch=B, seq=S, chunk=256)
    ors, ret_s = _retention(z3s, state_retention[0], batch=BS, seq=L, chunk=L)

    gain_col = jnp.broadcast_to(diff_subln_gain[0][:, None], (HEAD_V, ATT_TQ))
    odp = _attn_prompt(z3p, qt, vt5, bias_prev, bias_diag, lam_rows, gain_col, batch=B, seq=S)
    kc = jnp.transpose(cache_diff_k[0], (0, 2, 3, 4, 1)).reshape(BS, N_HEADS, LANES, past)
    vc = cache_diff_v[0].reshape(BS, past * N_HEADS, LANES)
    ods = _attn_sample(z3s, kc, vc, bias_past, bias_new, lam_rows, gain_sub, batch=BS, n_q=L, past=past)

    mgp = _post_a(orp, gp, odp, gatesp, wr_bf, wd_bf)
    mgs = _post_a(ors, gs, ods, gatess, wr_bf, wd_bf)

    w_rt = jnp.zeros((D, LANES), F32)
    w_rt = w_rt.at[:, 0:N_GROUPS].set(w_group_router[0])
    w_rt = w_rt.at[:, N_GROUPS:N_GROUPS + N_EXPERTS].set(
        jnp.transpose(w_expert_router[0], (1, 0, 2)).reshape(D, N_EXPERTS))
    w_rt_hi = w_rt.astype(BF16)
    w_rt_lo = (w_rt - w_rt_hi.astype(F32)).astype(BF16)
    w_router2 = jnp.concatenate([w_rt_hi, w_rt_lo], axis=1)
    bias_row = jnp.zeros((1, LANES), F32)
    bias_row = bias_row.at[0, 0:N_GROUPS].set(b_group_router[0])
    bias_row = bias_row.at[0, N_GROUPS:N_GROUPS + N_EXPERTS].set(b_expert_router[0].reshape(-1))

    x1_all, lg_all, h_packed = _post_b(xp2, mgp, xs2, mgs, wo_bf, gain_ffn, w_router2)

    info, counts = _route(lg_all, bias_row)
    counts = counts[0, :N_EXPERTS].astype(jnp.int32)
    padded = (counts + EXPERT_BLOCK - 1) // EXPERT_BLOCK * EXPERT_BLOCK
    pad_end = jnp.cumsum(padded)
    offs = pad_end - padded
    e12 = info[:, 0:2].astype(jnp.int32)
    rank12 = info[:, 4:6].astype(jnp.int32)
    expert_ids = jnp.arange(N_EXPERTS, dtype=jnp.int32)
    offs_e = jnp.sum(jnp.where(e12[:, :, None] == expert_ids, offs, 0), axis=-1)
    dest = (offs_e + rank12).reshape(-1)
    n_assign = 2 * T
    n_rows = -(-n_assign // EXPERT_BLOCK) * EXPERT_BLOCK + N_EXPERTS * EXPERT_BLOCK
    n_blocks = n_rows // EXPERT_BLOCK
    block_start = jnp.arange(n_blocks, dtype=jnp.int32) * EXPERT_BLOCK
    block_expert = jnp.minimum(jnp.sum((pad_end[None, :] <= block_start[:, None]).astype(jnp.int32), axis=1),
                               N_EXPERTS - 1)
    n_used = (pad_end[-1:] // EXPERT_BLOCK).astype(jnp.int32)
    later_used = (expert_ids[None, :] > expert_ids[:, None]) & (counts[None, :] > 0)
    next_of_expert = jnp.min(jnp.where(later_used, expert_ids[None, :], N_EXPERTS), axis=1)
    next_of_expert = jnp.where(next_of_expert == N_EXPERTS, -1, next_of_expert)
    next_expert = jnp.sum(jnp.where(block_expert[:, None] == expert_ids[None, :], next_of_expert[None, :], 0),
                          axis=1).astype(jnp.int32)

    pad_bounds = jnp.concatenate([offs + counts, pad_end]).astype(jnp.int32)
    out_rows = _moe(dest, block_expert, next_expert, pad_bounds, n_used, h_packed,
                    w_expert_gate[0], w_expert_up[0], w_expert_down[0])

    y_p = _final(dest, x1_all, info, gain_fin, out_rows, tok_offset=0, n_tok=TP)
    y_s = _final(dest, x1_all, info, gain_fin, out_rows, tok_offset=TP, n_tok=TS)

    new_k_p = jnp.transpose(kt32, (0, 1, 5, 2, 3, 4))
    new_v_p = v32p.reshape(1, B, S, N_HEADS, HEAD_V)
    new_k_s = k32s.reshape(1, BS, L, N_HEADS, 2, DIFF_HEAD_DIM)
    new_v_s = v32s.reshape(1, BS, L, N_HEADS, HEAD_V)
    return (y_p.reshape(B, S, D), y_s.reshape(BS, L, D), new_k_p, new_v_p, ret_p[None],
            new_k_s, new_v_s, ret_s[None])
```

```python
import functools
import math

import jax
import jax.numpy as jnp
from jax import lax
from jax.experimental import pallas as pl
from jax.experimental.pallas import tpu as pltpu

F32 = jnp.float32
BF16 = jnp.bfloat16

D_MODEL = 2048
CHUNK = 64
N_HEADS = 8
RET_KEY_DIM = 64
HEAD_V = 128
DIFF_HEAD_DIM = 64
IN_WIDTH = 10240
NUM_BUCKETS = 32
MAX_DISTANCE = 128
N_GROUPS = 8
EXPERTS_PER_GROUP = 8
N_EXPERTS = 64
EXPERT_HIDDEN = 512
EXPERT_BLOCK = 128
NORM_EPS = 1e-6
NEG_INF = -1e30
LANES = 128
LOG2E = 1.4426950408889634
VT_ROWS = HEAD_V + 16

VMEM_LIMIT = 56 * 1024 * 1024


def _cparams(sem):
    return pltpu.CompilerParams(dimension_semantics=sem, vmem_limit_bytes=VMEM_LIMIT)


PROJ_TM = 512
PROJ_TN = 1024
PROJ_MC = 128
ATT_TK = 256


def _rotate_pairs(acc, cos, sin):
    outs = []
    lane = lax.broadcasted_iota(jnp.int32, (acc.shape[0], LANES), 1)
    first_half = (lane % 64) < 32
    for c in range(acc.shape[1] // LANES):
        xs = acc[:, c * LANES:(c + 1) * LANES]
        swapped = jnp.where(first_half, pltpu.roll(xs, 96, axis=1), pltpu.roll(xs, 32, axis=1))
        outs.append(xs * cos + swapped * sin)
    return outs


def _proj_kernel(x_ref, gain_ref, w_ref, rot_ref, *out_refs, transposed_k):
    if transposed_k:
        z_ref, g_ref, gates_ref, v32_ref, qt_ref, vt_ref, kt32_ref, h_scr = out_refs
    else:
        z_ref, g_ref, gates_ref, v32_ref, k32_ref, h_scr = out_refs
    j = pl.program_id(1)

    @pl.when(j == 0)
    def _():
        x = x_ref[...]
        ms = jnp.mean(x * x, axis=-1, keepdims=True)
        h_scr[...] = (x * lax.rsqrt(ms + NORM_EPS) * gain_ref[...]).astype(BF16)

    n_slab = PROJ_TN // LANES

    def for_row_chunks(epilogue):
        for c in range(PROJ_TM // PROJ_MC):
            rows = slice(c * PROJ_MC, (c + 1) * PROJ_MC)
            epilogue(c, rows, jnp.dot(h_scr[rows, :], w_ref[...], preferred_element_type=F32))

    def store_slabs(rows, vals):
        for s in range(n_slab):
            z_ref[s, rows, :] = vals[s].astype(BF16)

    def split(a):
        return [a[:, s * LANES:(s + 1) * LANES] for s in range(n_slab)]

    @pl.when(j == 0)
    def _():
        def epilogue(c, rows, acc):
            half = PROJ_TN // 2
            q = _rotate_pairs(acc[:, :half], rot_ref[rows, 0:128], rot_ref[rows, 128:256])
            k = _rotate_pairs(acc[:, half:], rot_ref[rows, 256:384], rot_ref[rows, 384:512])
            store_slabs(rows, q + k)
        for_row_chunks(epilogue)

    @pl.when(j == 1)
    def _():
        for_row_chunks(lambda c, rows, acc: store_slabs(rows, split(acc)))

    @pl.when(j == 2)
    def _():
        def epilogue(c, rows, acc):
            g_ref[rows, :] = (acc * jax.nn.sigmoid(acc)).astype(BF16)
        for_row_chunks(epilogue)

    @pl.when(j == 3)
    def _():
        def epilogue(c, rows, acc):
            qs = acc * (DIFF_HEAD_DIM ** -0.5 * LOG2E)
            store_slabs(rows, split(qs))
            if transposed_k:
                qs_t = qs.T.astype(BF16)
                for hh in range(n_slab):
                    qt_ref[0, hh, :, rows] = qs_t[hh * LANES:(hh + 1) * LANES, :]
        for_row_chunks(epilogue)

    @pl.when(j == 4)
    def _():
        def epilogue(c, rows, acc):
            store_slabs(rows, split(acc))
            if transposed_k:
                kt32_ref[0, 0, :, :, :, rows] = acc.T.reshape(n_slab, 2, DIFF_HEAD_DIM, PROJ_MC)
            else:
                k32_ref[rows, :] = acc
        for_row_chunks(epilogue)

    @pl.when(j == 5)
    def _():
        def epilogue(c, rows, acc):
            store_slabs(rows, split(acc))
            v32_ref[rows, :] = acc
            if transposed_k:
                acc_tb = acc.T.astype(BF16)
                t, cols = (c * PROJ_MC) // ATT_TK, slice((c * PROJ_MC) % ATT_TK, (c * PROJ_MC) % ATT_TK + PROJ_MC)
                ones = jnp.ones((VT_ROWS - HEAD_V, PROJ_MC), BF16)
                for hh in range(n_slab):
                    vt_ref[0, hh, t, 0:HEAD_V, cols] = acc_tb[hh * LANES:(hh + 1) * LANES, :]
                    vt_ref[0, hh, t, HEAD_V:VT_ROWS, cols] = ones
        for_row_chunks(epilogue)

    @pl.when(j >= 6)
    def _():
        def epilogue(c, rows, acc):
            gates_ref[rows, :] = jax.nn.sigmoid(acc).astype(BF16)
        for_row_chunks(epilogue)


def _proj(x2d, gain, w_bf, rot, *, seq, transposed_k):
    T = x2d.shape[0]
    tm, tn = PROJ_TM, PROJ_TN
    ni, nj = T // tm, IN_WIDTH // tn
    rot_blocks = rot.shape[0] // tm

    def zmap(i, j):
        zj = jnp.where(j < 2, j, jnp.where(j < 3, 1, jnp.where(j < 6, j - 1, 4)))
        return (zj, i, 0)

    out_shape = [
        jax.ShapeDtypeStruct((40, T, LANES), BF16),
        jax.ShapeDtypeStruct((T, 1024), BF16),
        jax.ShapeDtypeStruct((T, 4096), BF16),
        jax.ShapeDtypeStruct((T, 1024), F32),
    ]
    out_specs = [
        pl.BlockSpec((8, tm, LANES), zmap),
        pl.BlockSpec((tm, tn), lambda i, j: (i, 0)),
        pl.BlockSpec((tm, tn), lambda i, j: (i, jnp.clip(j - 6, 0, 3))),
        pl.BlockSpec((tm, tn), lambda i, j: (i, 0)),
    ]
    if transposed_k:
        B = T // seq
        spb = seq // tm
        out_shape += [
            jax.ShapeDtypeStruct((B, N_HEADS, LANES, seq), BF16),
            jax.ShapeDtypeStruct((B, N_HEADS, seq // ATT_TK, VT_ROWS, ATT_TK), BF16),
            jax.ShapeDtypeStruct((1, B, N_HEADS, 2, DIFF_HEAD_DIM, seq), F32),
        ]
        out_specs += [
            pl.BlockSpec((1, N_HEADS, LANES, tm), lambda i, j: (i // spb, 0, 0, i % spb)),
            pl.BlockSpec((1, N_HEADS, tm // ATT_TK, VT_ROWS, ATT_TK), lambda i, j: (i // spb, 0, i % spb, 0, 0)),
            pl.BlockSpec((1, 1, N_HEADS, 2, DIFF_HEAD_DIM, tm), lambda i, j: (0, i // spb, 0, 0, 0, i % spb)),
        ]
    else:
        out_shape += [jax.ShapeDtypeStruct((T, 1024), F32)]
        out_specs += [pl.BlockSpec((tm, tn), lambda i, j: (i, 0))]

    return pl.pallas_call(
        functools.partial(_proj_kernel, transposed_k=transposed_k),
        out_shape=out_shape,
        grid=(ni, nj),
        in_specs=[
            pl.BlockSpec((tm, D_MODEL), lambda i, j: (i, 0)),
            pl.BlockSpec((1, D_MODEL), lambda i, j: (0, 0)),
            pl.BlockSpec((D_MODEL, tn), lambda i, j: (0, j)),
            pl.BlockSpec((tm, 512), lambda i, j: (i % rot_blocks, 0)),
        ],
        out_specs=out_specs,
        scratch_shapes=[pltpu.VMEM((tm, D_MODEL), BF16)],
        compiler_params=_cparams(("arbitrary", "arbitrary")),
        name="proj_t" if transposed_k else "proj",
    )(x2d, gain, w_bf, rot)


def _retention_kernel(q_ref, k_ref, v_ref, decay_ref, inner_ref, outer_ref, st0_ref,
                      o_ref, st_out_ref, st_scr, *, chunk):
    c = pl.program_id(1)
    nc = pl.num_programs(1)

    @pl.when(c == 0)
    def _():
        st_scr[...] = jnp.zeros_like(st_scr)
        for h in range(N_HEADS):
            a = h % 2
            st_scr[h, a * RET_KEY_DIM:(a + 1) * RET_KEY_DIM, :] = st0_ref[0, h]

    lane_lo = lax.broadcasted_iota(jnp.int32, (chunk, LANES), 1) < RET_KEY_DIM
    row_lo = lax.broadcasted_iota(jnp.int32, (LANES, LANES), 0) < RET_KEY_DIM
    for h in range(N_HEADS):
        p, a = h // 2, h % 2
        q = q_ref[p]
        k = k_ref[p]
        v = v_ref[h]
        qa = jnp.where(lane_lo == (a == 0), q, jnp.zeros_like(q))
        s = lax.dot_general(qa, k, (((1,), (1,)), ((), ())), preferred_element_type=F32)
        s = s * decay_ref[h]
        st = st_scr[h]
        inner = inner_ref[h]
        o = (jnp.dot(s.astype(BF16), v, preferred_element_type=F32)
             + jnp.dot(qa, st.astype(BF16), preferred_element_type=F32) * inner)
        ko = (k.astype(F32) * outer_ref[p]).astype(BF16)
        upd = lax.dot_general(ko, v, (((0,), (0,)), ((), ())), preferred_element_type=F32)
        upd = jnp.where(row_lo == (a == 0), upd, 0.0)
        chunk_decay = inner[chunk - 1:chunk, :]
        st_scr[h] = chunk_decay * st + upd
        o_n = o * lax.rsqrt(jnp.mean(o * o, axis=-1, keepdims=True) + NORM_EPS)
        o_ref[:, h * HEAD_V:(h + 1) * HEAD_V] = o_n.astype(BF16)

    @pl.when(c == nc - 1)
    def _():
        for h in range(N_HEADS):
            a = h % 2
            st_out_ref[0, h] = st_scr[h, a * RET_KEY_DIM:(a + 1) * RET_KEY_DIM, :]


def _retention_consts(chunk):
    log_g = jnp.log(1.0 - 2.0 ** (-5.0 - jnp.arange(N_HEADS, dtype=F32)))
    j = jnp.arange(chunk, dtype=F32)
    diff = j[:, None] - j[None, :]
    decay = jnp.where(diff >= 0, jnp.exp(log_g[:, None, None] * jnp.maximum(diff, 0.0)), 0.0)
    inner = jnp.exp(log_g[:, None] * (j + 1.0))
    outer = jnp.exp(log_g[:, None] * (chunk - 1.0 - j))
    inner_b = jnp.broadcast_to(inner[:, :, None], (N_HEADS, chunk, LANES))
    outer_pair = jnp.repeat(outer.reshape(N_HEADS // 2, 2, chunk).transpose(0, 2, 1), RET_KEY_DIM, axis=2)
    return decay, inner_b, outer_pair


def _retention(z3, state0, *, batch, seq, chunk):
    T = batch * seq
    nc = seq // chunk
    decay, inner_b, outer_pair = _retention_consts(chunk)
    return pl.pallas_call(
        functools.partial(_retention_kernel, chunk=chunk),
        out_shape=[jax.ShapeDtypeStruct((T, 1024), BF16),
                   jax.ShapeDtypeStruct((batch, N_HEADS, RET_KEY_DIM, HEAD_V), F32)],
        grid=(batch, nc),
        in_specs=[
            pl.BlockSpec((4, chunk, LANES), lambda b, c: (0, b * nc + c, 0)),
            pl.BlockSpec((4, chunk, LANES), lambda b, c: (1, b * nc + c, 0)),
            pl.BlockSpec((8, chunk, LANES), lambda b, c: (1, b * nc + c, 0)),
            pl.BlockSpec((N_HEADS, chunk, chunk), lambda b, c: (0, 0, 0)),
            pl.BlockSpec((N_HEADS, chunk, LANES), lambda b, c: (0, 0, 0)),
            pl.BlockSpec((N_HEADS // 2, chunk, LANES), lambda b, c: (0, 0, 0)),
            pl.BlockSpec((1, N_HEADS, RET_KEY_DIM, HEAD_V), lambda b, c: (b, 0, 0, 0)),
        ],
        out_specs=[
            pl.BlockSpec((chunk, 1024), lambda b, c: (b * nc + c, 0)),
            pl.BlockSpec((1, N_HEADS, RET_KEY_DIM, HEAD_V), lambda b, c: (b, 0, 0, 0)),
        ],
        scratch_shapes=[pltpu.VMEM((N_HEADS, LANES, LANES), F32)],
        compiler_params=_cparams(("arbitrary", "arbitrary")),
        name=f"retention_c{chunk}",
    )(z3, z3, z3, decay, inner_b, outer_pair, state0)


def _lam_init(layer=0):
    return 0.8 - 0.6 * math.exp(-0.3 * layer)


def _lam_from_ref(lam_ref):
    lp = lam_ref[...]
    s1 = jnp.sum(lp[0:1] * lp[1:2], axis=-1, keepdims=True)
    s2 = jnp.sum(lp[2:3] * lp[3:4], axis=-1, keepdims=True)
    return jnp.exp(s1) - jnp.exp(s2) + _lam_init()


def _stack_q(q):
    lane_lo = lax.broadcasted_iota(jnp.int32, q.shape, 1) < DIFF_HEAD_DIM
    zero = jnp.zeros_like(q)
    return jnp.concatenate([jnp.where(lane_lo, q, zero), jnp.where(lane_lo, zero, q)], axis=0)


def _with_ones(v):
    return jnp.concatenate([v, jnp.ones_like(v)], axis=1)


def _diff_finish(acc_ref, lam, gain, n):
    o = acc_ref[:, 0:HEAD_V] / acc_ref[:, HEAD_V:2 * HEAD_V]
    o = o[:n] - lam * o[n:]
    o = o * lax.rsqrt(jnp.mean(o * o, axis=-1, keepdims=True) + NORM_EPS) * gain
    return o * (1.0 - _lam_init())


def _relative_bias(q_pos, k_pos, table):
    rel = k_pos[None, :] - q_pos[:, None]
    half = NUM_BUCKETS // 2
    max_exact = half // 2
    n = jnp.abs(rel)
    log_ratio = jnp.log(jnp.maximum(n, 1).astype(F32) / max_exact) / math.log(MAX_DISTANCE / max_exact)
    large = jnp.minimum(max_exact + (log_ratio * (half - max_exact)).astype(jnp.int32), half - 1)
    bucket = (rel > 0).astype(jnp.int32) * half + jnp.where(n < max_exact, n, large)
    out = jnp.zeros((table.shape[1],) + bucket.shape, F32)
    for bkt in range(NUM_BUCKETS):
        out = jnp.where(bucket[None] == bkt, table[bkt].astype(F32)[:, None, None], out)
    return out


def _far_bias(table):
    return table[NUM_BUCKETS // 2 - 1].astype(F32)


ATT_TQ = 256
ATT_HP = 8


def _attn_prompt_kernel(qt_ref, k_ref, vt_ref, bprev_ref, bdiag_ref, lam_ref, gain_ref, o_ref, acc_scr):
    i = pl.program_id(2)
    row_lo = lax.broadcasted_iota(jnp.int32, (LANES, ATT_TQ), 0) < DIFF_HEAD_DIM
    qs = []
    for hh in range(ATT_HP):
        qt = qt_ref[0, hh]
        zero = jnp.zeros_like(qt)
        qs.append(jnp.concatenate([jnp.where(row_lo, qt, zero), jnp.where(row_lo, zero, qt)], axis=1))
    acc_scr[...] = jnp.zeros_like(acc_scr)

    def tile(j, ms, biases):
        ss = []
        for hh in range(ATT_HP):
            kt = k_ref[hh, pl.ds(pl.multiple_of(j * ATT_TK, ATT_TK), ATT_TK), :]
            s = jnp.dot(kt, qs[hh], preferred_element_type=F32)
            ss.append(s if biases is None else s + biases[hh])
        ps, alphas, m_out = [], [], []
        for hh in range(ATT_HP):
            m_new = jnp.maximum(ms[hh], jnp.max(ss[hh], axis=0, keepdims=True))
            alphas.append(jnp.exp2(ms[hh] - m_new))
            ps.append(jnp.exp2(ss[hh] - m_new).astype(BF16))
            m_out.append(m_new)
        for hh in range(ATT_HP):
            acc_scr[hh] = alphas[hh] * acc_scr[hh] + jnp.dot(vt_ref[0, hh, j], ps[hh],
                                                             preferred_element_type=F32)
        return tuple(m_out)

    ms = tuple(jnp.full((1, 2 * ATT_TQ), NEG_INF, F32) for _ in range(ATT_HP))
    ms = lax.fori_loop(0, i - 1, lambda j, c: tile(j, c, None), ms)
    jp = jnp.maximum(i - 1, 0)
    ms = lax.cond(i >= 1,
                  lambda c: tile(jp, c, [bprev_ref[hh] for hh in range(ATT_HP)]),
                  lambda c: c, ms)
    tile(i, ms, [bdiag_ref[hh] for hh in range(ATT_HP)])

    lam = _lam_from_ref(lam_ref)
    for hh in range(ATT_HP):
        o = acc_scr[hh, 0:HEAD_V, :] / acc_scr[hh, HEAD_V:HEAD_V + 1, :]
        o = o[:, :ATT_TQ] - lam * o[:, ATT_TQ:]
        o = o * lax.rsqrt(jnp.mean(o * o, axis=0, keepdims=True) + NORM_EPS) * gain_ref[...]
        o_ref[:, hh * HEAD_V:(hh + 1) * HEAD_V] = (o * (1.0 - _lam_init())).T.astype(BF16)


def _attn_prompt(z3, qt, vt5, bias_prev, bias_diag, lam_rows, gain_col, *, batch, seq):
    T = batch * seq
    nq = seq // ATT_TQ
    hp = ATT_HP
    return pl.pallas_call(
        _attn_prompt_kernel,
        out_shape=jax.ShapeDtypeStruct((T, 1024), BF16),
        grid=(batch, N_HEADS // hp, nq),
        in_specs=[
            pl.BlockSpec((1, hp, LANES, ATT_TQ), lambda b, g, i: (b, g, 0, i)),
            pl.BlockSpec((hp, seq, LANES), lambda b, g, i: (24 // hp + g, b, 0), pipeline_mode=pl.Buffered(1)),
            pl.BlockSpec((1, hp, seq // ATT_TK, VT_ROWS, ATT_TK), lambda b, g, i: (b, g, 0, 0, 0),
                         pipeline_mode=pl.Buffered(1)),
            pl.BlockSpec((hp, ATT_TK, 2 * ATT_TQ), lambda b, g, i: (g, 0, 0), pipeline_mode=pl.Buffered(1)),
            pl.BlockSpec((hp, ATT_TK, 2 * ATT_TQ), lambda b, g, i: (g, 0, 0), pipeline_mode=pl.Buffered(1)),
            pl.BlockSpec((8, LANES), lambda b, g, i: (0, 0)),
            pl.BlockSpec((LANES, ATT_TQ), lambda b, g, i: (0, 0)),
        ],
        out_specs=pl.BlockSpec((ATT_TQ, hp * HEAD_V), lambda b, g, i: (b * nq + i, g)),
        scratch_shapes=[pltpu.VMEM((hp, VT_ROWS, 2 * ATT_TQ), F32)],
        compiler_params=_cparams(("arbitrary", "arbitrary", "arbitrary")),
        name="attn_prompt",
    )(qt, z3, vt5, bias_prev, bias_diag, lam_rows, gain_col)


SAMPLE_TK = 2048
SAMPLE_NEAR = 256


def _attn_sample_kernel(q_ref, kc_ref, vc_ref, kn_ref, vn_ref, bpast_ref, bnew_ref, lam_ref, gain_ref,
                        o_ref, m_scr, acc_scr, *, n_past_tiles, n_q):
    t = pl.program_id(1)

    @pl.when(t == 0)
    def _():
        m_scr[...] = jnp.full_like(m_scr, NEG_INF)
        acc_scr[...] = jnp.zeros_like(acc_scr)

    def update(logits, values):
        ps, alphas = [], []
        for h in range(N_HEADS):
            m_prev = m_scr[h]
            m_new = jnp.maximum(m_prev, jnp.max(logits[h], axis=1, keepdims=True))
            alpha = jnp.exp2(m_prev - m_new)
            ps.append(jnp.exp2(logits[h] - m_new[:, 0:1]).astype(BF16))
            alphas.append(jnp.concatenate([alpha, alpha], axis=1))
            m_scr[h] = m_new
        for h in range(N_HEADS):
            acc_scr[h] = alphas[h] * acc_scr[h] + jnp.dot(ps[h], _with_ones(values[h]()),
                                                          preferred_element_type=F32)

    @pl.when(t < n_past_tiles)
    def _():
        is_last = jnp.where(t == n_past_tiles - 1, 1.0, 0.0)
        far_w = SAMPLE_TK - SAMPLE_NEAR
        logits = []
        for h in range(N_HEADS):
            kt = kc_ref[0, h].astype(BF16)
            s = jnp.dot(_stack_q(q_ref[h]), kt, preferred_element_type=F32)
            logits.append(jnp.concatenate([s[:, :far_w], s[:, far_w:] + is_last * bpast_ref[h]], axis=1))
        update(logits, [lambda h=h: vc_ref[0, pl.ds(h, SAMPLE_TK, stride=N_HEADS), :].astype(BF16)
                        for h in range(N_HEADS)])

    @pl.when(t == n_past_tiles)
    def _():
        logits = []
        for h in range(N_HEADS):
            s = lax.dot_general(_stack_q(q_ref[h]), kn_ref[h], (((1,), (1,)), ((), ())),
                                preferred_element_type=F32)
            logits.append(s + bnew_ref[h])
        update(logits, [lambda h=h: vn_ref[h] for h in range(N_HEADS)])
        lam = _lam_from_ref(lam_ref)
        for h in range(N_HEADS):
            o = _diff_finish(acc_scr.at[h], lam, gain_ref[...], n_q)
            o_ref[:, h * HEAD_V:(h + 1) * HEAD_V] = o.astype(BF16)


def _attn_sample(z3, kc, vc, bias_past, bias_new, lam_rows, gain, *, batch, n_q, past):
    npt = past // SAMPLE_TK
    return pl.pallas_call(
        functools.partial(_attn_sample_kernel, n_past_tiles=npt, n_q=n_q),
        out_shape=jax.ShapeDtypeStruct((batch * n_q, 1024), BF16),
        grid=(batch, npt + 1),
        in_specs=[
            pl.BlockSpec((8, n_q, LANES), lambda b, t: (2, b, 0)),
            pl.BlockSpec((1, N_HEADS, LANES, SAMPLE_TK),
                         lambda b, t: (jnp.minimum(b + t // npt, batch - 1), 0, 0, t % npt)),
            pl.BlockSpec((1, SAMPLE_TK * N_HEADS, LANES),
                         lambda b, t: (jnp.minimum(b + t // npt, batch - 1), t % npt, 0)),
            pl.BlockSpec((8, n_q, LANES), lambda b, t: (3, b, 0)),
            pl.BlockSpec((8, n_q, LANES), lambda b, t: (4, b, 0)),
            pl.BlockSpec((N_HEADS, 2 * n_q, SAMPLE_NEAR), lambda b, t: (0, 0, 0)),
            pl.BlockSpec((N_HEADS, 2 * n_q, n_q), lambda b, t: (0, 0, 0)),
            pl.BlockSpec((8, LANES), lambda b, t: (0, 0)),
            pl.BlockSpec((1, LANES), lambda b, t: (0, 0)),
        ],
        out_specs=pl.BlockSpec((n_q, 1024), lambda b, t: (b, 0)),
        scratch_shapes=[pltpu.VMEM((N_HEADS, 2 * n_q, LANES), F32),
                        pltpu.VMEM((N_HEADS, 2 * n_q, 2 * LANES), F32)],
        compiler_params=_cparams(("arbitrary", "arbitrary")),
        name="attn_sample",
    )(z3, kc, vc, z3, z3, bias_past, bias_new, lam_rows, gain)


POST_TM = 256


def _post_a_kernel(or_ref, g_ref, od_ref, gr_ref, gd_ref, wr_ref, wd_ref, out_ref):
    a = jnp.dot(or_ref[...] * g_ref[...], wr_ref[...], preferred_element_type=F32)
    b = jnp.dot(od_ref[...], wd_ref[...], preferred_element_type=F32)
    out_ref[...] = (gr_ref[...].astype(F32) * a + gd_ref[...].astype(F32) * b).astype(BF16)


def _post_a(o_r, g, o_d, gates, wr_bf, wd_bf):
    T = o_r.shape[0]
    tm = POST_TM
    return pl.pallas_call(
        _post_a_kernel,
        out_shape=jax.ShapeDtypeStruct((T, D_MODEL), BF16),
        grid=(T // tm,),
        in_specs=[
            pl.BlockSpec((tm, 1024), lambda i: (i, 0)),
            pl.BlockSpec((tm, 1024), lambda i: (i, 0)),
            pl.BlockSpec((tm, 1024), lambda i: (i, 0)),
            pl.BlockSpec((tm, D_MODEL), lambda i: (i, 0)),
            pl.BlockSpec((tm, D_MODEL), lambda i: (i, 1)),
            pl.BlockSpec((1024, D_MODEL), lambda i: (0, 0)),
            pl.BlockSpec((1024, D_MODEL), lambda i: (0, 0)),
        ],
        out_specs=pl.BlockSpec((tm, D_MODEL), lambda i: (i, 0)),
        compiler_params=_cparams(("arbitrary",)),
        name="post_a",
    )(o_r, g, o_d, gates, gates, wr_bf, wd_bf)


def _pack_bf16_pairs(x):
    w = x.shape[1] // 2
    xb = x.astype(BF16).astype(F32)
    lo = pltpu.bitcast(xb[:, :w], jnp.uint32) >> 16
    hi = pltpu.bitcast(xb[:, w:], jnp.uint32) & jnp.uint32(0xFFFF0000)
    return lo | hi


def _unpack_bf16_pairs(words):
    lo = pltpu.bitcast(words << 16, F32)
    hi = pltpu.bitcast(words & jnp.uint32(0xFFFF0000), F32)
    return jnp.concatenate([lo, hi], axis=1)


def _post_b_kernel(xp_ref, mp_ref, xs_ref, ms_ref, wo_ref, gain_ref, wrt_ref, x1_ref, lg_ref, hp_ref, *,
                   n_prompt_blocks):
    i = pl.program_id(0)
    tm = POST_TM

    def body(x_ref, mg_ref):
        x1 = x_ref[...] + jnp.dot(mg_ref[...], wo_ref[...], preferred_element_type=F32)
        x1_ref[...] = x1
        h2 = x1 * lax.rsqrt(jnp.mean(x1 * x1, axis=-1, keepdims=True) + NORM_EPS) * gain_ref[...]
        hp_ref[...] = _pack_bf16_pairs(h2)
        h_hi = h2.astype(BF16)
        h_lo = (h2 - h_hi.astype(F32)).astype(BF16)
        r = jnp.dot(jnp.concatenate([h_hi, h_lo], axis=0), wrt_ref[...], preferred_element_type=F32)
        lg_ref[...] = (r[:tm, :LANES] + r[:tm, LANES:]) + (r[tm:, :LANES] + r[tm:, LANES:])

    @pl.when(i < n_prompt_blocks)
    def _():
        body(xp_ref, mp_ref)

    @pl.when(i >= n_prompt_blocks)
    def _():
        body(xs_ref, ms_ref)


def _post_b(xp2, mgp, xs2, mgs, wo_bf, gain, w_router2):
    tm = POST_TM
    npb, nsb = xp2.shape[0] // tm, xs2.shape[0] // tm
    T = xp2.shape[0] + xs2.shape[0]
    pmap = lambda i: (jnp.minimum(i, npb - 1), 0)
    smap = lambda i: (jnp.maximum(i - npb, 0), 0)
    return pl.pallas_call(
        functools.partial(_post_b_kernel, n_prompt_blocks=npb),
        out_shape=[jax.ShapeDtypeStruct((T, D_MODEL), F32),
                   jax.ShapeDtypeStruct((T, LANES), F32),
                   jax.ShapeDtypeStruct((T, D_MODEL // 2), jnp.uint32)],
        grid=(npb + nsb,),
        in_specs=[
            pl.BlockSpec((tm, D_MODEL), pmap),
            pl.BlockSpec((tm, D_MODEL), pmap),
            pl.BlockSpec((tm, D_MODEL), smap),
            pl.BlockSpec((tm, D_MODEL), smap),
            pl.BlockSpec((D_MODEL, D_MODEL), lambda i: (0, 0)),
            pl.BlockSpec((1, D_MODEL), lambda i: (0, 0)),
            pl.BlockSpec((D_MODEL, 2 * LANES), lambda i: (0, 0)),
        ],
        out_specs=[pl.BlockSpec((tm, D_MODEL), lambda i: (i, 0)),
                   pl.BlockSpec((tm, LANES), lambda i: (i, 0)),
                   pl.BlockSpec((tm, D_MODEL // 2), lambda i: (i, 0))],
        compiler_params=_cparams(("arbitrary",)),
        name="post_b",
    )(xp2, mgp, xs2, mgs, wo_bf, gain, w_router2)


ROUTE_TM = 512


def _route_kernel(lg_ref, bias_ref, info_ref, cnt_ref, tri_scr, carry_scr):
    i = pl.program_id(0)
    tm = ROUTE_TM

    @pl.when(i == 0)
    def _():
        r = lax.broadcasted_iota(jnp.int32, (tm, tm), 0)
        c = lax.broadcasted_iota(jnp.int32, (tm, tm), 1)
        tri_scr[...] = jnp.where(c < r, 1.0, 0.0).astype(BF16)
        carry_scr[...] = jnp.zeros_like(carry_scr)

    lg = lg_ref[...] + bias_ref[...]
    lane = lax.broadcasted_iota(jnp.int32, (tm, LANES), 1)
    lane_f = lane.astype(F32)
    neg = jnp.float32(-jnp.inf)

    def first_argmax(vals):
        top = jnp.max(vals, axis=1, keepdims=True)
        idx = jnp.min(jnp.where(vals == top, lane_f, float(LANES)), axis=1, keepdims=True)
        return top, idx

    is_group = lane < N_GROUPS
    gl = jnp.where(is_group, lg, neg)
    g_top, g_idx = first_argmax(gl)
    g_weight = 1.0 / jnp.sum(jnp.exp(gl - g_top), axis=1, keepdims=True)
    lane_group = ((lane - N_GROUPS) >> 3).astype(F32)
    in_group = (lane >= N_GROUPS) & (lane < N_GROUPS + N_EXPERTS) & (lane_group == g_idx)
    el = jnp.where(in_group, lg, neg)
    t1, i1 = first_argmax(el)
    el2 = jnp.where(lane_f == i1, neg, el)
    t2, i2 = first_argmax(el2)
    e2w = jnp.exp(t2 - t1)
    p1 = 1.0 / (1.0 + e2w)
    gate1 = g_weight * p1
    gate2 = g_weight * (e2w * p1)
    e1 = i1 - float(N_GROUPS)
    e2 = i2 - float(N_GROUPS)

    hot1 = lane_f == e1
    hot2 = lane_f == e2
    both = jnp.where(hot1 | hot2, 1.0, 0.0)
    prefix = jnp.dot(tri_scr[...], both.astype(BF16), preferred_element_type=F32) + carry_scr[...]
    rank1 = jnp.sum(jnp.where(hot1, prefix, 0.0), axis=1, keepdims=True)
    rank2 = jnp.sum(jnp.where(hot2, prefix, 0.0), axis=1, keepdims=True)
    carry_scr[...] = carry_scr[...] + jnp.sum(both, axis=0, keepdims=True)

    info = jnp.where(lane == 0, e1, jnp.where(lane == 1, e2, jnp.where(lane == 2, gate1, jnp.where(
        lane == 3, gate2, jnp.where(lane == 4, rank1, jnp.where(lane == 5, rank2, 0.0))))))
    info_ref[...] = info
    cnt_ref[...] = carry_scr[...]


def _route(lg_all, bias_row):
    T = lg_all.shape[0]
    tm = ROUTE_TM
    return pl.pallas_call(
        _route_kernel,
        out_shape=[jax.ShapeDtypeStruct((T, LANES), F32), jax.ShapeDtypeStruct((1, LANES), F32)],
        grid=(T // tm,),
        in_specs=[pl.BlockSpec((tm, LANES), lambda i: (i, 0)), pl.BlockSpec((1, LANES), lambda i: (0, 0))],
        out_specs=[pl.BlockSpec((tm, LANES), lambda i: (i, 0)), pl.BlockSpec((1, LANES), lambda i: (0, 0))],
        scratch_shapes=[pltpu.VMEM((tm, tm), BF16), pltpu.VMEM((1, LANES), F32)],
        compiler_params=_cparams(("arbitrary",)),
        name="route",
    )(lg_all, bias_row)


WEIGHT_DMA_PRIORITY = 1


def _moe_kernel(dest_ref, bexp_ref, nxt_ref, pad_ref, nused_ref, hp_hbm, wg_hbm, wu_hbm, wd_hbm,
                out_ref, xbuf, gsem, wgbuf, wubuf, wdbuf, wsem, wg_bf, wu_bf, wd_bf, wslot, tok_ref, *,
                n_blocks):
    b = pl.program_id(0)
    n_used = nused_ref[0]
    slot = b % 2

    def gather_copy(row, s, r):
        return pltpu.make_async_copy(hp_hbm.at[pl.ds(row, 1)], xbuf.at[s, pl.ds(r, 1)], gsem.at[s])

    def weight_copies(e, s):
        return (pltpu.make_async_copy(wg_hbm.at[e], wgbuf.at[s], wsem.at[s]),
                pltpu.make_async_copy(wu_hbm.at[e], wubuf.at[s], wsem.at[s]),
                pltpu.make_async_copy(wd_hbm.at[e], wdbuf.at[s], wsem.at[s]))

    def issue(blk, s):
        base = blk * EXPERT_BLOCK

        def body(r, carry):
            gather_copy(tok_ref[base + r], s, r).start()
            return carry

        lax.fori_loop(0, EXPERT_BLOCK, body, 0, unroll=8)

    @pl.when(b == 0)
    def _():
        for cp in weight_copies(bexp_ref[0], 0):
            cp.start(priority=WEIGHT_DMA_PRIORITY)
        wslot[0] = 0

        def zero_rows(lo, hi):
            def zero_body(r, carry):
                tok_ref[r] = 0
                return carry

            lax.fori_loop(lo, hi, zero_body, 0)

        def expert_body(e, carry):
            zero_rows(pad_ref[e], pad_ref[N_EXPERTS + e])
            return carry

        lax.fori_loop(0, N_EXPERTS, expert_body, 0)
        tail = n_used * EXPERT_BLOCK
        zero_rows(tail, jnp.minimum(tail + EXPERT_BLOCK, n_blocks * EXPERT_BLOCK))

        def fill_body(t, carry):
            tok_ref[dest_ref[2 * t]] = t
            tok_ref[dest_ref[2 * t + 1]] = t
            return carry

        lax.fori_loop(0, dest_ref.shape[0] // 2, fill_body, 0, unroll=8)
        issue(0, 0)

    @pl.when(b < n_used)
    def _():
        e = bexp_ref[b]
        new_expert = (b == 0) | (e != bexp_ref[jnp.maximum(b - 1, 0)])

        @pl.when(new_expert)
        def _():
            @pl.when(b > 0)
            def _():
                wslot[0] = 1 - wslot[0]

            s = wslot[0]
            for cp in weight_copies(e, s):
                cp.wait()
            nxt = nxt_ref[b]

            @pl.when(nxt >= 0)
            def _():
                for cp in weight_copies(nxt, 1 - s):
                    cp.start(priority=WEIGHT_DMA_PRIORITY)

            wg_bf[...] = wgbuf[s].astype(BF16)
            wu_bf[...] = wubuf[s].astype(BF16)
            wd_bf[...] = wdbuf[s].astype(BF16)

    def expert_block(cur):
        pltpu.make_async_copy(hp_hbm.at[pl.ds(0, EXPERT_BLOCK)], xbuf.at[cur], gsem.at[cur]).wait()
        h = _unpack_bf16_pairs(xbuf[cur]).astype(BF16)
        g = jnp.dot(h, wg_bf[...], preferred_element_type=F32)
        u = jnp.dot(h, wu_bf[...], preferred_element_type=F32)
        a = (g * jax.nn.sigmoid(g) * u).astype(BF16)
        out_ref[...] = _pack_bf16_pairs(jnp.dot(a, wd_bf[...], preferred_element_type=F32))
        base = jnp.minimum(b + 1, n_blocks - 1) * EXPERT_BLOCK
        for r in range(EXPERT_BLOCK):
            gather_copy(tok_ref[base + r], 1 - cur, r).start()

    for cur in range(2):
        pl.when((b < n_used) & (slot == cur))(functools.partial(expert_block, cur))

    @pl.when(b == n_used)
    def _():
        pltpu.make_async_copy(hp_hbm.at[pl.ds(0, EXPERT_BLOCK)], xbuf.at[slot], gsem.at[slot]).wait()

    @pl.when((b >= n_used) & (b < n_blocks))
    def _():
        out_ref[...] = jnp.zeros_like(out_ref)


def _moe(dest, block_expert, next_expert, pad_bounds, n_used, h_packed, wg, wu, wd):
    n_blocks = block_expert.shape[0]
    n_rows = n_blocks * EXPERT_BLOCK
    half = D_MODEL // 2
    grid_spec = pltpu.PrefetchScalarGridSpec(
        num_scalar_prefetch=5,
        grid=(n_blocks + 1,),
        in_specs=[
            pl.BlockSpec(memory_space=pl.ANY),
            pl.BlockSpec(memory_space=pl.ANY),
            pl.BlockSpec(memory_space=pl.ANY),
            pl.BlockSpec(memory_space=pl.ANY),
        ],
        out_specs=pl.BlockSpec((EXPERT_BLOCK, half), lambda b, *_: (jnp.minimum(b, n_blocks - 1), 0)),
        scratch_shapes=[
            pltpu.VMEM((2, EXPERT_BLOCK, half), jnp.uint32),
            pltpu.SemaphoreType.DMA((2,)),
            pltpu.VMEM((2, D_MODEL, EXPERT_HIDDEN), F32),
            pltpu.VMEM((2, D_MODEL, EXPERT_HIDDEN), F32),
            pltpu.VMEM((2, EXPERT_HIDDEN, D_MODEL), F32),
            pltpu.SemaphoreType.DMA((2,)),
            pltpu.VMEM((D_MODEL, EXPERT_HIDDEN), BF16),
            pltpu.VMEM((D_MODEL, EXPERT_HIDDEN), BF16),
            pltpu.VMEM((EXPERT_HIDDEN, D_MODEL), BF16),
            pltpu.SMEM((1,), jnp.int32),
            pltpu.SMEM((n_rows,), jnp.int32),
        ],
    )
    return pl.pallas_call(
        functools.partial(_moe_kernel, n_blocks=n_blocks),
        out_shape=jax.ShapeDtypeStruct((n_rows, half), jnp.uint32),
        grid_spec=grid_spec,
        compiler_params=_cparams(("arbitrary",)),
        name="moe",
    )(dest, block_expert, next_expert, pad_bounds, n_used, h_packed, wg, wu, wd)


FINAL_TM = 256


def _final_kernel(dest_ref, x1_ref, info_ref, gain_ref, rows_hbm, y_ref, rbuf, sem, *, tok_offset):
    i = pl.program_id(0)
    n = pl.num_programs(0)
    tm = FINAL_TM
    slot = i % 2

    def issue(blk, s):
        base = 2 * (tok_offset + blk * tm)
        for r in range(tm):
            for kk in range(2):
                pltpu.make_async_copy(rows_hbm.at[pl.ds(dest_ref[base + 2 * r + kk], 1)],
                                      rbuf.at[s, kk, pl.ds(r, 1)], sem.at[s]).start(priority=kk)

    @pl.when(i == 0)
    def _():
        issue(0, 0)

    for cur in range(2):
        pl.when((i + 1 < n) & (slot == cur))(functools.partial(issue, i + 1, 1 - cur))

    for kk in range(2):
        pltpu.make_async_copy(rows_hbm.at[pl.ds(0, tm)], rbuf.at[slot, kk], sem.at[slot]).wait()
    info = info_ref[...]
    x = (x1_ref[...] + info[:, 2:3] * _unpack_bf16_pairs(rbuf[slot, 0])
         + info[:, 3:4] * _unpack_bf16_pairs(rbuf[slot, 1]))
    y_ref[...] = x * lax.rsqrt(jnp.mean(x * x, axis=-1, keepdims=True) + NORM_EPS) * gain_ref[...]


def _final(dest, x1_all, info, gain, out_rows, *, tok_offset, n_tok):
    tm = FINAL_TM
    off = tok_offset // tm
    grid_spec = pltpu.PrefetchScalarGridSpec(
        num_scalar_prefetch=1,
        grid=(n_tok // tm,),
        in_specs=[
            pl.BlockSpec((tm, D_MODEL), lambda i, d: (i + off, 0)),
            pl.BlockSpec((tm, LANES), lambda i, d: (i + off, 0)),
            pl.BlockSpec((1, D_MODEL), lambda i, d: (0, 0)),
            pl.BlockSpec(memory_space=pl.ANY),
        ],
        out_specs=pl.BlockSpec((tm, D_MODEL), lambda i, d: (i, 0)),
        scratch_shapes=[pltpu.VMEM((2, 2, tm, D_MODEL // 2), jnp.uint32), pltpu.SemaphoreType.DMA((2,))],
    )
    return pl.pallas_call(
        functools.partial(_final_kernel, tok_offset=tok_offset),
        out_shape=jax.ShapeDtypeStruct((n_tok, D_MODEL), F32),
        grid_spec=grid_spec,
        compiler_params=_cparams(("arbitrary",)),
        name="final",
    )(dest, x1_all, info, gain, out_rows)


def _rot_table(pos, n_rows):
    half = RET_KEY_DIM // 2
    theta = 10000.0 ** (-jnp.linspace(0.0, 1.0, half, dtype=F32))
    ang = pos.astype(F32)[:, None] * theta[None, :]
    cos, sin = jnp.cos(ang), jnp.sin(ang)
    cos128 = jnp.tile(cos, (1, 4))
    sin128 = jnp.tile(jnp.concatenate([-sin, sin], axis=1), (1, 2))
    k_scale = RET_KEY_DIM ** -0.5
    tab = jnp.concatenate([cos128, sin128, cos128 * k_scale, sin128 * k_scale], axis=1)
    return jnp.tile(tab, (n_rows // tab.shape[0], 1))


def kernel(x_prompt, x_sample, cache_diff_k, cache_diff_v, state_retention, norm_mix_gain, w_in, lambda_q1,
           lambda_k1, lambda_q2, lambda_k2, diff_subln_gain, w_ret_out, w_diff_out, w_out, rel_bias_table,
           norm_ffn_gain, w_group_router, b_group_router, w_expert_router, b_expert_router, w_expert_gate,
           w_expert_up, w_expert_down, norm_final_gain):
    B, S, D = x_prompt.shape
    BS, L, _ = x_sample.shape
    past = cache_diff_k.shape[2]
    TP, TS = B * S, BS * L
    T = TP + TS

    w_in_bf = w_in[0].astype(BF16)
    wr_bf = w_ret_out[0].astype(BF16)
    wd_bf = w_diff_out[0].astype(BF16)
    wo_bf = w_out[0].astype(BF16)
    gain_mix = norm_mix_gain[0][None, :]
    gain_ffn = norm_ffn_gain[0][None, :]
    gain_fin = norm_final_gain[None, :]
    gain_sub = diff_subln_gain[0][None, :]
    lam_rows = jnp.zeros((8, LANES), F32).at[0:4, 0:DIFF_HEAD_DIM].set(
        jnp.stack([lambda_q1[0], lambda_k1[0], lambda_q2[0], lambda_k2[0]]))
    rot_p = _rot_table(jnp.arange(S), S)
    rot_s = _rot_table(past + jnp.arange(L), PROJ_TM)

    far = _far_bias(rel_bias_table)[:, None, None]
    qp = ATT_TQ + jnp.arange(ATT_TQ)
    def tile_bias(k_pos):
        bias = (_relative_bias(qp, k_pos, rel_bias_table) - far) * LOG2E
        visible = (k_pos[None, :] // CHUNK) <= (qp[:, None] // CHUNK)
        bias = jnp.swapaxes(jnp.where(visible[None], bias, NEG_INF), 1, 2)
        return jnp.concatenate([bias, bias], axis=2)
    bias_prev = tile_bias(jnp.arange(ATT_TK))
    bias_diag = tile_bias(ATT_TQ + jnp.arange(ATT_TK))
    q_pos_s = past + jnp.arange(L)
    assert SAMPLE_NEAR - L >= MAX_DISTANCE
    b_last = (_relative_bias(q_pos_s, past - SAMPLE_NEAR + jnp.arange(SAMPLE_NEAR), rel_bias_table) - far) * LOG2E
    bias_past = jnp.concatenate([b_last, b_last], axis=1)
    b_new = (_relative_bias(q_pos_s, q_pos_s, rel_bias_table) - far) * LOG2E
    bias_new = jnp.concatenate([b_new, b_new], axis=1)

    xp2 = x_prompt.reshape(TP, D)
    xs2 = x_sample.reshape(TS, D)
    z3p, gp, gatesp, v32p, qt, vt5, kt32 = _proj(xp2, gain_mix, w_in_bf, rot_p, seq=S, transposed_k=True)
    z3s, gs, gatess, v32s, k32s = _proj(xs2, gain_mix, w_in_bf, rot_s, seq=L, transposed_k=False)

    zero_state = jnp.zeros((B, N_HEADS, RET_KEY_DIM, HEAD_V), F32)
    orp, ret_p = _retention(z3p, zero_state, batch=B, seq=S, chunk=256)
    ors, ret_s = _retention(z3s, state_retention[0], batch=BS, seq=L, chunk=L)

    gain_col = jnp.broadcast_to(diff_subln_gain[0][:, None], (HEAD_V, ATT_TQ))
    odp = _attn_prompt(z3p, qt, vt5, bias_prev, bias_diag, lam_rows, gain_col, batch=B, seq=S)
    kc = jnp.transpose(cache_diff_k[0], (0, 2, 3, 4, 1)).reshape(BS, N_HEADS, LANES, past)
    vc = cache_diff_v[0].reshape(BS, past * N_HEADS, LANES)
    ods = _attn_sample(z3s, kc, vc, bias_past, bias_new, lam_rows, gain_sub, batch=BS, n_q=L, past=past)

    mgp = _post_a(orp, gp, odp, gatesp, wr_bf, wd_bf)
    mgs = _post_a(ors, gs, ods, gatess, wr_bf, wd_bf)

    w_rt = jnp.zeros((D, LANES), F32)
    w_rt = w_rt.at[:, 0:N_GROUPS].set(w_group_router[0])
    w_rt = w_rt.at[:, N_GROUPS:N_GROUPS + N_EXPERTS].set(
        jnp.transpose(w_expert_router[0], (1, 0, 2)).reshape(D, N_EXPERTS))
    w_rt_hi = w_rt.astype(BF16)
    w_rt_lo = (w_rt - w_rt_hi.astype(F32)).astype(BF16)
    w_router2 = jnp.concatenate([w_rt_hi, w_rt_lo], axis=1)
    bias_row = jnp.zeros((1, LANES), F32)
    bias_row = bias_row.at[0, 0:N_GROUPS].set(b_group_router[0])
    bias_row = bias_row.at[0, N_GROUPS:N_GROUPS + N_EXPERTS].set(b_expert_router[0].reshape(-1))

    x1_all, lg_all, h_packed = _post_b(xp2, mgp, xs2, mgs, wo_bf, gain_ffn, w_router2)

    info, counts = _route(lg_all, bias_row)
    counts = counts[0, :N_EXPERTS].astype(jnp.int32)
    padded = (counts + EXPERT_BLOCK - 1) // EXPERT_BLOCK * EXPERT_BLOCK
    pad_end = jnp.cumsum(padded)
    offs = pad_end - padded
    e12 = info[:, 0:2].astype(jnp.int32)
    rank12 = info[:, 4:6].astype(jnp.int32)
    expert_ids = jnp.arange(N_EXPERTS, dtype=jnp.int32)
    offs_e = jnp.sum(jnp.where(e12[:, :, None] == expert_ids, offs, 0), axis=-1)
    dest = (offs_e + rank12).reshape(-1)
    n_assign = 2 * T
    n_rows = -(-n_assign // EXPERT_BLOCK) * EXPERT_BLOCK + N_EXPERTS * EXPERT_BLOCK
    n_blocks = n_rows // EXPERT_BLOCK
    block_start = jnp.arange(n_blocks, dtype=jnp.int32) * EXPERT_BLOCK
    block_expert = jnp.minimum(jnp.sum((pad_end[None, :] <= block_start[:, None]).astype(jnp.int32), axis=1),
                               N_EXPERTS - 1)
    n_used = (pad_end[-1:] // EXPERT_BLOCK).astype(jnp.int32)
    later_used = (expert_ids[None, :] > expert_ids[:, None]) & (counts[None, :] > 0)
    next_of_expert = jnp.min(jnp.where(later_used, expert_ids[None, :], N_EXPERTS), axis=1)
    next_of_expert = jnp.where(next_of_expert == N_EXPERTS, -1, next_of_expert)
    next_expert = jnp.sum(jnp.where(block_expert[:, None] == expert_ids[None, :], next_of_expert[None, :], 0),
                          axis=1).astype(jnp.int32)

    pad_bounds = jnp.concatenate([offs + counts, pad_end]).astype(jnp.int32)
    out_rows = _moe(dest, block_expert, next_expert, pad_bounds, n_used, h_packed,
                    w_expert_gate[0], w_expert_up[0], w_expert_down[0])

    y_p = _final(dest, x1_all, info, gain_fin, out_rows, tok_offset=0, n_tok=TP)
    y_s = _final(dest, x1_all, info, gain_fin, out_rows, tok_offset=TP, n_tok=TS)

    new_k_p = jnp.transpose(kt32, (0, 1, 5, 2, 3, 4))
    new_v_p = v32p.reshape(1, B, S, N_HEADS, HEAD_V)
    new_k_s = k32s.reshape(1, BS, L, N_HEADS, 2, DIFF_HEAD_DIM)
    new_v_s = v32s.reshape(1, BS, L, N_HEADS, HEAD_V)
    return (y_p.reshape(B, S, D), y_s.reshape(BS, L, D), new_k_p, new_v_p, ret_p[None],
            new_k_s, new_v_s, ret_s[None])
```
